```python
import math
import jax, jax.numpy as jnp
from jax import lax
import numpy as np


D_MODEL = 1024
BATCH = 2
SEQ = 8192
DEPTH = 2

N_EVEN = (DEPTH + 1) // 2
N_ODD = DEPTH // 2
RMS_EPS = 1e-6
SSM_GROUP = 16
SSM_GROUPS = D_MODEL // SSM_GROUP
SSM_STATE = 64
SSM_CHUNK = 128
DT_MIN = 1e-3
DT_MAX = 1e-1
MLA_HEADS = 16
MLA_NOPE = 64
MLA_ROPE = 32
MLA_V = 64
MLA_QK = MLA_NOPE + MLA_ROPE
MLA_Q_RANK = 384
MLA_KV_RANK = 256
ROPE_THETA = 10000.0
Q_BLOCK = 128
FFN_DIM = 2816
N_EXPERTS = 8
TOP_K = 2
EXPERT_DIM = 1792

kernel_name = 'hybrid_s5_mla_moe_sandwich_adaln'


def rms_norm(x, g):
    xf = x.astype(jnp.float32)
    y = xf * lax.rsqrt(jnp.mean(xf * xf, axis=-1, keepdims=True) + RMS_EPS)
    return (y * g.astype(jnp.float32)).astype(x.dtype)


def modulate(h, shift, scale):
    return h * (1 + scale[:, None, :]) + shift[:, None, :]


def cmul(ar, ai, br, bi):
    return ar * br - ai * bi, ar * bi + ai * br


def s5_mixer(u, a_re, a_im, log_dt, b_re, b_im, c_re, c_im, d, w_glu):
    bsz, seq, _ = u.shape
    f32 = jnp.float32
    a_re = a_re.astype(f32)
    a_im = a_im.astype(f32)
    dt = jnp.exp(log_dt.astype(f32))[:, None]
    mag = jnp.exp(a_re * dt)
    ab_re, ab_im = mag * jnp.cos(a_im * dt), mag * jnp.sin(a_im * dt)
    den = a_re * a_re + a_im * a_im
    num_re, num_im = ab_re - 1.0, ab_im
    f_re = (num_re * a_re + num_im * a_im) / den
    f_im = (num_im * a_re - num_re * a_im) / den
    bb_re, bb_im = cmul(f_re[..., None], f_im[..., None], b_re.astype(f32), b_im.astype(f32))
    c_re = c_re.astype(f32)
    c_im = c_im.astype(f32)
    n_chunks = seq // SSM_CHUNK
    uf = u.astype(f32)
    u_chunks = uf.reshape(bsz, n_chunks, SSM_CHUNK, SSM_GROUPS, SSM_GROUP).transpose(1, 2, 0, 3, 4)
    a_el_re = jnp.broadcast_to(ab_re, (SSM_CHUNK, 1, SSM_GROUPS, SSM_STATE))
    a_el_im = jnp.broadcast_to(ab_im, (SSM_CHUNK, 1, SSM_GROUPS, SSM_STATE))

    def combine(e1, e2):
        a1r, a1i, b1r, b1i = e1
        a2r, a2i, b2r, b2i = e2
        ar, ai = cmul(a2r, a2i, a1r, a1i)
        br, bi = cmul(a2r, a2i, b1r, b1i)
        return ar, ai, br + b2r, bi + b2i

    def chunk_step(carry, u_c):
        h0r, h0i = carry
        bu_re = jnp.einsum('tbgh,gph->tbgp', u_c, bb_re)
        bu_im = jnp.einsum('tbgh,gph->tbgp', u_c, bb_im)
        acr, aci, hr, hi = lax.associative_scan(combine, (a_el_re, a_el_im, bu_re, bu_im), axis=0)
        pr, pi = cmul(acr, aci, h0r[None], h0i[None])
        hr = hr + pr
        hi = hi + pi
        y = jnp.einsum('tbgp,ghp->tbgh', hr, c_re) - jnp.einsum('tbgp,ghp->tbgh', hi, c_im)
        return (hr[-1], hi[-1]), y

    h_init = (jnp.zeros((bsz, SSM_GROUPS, SSM_STATE), f32), jnp.zeros((bsz, SSM_GROUPS, SSM_STATE), f32))
    _, ys = lax.scan(chunk_step, h_init, u_chunks)
    y = ys.transpose(2, 0, 1, 3, 4).reshape(bsz, seq, D_MODEL)
    y = (y + d.astype(f32) * uf).astype(u.dtype)
    z = jax.nn.gelu(y)
    gl = z @ w_glu
    return gl[..., :D_MODEL] * jax.nn.sigmoid(gl[..., D_MODEL:])


def rope(x, cos, sin):
    half = MLA_ROPE // 2
    xf = x.astype(jnp.float32)
    x1, x2 = xf[..., :half], xf[..., half:]
    return jnp.concatenate([x1 * cos - x2 * sin, x1 * sin + x2 * cos], axis=-1).astype(x.dtype)


def mla_mixer(h, positions, w_in, q_norm, kv_norm, w_uq, w_ukv, w_o):
    bsz, seq, _ = h.shape
    proj = h @ w_in
    cq = rms_norm(proj[..., :MLA_Q_RANK], q_norm)
    ckv = rms_norm(proj[..., MLA_Q_RANK:MLA_Q_RANK + MLA_KV_RANK], kv_norm)
    k_rope_in = proj[..., MLA_Q_RANK + MLA_KV_RANK:]
    q = (cq @ w_uq).reshape(bsz, seq, MLA_HEADS, MLA_QK)
    kv = (ckv @ w_ukv).reshape(bsz, seq, MLA_HEADS, MLA_NOPE + MLA_V)
    k_nope, v = kv[..., :MLA_NOPE], kv[..., MLA_NOPE:]
    half = MLA_ROPE // 2
    inv_freq = ROPE_THETA ** (-jnp.arange(half, dtype=jnp.float32) / half)
    ang = positions.astype(jnp.float32)[..., None, None] * inv_freq
    cos, sin = jnp.cos(ang), jnp.sin(ang)
    q = jnp.concatenate([q[..., :MLA_NOPE], rope(q[..., MLA_NOPE:], cos, sin)], axis=-1)
    k_rope = rope(k_rope_in[:, :, None, :], cos, sin)
    k = jnp.concatenate([k_nope, jnp.broadcast_to(k_rope, (bsz, seq, MLA_HEADS, MLA_ROPE))], axis=-1)
    n_blocks = seq // Q_BLOCK
    q_blocks = q.reshape(bsz, n_blocks, Q_BLOCK, MLA_HEADS, MLA_QK).transpose(1, 0, 2, 3, 4)
    key_idx = jnp.arange(seq)
    sm_scale = MLA_QK ** -0.5

    def attend(args):
        q_blk, blk = args
        q_idx = blk * Q_BLOCK + jnp.arange(Q_BLOCK)
        s = jnp.einsum('bqhd,bkhd->bhqk', q_blk, k, preferred_element_type=jnp.float32) * sm_scale
        s = jnp.where(key_idx[None, :] <= q_idx[:, None], s, -jnp.inf)
        p = jax.nn.softmax(s, axis=-1).astype(v.dtype)
        return jnp.einsum('bhqk,bkhd->bqhd', p, v)

    o = lax.map(attend, (q_blocks, jnp.arange(n_blocks)))
    o = o.transpose(1, 0, 2, 3, 4).reshape(bsz, seq, MLA_HEADS * MLA_V)
    return o @ w_o


def dense_swiglu(h, w_in, w_out):
    gu = h @ w_in
    return (jax.nn.silu(gu[..., :FFN_DIM]) * gu[..., FFN_DIM:]) @ w_out


def moe_swiglu(h, w_router, b_router, w_in, w_out):
    bsz, seq, d = h.shape
    t = h.reshape(-1, d)
    logits = (t @ w_router).astype(jnp.float32) + b_router.astype(jnp.float32)
    top_val, top_idx = lax.top_k(logits, TOP_K)
    top_w = jax.nn.softmax(top_val, axis=-1)
    comb = jnp.sum(jax.nn.one_hot(top_idx, N_EXPERTS, dtype=jnp.float32) * top_w[..., None], axis=1)
    out = jnp.zeros_like(t)
    for e in range(N_EXPERTS):
        gu = t @ w_in[e]
        y_e = (jax.nn.silu(gu[:, :EXPERT_DIM]) * gu[:, EXPERT_DIM:]) @ w_out[e]
        out = out + comb[:, e:e + 1].astype(t.dtype) * y_e
    return out.reshape(bsz, seq, d)


def setup_inputs(seed: int = 0) -> dict:
    key = jax.random.key(seed)
    ks = iter(jax.random.split(key, 32))
    f32 = jnp.float32

    def nrm(shape, scale):
        return jax.random.normal(next(ks), shape, f32) * scale

    x = nrm((BATCH, SEQ, D_MODEL), 1.0)
    c = nrm((BATCH, D_MODEL), 1.0)
    offset = jax.random.randint(next(ks), (BATCH, 1), 0, 4096, dtype=jnp.int32)
    positions = offset + jnp.arange(SEQ, dtype=jnp.int32)[None, :]
    norm_g = 1.0 + nrm((DEPTH, 4, D_MODEL), 0.01)
    w_ada = nrm((DEPTH, D_MODEL, 6 * D_MODEL), 0.5 * D_MODEL ** -0.5)
    b_ada = nrm((DEPTH, 6 * D_MODEL), 0.02)
    n_idx = jnp.arange(SSM_STATE, dtype=f32)
    ssm_a_re = -0.5 + nrm((N_EVEN, SSM_GROUPS, SSM_STATE), 0.01)
    ssm_a_im = math.pi * n_idx + nrm((N_EVEN, SSM_GROUPS, SSM_STATE), 0.01)
    ssm_log_dt = jax.random.uniform(next(ks), (N_EVEN, SSM_GROUPS), f32, math.log(DT_MIN), math.log(DT_MAX))
    ssm_b_re = nrm((N_EVEN, SSM_GROUPS, SSM_STATE, SSM_GROUP), (2 * SSM_GROUP) ** -0.5)
    ssm_b_im = nrm((N_EVEN, SSM_GROUPS, SSM_STATE, SSM_GROUP), (2 * SSM_GROUP) ** -0.5)
    ssm_c_re = nrm((N_EVEN, SSM_GROUPS, SSM_GROUP, SSM_STATE), (2 * SSM_STATE) ** -0.5)
    ssm_c_im = nrm((N_EVEN, SSM_GROUPS, SSM_GROUP, SSM_STATE), (2 * SSM_STATE) ** -0.5)
    ssm_d = nrm((N_EVEN, D_MODEL), 1.0)
    ssm_w_glu = nrm((N_EVEN, D_MODEL, 2 * D_MODEL), D_MODEL ** -0.5)
    ffn_w_in = nrm((N_EVEN, D_MODEL, 2 * FFN_DIM), D_MODEL ** -0.5)
    ffn_w_out = nrm((N_EVEN, FFN_DIM, D_MODEL), FFN_DIM ** -0.5)
    mla_w_in = nrm((N_ODD, D_MODEL, MLA_Q_RANK + MLA_KV_RANK + MLA_ROPE), D_MODEL ** -0.5)
    mla_q_norm = 1.0 + nrm((N_ODD, MLA_Q_RANK), 0.01)
    mla_kv_norm = 1.0 + nrm((N_ODD, MLA_KV_RANK), 0.01)
    mla_w_uq = nrm((N_ODD, MLA_Q_RANK, MLA_HEADS * MLA_QK), MLA_Q_RANK ** -0.5)
    mla_w_ukv = nrm((N_ODD, MLA_KV_RANK, MLA_HEADS * (MLA_NOPE + MLA_V)), MLA_KV_RANK ** -0.5)
    mla_w_o = nrm((N_ODD, MLA_HEADS * MLA_V, D_MODEL), (MLA_HEADS * MLA_V) ** -0.5)
    moe_w_router = nrm((N_ODD, D_MODEL, N_EXPERTS), D_MODEL ** -0.5)
    moe_b_router = nrm((N_ODD, N_EXPERTS), 0.01)
    moe_w_in = nrm((N_ODD, N_EXPERTS, D_MODEL, 2 * EXPERT_DIM), D_MODEL ** -0.5)
    moe_w_out = nrm((N_ODD, N_EXPERTS, EXPERT_DIM, D_MODEL), EXPERT_DIM ** -0.5)
    return {'x': x, 'c': c, 'positions': positions, 'norm_g': norm_g, 'w_ada': w_ada, 'b_ada': b_ada,
            'ssm_a_re': ssm_a_re, 'ssm_a_im': ssm_a_im, 'ssm_log_dt': ssm_log_dt,
            'ssm_b_re': ssm_b_re, 'ssm_b_im': ssm_b_im, 'ssm_c_re': ssm_c_re, 'ssm_c_im': ssm_c_im,
            'ssm_d': ssm_d, 'ssm_w_glu': ssm_w_glu, 'ffn_w_in': ffn_w_in, 'ffn_w_out': ffn_w_out,
            'mla_w_in': mla_w_in, 'mla_q_norm': mla_q_norm, 'mla_kv_norm': mla_kv_norm,
            'mla_w_uq': mla_w_uq, 'mla_w_ukv': mla_w_ukv, 'mla_w_o': mla_w_o,
            'moe_w_router': moe_w_router, 'moe_b_router': moe_b_router,
            'moe_w_in': moe_w_in, 'moe_w_out': moe_w_out}


def reference(x, c, positions, norm_g, w_ada, b_ada,
              ssm_a_re, ssm_a_im, ssm_log_dt, ssm_b_re, ssm_b_im, ssm_c_re, ssm_c_im,
              ssm_d, ssm_w_glu, ffn_w_in, ffn_w_out,
              mla_w_in, mla_q_norm, mla_kv_norm, mla_w_uq, mla_w_ukv, mla_w_o,
              moe_w_router, moe_b_router, moe_w_in, moe_w_out):
    c_act = jax.nn.silu(c)
    for i in range(DEPTH):
        j = i // 2
        mod = c_act @ w_ada[i] + b_ada[i]
        sh1, sc1, g1 = mod[:, :D_MODEL], mod[:, D_MODEL:2 * D_MODEL], mod[:, 2 * D_MODEL:3 * D_MODEL]
        sh2, sc2, g2 = mod[:, 3 * D_MODEL:4 * D_MODEL], mod[:, 4 * D_MODEL:5 * D_MODEL], mod[:, 5 * D_MODEL:]
        h = modulate(rms_norm(x, norm_g[i, 0]), sh1, sc1)
        if i % 2 == 0:
            h = s5_mixer(h, ssm_a_re[j], ssm_a_im[j], ssm_log_dt[j], ssm_b_re[j], ssm_b_im[j],
                         ssm_c_re[j], ssm_c_im[j], ssm_d[j], ssm_w_glu[j])
        else:
            h = mla_mixer(h, positions, mla_w_in[j], mla_q_norm[j], mla_kv_norm[j],
                          mla_w_uq[j], mla_w_ukv[j], mla_w_o[j])
        x = x + g1[:, None, :] * rms_norm(h, norm_g[i, 1])
        h = modulate(rms_norm(x, norm_g[i, 2]), sh2, sc2)
        if i % 2 == 0:
            h = dense_swiglu(h, ffn_w_in[j], ffn_w_out[j])
        else:
            h = moe_swiglu(h, moe_w_router[j], moe_b_router[j], moe_w_in[j], moe_w_out[j])
        x = x + g2[:, None, :] * rms_norm(h, norm_g[i, 3])
    return x
```

```python
import functools
import math

import jax
import jax.numpy as jnp
from jax import lax
from jax.experimental import pallas as pl
from jax.experimental.pallas import tpu as pltpu

F32 = jnp.float32
BF16 = jnp.bfloat16
HIGHEST = lax.Precision.HIGHEST

RMS_EPS = 1e-6
LANES = 128
SSM_GROUP = 16
SSM_STATE = 64
SSM_L = 16
SSM_LG = LANES // SSM_GROUP
MLA_HEADS = 16
MLA_NOPE = 64
MLA_ROPE = 32
MLA_V = 64
MLA_QK = MLA_NOPE + MLA_ROPE
MLA_Q_RANK = 384
MLA_KV_RANK = 256
ROPE_THETA = 10000.0
N_EXPERTS = 8
VMEM_LIMIT = 56 * 1024 * 1024


def _cparams(sem):
    return pltpu.CompilerParams(dimension_semantics=sem, vmem_limit_bytes=VMEM_LIMIT)


def _rms(x, g):
    return x * lax.rsqrt(jnp.mean(x * x, axis=-1, keepdims=True) + RMS_EPS) * g


def _sigmoid(x):
    return 1.0 / (1.0 + jnp.exp(-x))


def _dot(a, b):
    return jnp.dot(a, b, preferred_element_type=F32)


def _ada_kernel(c_ref, w_ref, b_ref, o_ref):
    c = c_ref[...]
    c_act = c * _sigmoid(c)
    o_ref[...] = jnp.dot(c_act, w_ref[...], preferred_element_type=F32, precision=HIGHEST) + b_ref[...]


def _ada(c, w_ada, b_ada):
    depth, d, _ = w_ada.shape
    bsz = c.shape[0]
    rows = 8
    c_pad = jnp.pad(c, ((0, rows - bsz), (0, 0)))
    out = pl.pallas_call(
        _ada_kernel,
        grid=(depth, 6),
        in_specs=[
            pl.BlockSpec((rows, d), lambda i, j: (0, 0)),
            pl.BlockSpec((None, d, d), lambda i, j: (i, 0, j)),
            pl.BlockSpec((None, None, 1, d), lambda i, j: (i, j, 0, 0)),
        ],
        out_specs=pl.BlockSpec((None, None, rows, d), lambda i, j: (i, j, 0, 0)),
        out_shape=jax.ShapeDtypeStruct((depth, 6, rows, d), F32),
        compiler_params=_cparams(("arbitrary", "arbitrary")),
        name="ada_mod",
    )(c_pad, w_ada, b_ada.reshape(depth, 6, 1, d))
    return out[:, :, :bsz, :].transpose(0, 2, 1, 3)


def _ssm_pre_kernel(x_ref, mod_ref, g_ref, o_ref, scr):
    tm = x_ref.shape[0]
    n = tm // SSM_L
    u = _rms(x_ref[...], g_ref[0:1, :]) * (1.0 + mod_ref[1:2, :]) + mod_ref[0:1, :]
    n_lg = o_ref.shape[0]
    for lg in range(n_lg):
        scr[lg] = u[:, lg * LANES:(lg + 1) * LANES]
    for s in range(SSM_L):
        for lg in range(n_lg):
            o_ref[lg, :, s * LANES:(s + 1) * LANES] = scr[lg, pl.ds(s, n, stride=SSM_L), :].astype(BF16)


def _ssm_pre(x, mod_l, g_l, tm):
    bsz, seq, d = x.shape
    n_lg = d // LANES
    return pl.pallas_call(
        _ssm_pre_kernel,
        grid=(bsz, seq // tm),
        in_specs=[
            pl.BlockSpec((None, tm, d), lambda b, i: (b, i, 0)),
            pl.BlockSpec((None, 6, d), lambda b, i: (b, 0, 0)),
            pl.BlockSpec((4, d), lambda b, i: (0, 0)),
        ],
        out_specs=pl.BlockSpec((None, n_lg, tm // SSM_L, SSM_L * LANES), lambda b, i: (b, 0, i, 0)),
        out_shape=jax.ShapeDtypeStruct((bsz, n_lg, seq // SSM_L, SSM_L * LANES), BF16),
        scratch_shapes=[pltpu.VMEM((n_lg, tm, LANES), F32)],
        compiler_params=_cparams(("parallel", "parallel")),
        name="ssm_pre",
    )(x, mod_l, g_l)


def _gelu_tanh(y):
    return 0.5 * y * (1.0 + jnp.tanh(math.sqrt(2.0 / math.pi) * (y + 0.044715 * (y * y * y))))


def _ssm_kernel(u_ref, win_ref, m8_ref, mx_ref, wout_ref, a_ref, d_ref, o_ref, v_scr, h_scr):
    n_chunks = u_ref.shape[0]
    half = (SSM_L // 2) * LANES
    ns = a_ref.shape[1]
    uc = u_ref[...]
    v_scr[...] = _dot(uc, win_ref[...])
    ar = a_ref[0:1, :]
    ai = a_ref[1:2, :]

    def step(c, carry):
        hr, hi = carry
        h_scr[pl.ds(c, 1), pl.ds(0, ns)] = hr
        h_scr[pl.ds(c, 1), pl.ds(ns, ns)] = hi
        vr = v_scr[pl.ds(c, 1), pl.ds(0, ns)]
        vi = v_scr[pl.ds(c, 1), pl.ds(ns, ns)]
        return ar * hr - ai * hi + vr, ar * hi + ai * hr + vi

    zero = jnp.zeros((1, ns), F32)
    lax.fori_loop(0, n_chunks, step, (zero, zero))
    y = _dot(h_scr[...].astype(BF16), wout_ref[...])
    u_lo = uc[:, :half]
    u_hi = uc[:, half:]
    m8 = m8_ref[...]
    y_lo = y[:, :half] + _dot(u_lo, m8)
    y_hi = y[:, half:] + _dot(u_hi, m8) + _dot(u_lo, mx_ref[...])
    dd = d_ref[...]
    o_ref[:, :half] = _gelu_tanh(y_lo + dd[:, :half] * u_lo.astype(F32)).astype(BF16)
    o_ref[:, half:] = _gelu_tanh(y_hi + dd[:, half:] * u_hi.astype(F32)).astype(BF16)


def _ssm(ucat, win, m8, mx, wout, a16, dcat):
    bsz, n_lg, n_chunks, width = ucat.shape
    ns2 = win.shape[-1]
    half = width // 2
    return pl.pallas_call(
        _ssm_kernel,
        grid=(n_lg, bsz),
        in_specs=[
            pl.BlockSpec((None, None, n_chunks, width), lambda g, b: (b, g, 0, 0)),
            pl.BlockSpec((None, width, ns2), lambda g, b: (g, 0, 0)),
            pl.BlockSpec((None, half, half), lambda g, b: (g, 0, 0)),
            pl.BlockSpec((None, half, half), lambda g, b: (g, 0, 0)),
            pl.BlockSpec((None, ns2, width), lambda g, b: (g, 0, 0)),
            pl.BlockSpec((None, 2, ns2 // 2), lambda g, b: (g, 0, 0)),
            pl.BlockSpec((None, 1, width), lambda g, b: (g, 0, 0)),
        ],
        out_specs=pl.BlockSpec((None, None, n_chunks, width), lambda g, b: (b, g, 0, 0)),
        out_shape=jax.ShapeDtypeStruct(ucat.shape, BF16),
        scratch_shapes=[pltpu.VMEM((n_chunks, ns2), F32), pltpu.VMEM((n_chunks, ns2), F32)],
        compiler_params=_cparams(("parallel", "parallel")),
        name="ssm_scan",
    )(ucat, win, m8, mx, wout, a16, dcat)


def _ssm_post_kernel(z_ref, x_ref, w_ref, mod_ref, g_ref, o_ref, zp, rp):
    n_lg, n, _ = z_ref.shape
    d = x_ref.shape[1]
    for s in range(SSM_L):
        for lg in range(n_lg):
            zp[s * n:(s + 1) * n, lg * LANES:(lg + 1) * LANES] = z_ref[lg, :, s * LANES:(s + 1) * LANES]
    gl = _dot(zp[...], w_ref[...])
    h = gl[:, :d] * _sigmoid(gl[:, d:])
    r = mod_ref[2:3, :] * _rms(h, g_ref[1:2, :])
    for lg in range(n_lg):
        rp[lg] = r[:, lg * LANES:(lg + 1) * LANES]
    sub = 8
    for c in range(n):
        for k in range(SSM_L // sub):
            rows = pl.ds(c * SSM_L + k * sub, sub)
            for lg in range(n_lg):
                cols = slice(lg * LANES, (lg + 1) * LANES)
                o_ref[rows, cols] = x_ref[rows, cols] + rp[lg, pl.ds(k * sub * n + c, sub, stride=n), :]


def _ssm_post(zcat, x, w_glu, mod_l, g_l, tm):
    bsz, seq, d = x.shape
    n_lg = d // LANES
    return pl.pallas_call(
        _ssm_post_kernel,
        grid=(bsz, seq // tm),
        in_specs=[
            pl.BlockSpec((None, n_lg, tm // SSM_L, SSM_L * LANES), lambda b, i: (b, 0, i, 0)),
            pl.BlockSpec((None, tm, d), lambda b, i: (b, i, 0)),
            pl.BlockSpec((d, 2 * d), lambda b, i: (0, 0)),
            pl.BlockSpec((None, 6, d), lambda b, i: (b, 0, 0)),
            pl.BlockSpec((4, d), lambda b, i: (0, 0)),
        ],
        out_specs=pl.BlockSpec((None, tm, d), lambda b, i: (b, i, 0)),
        out_shape=jax.ShapeDtypeStruct(x.shape, F32),
        scratch_shapes=[pltpu.VMEM((tm, d), BF16), pltpu.VMEM((n_lg, tm, LANES), F32)],
        compiler_params=_cparams(("parallel", "parallel")),
        name="ssm_post",
    )(zcat, x, w_glu, mod_l, g_l)


def _ssm_weights(a_re, a_im, log_dt, b_re, b_im, c_re, c_im, d_skip):
    n_groups = a_re.shape[0]
    n_lg = n_groups // SSM_LG
    p, hh, ll = SSM_STATE, SSM_GROUP, SSM_L
    dt = jnp.exp(log_dt)[:, None]
    k = jnp.arange(ll + 1, dtype=F32)[:, None, None]
    mag = jnp.exp(k * (a_re * dt))
    pw_re = mag * jnp.cos(k * (a_im * dt))
    pw_im = mag * jnp.sin(k * (a_im * dt))
    den = a_re * a_re + a_im * a_im
    num_re, num_im = pw_re[1] - 1.0, pw_im[1]
    f_re = (num_re * a_re + num_im * a_im) / den
    f_im = (num_im * a_re - num_re * a_im) / den
    bb_re = f_re[..., None] * b_re - f_im[..., None] * b_im
    bb_im = f_re[..., None] * b_im + f_im[..., None] * b_re
    eye = jnp.eye(SSM_LG, dtype=F32)

    def grp(x):
        return x.reshape((x.shape[0], n_lg, SSM_LG) + x.shape[2:])

    pb_re = pw_re[:ll, :, :, None] * bb_re[None] - pw_im[:ll, :, :, None] * bb_im[None]
    pb_im = pw_re[:ll, :, :, None] * bb_im[None] + pw_im[:ll, :, :, None] * bb_re[None]
    pb = jnp.stack([pb_re[::-1], pb_im[::-1]], axis=1)
    pb = pb.reshape(ll, 2, n_lg, SSM_LG, p, hh)
    win = jnp.einsum('srlgph,gk->lsghrkp', pb, eye).reshape(n_lg, ll * LANES, 2 * SSM_LG * p)
    cp_re = c_re[None] * pw_re[1:, :, None, :] - c_im[None] * pw_im[1:, :, None, :]
    cp_im = c_re[None] * pw_im[1:, :, None, :] + c_im[None] * pw_re[1:, :, None, :]
    cp = jnp.stack([cp_re, -cp_im], axis=1).reshape(ll, 2, n_lg, SSM_LG, hh, p)
    wout = jnp.einsum('srlghp,gk->lrgpskh', cp, eye).reshape(n_lg, 2 * SSM_LG * p, ll * LANES)
    kt = (jnp.einsum('ghp,tgp,gpi->tghi', c_re, pw_re[:ll], bb_re, precision=HIGHEST)
          - jnp.einsum('ghp,tgp,gpi->tghi', c_re, pw_im[:ll], bb_im, precision=HIGHEST)
          - jnp.einsum('ghp,tgp,gpi->tghi', c_im, pw_re[:ll], bb_im, precision=HIGHEST)
          - jnp.einsum('ghp,tgp,gpi->tghi', c_im, pw_im[:ll], bb_re, precision=HIGHEST))
    kt = jnp.concatenate([kt, jnp.zeros_like(kt)], axis=0)
    hl = ll // 2
    s_in = jnp.arange(hl)[:, None]
    s_out = jnp.arange(hl)[None, :]
    lag8 = (s_out - s_in) % (2 * ll)
    lagx = s_out - s_in + hl

    def toeplitz(lag):
        t = kt[lag].reshape(hl, hl, n_lg, SSM_LG, hh, hh)
        return jnp.einsum('ablgoi,gk->lagibko', t, eye).reshape(n_lg, hl * LANES, hl * LANES)

    m8 = toeplitz(lag8)
    mx = toeplitz(lagx)
    a16 = jnp.stack([pw_re[ll], pw_im[ll]], axis=0).reshape(2, n_lg, SSM_LG * p).transpose(1, 0, 2)
    dcat = jnp.tile(d_skip.reshape(n_lg, 1, LANES), (1, 1, ll))
    return win.astype(BF16), m8.astype(BF16), mx.astype(BF16), wout.astype(BF16), a16, dcat


def _ffn_kernel(x_ref, wg_ref, wu_ref, wo_ref, mod_ref, g_ref, o_ref, h_scr, acc):
    k = pl.program_id(2)

    @pl.when(k == 0)
    def _():
        h = _rms(x_ref[...], g_ref[2:3, :]) * (1.0 + mod_ref[4:5, :]) + mod_ref[3:4, :]
        h_scr[...] = h.astype(BF16)
        acc[...] = jnp.zeros_like(acc)

    h = h_scr[...]
    g = _dot(h, wg_ref[...])
    u = _dot(h, wu_ref[...])
    act = (g * _sigmoid(g) * u).astype(BF16)
    acc[...] += _dot(act, wo_ref[...])

    @pl.when(k == pl.num_programs(2) - 1)
    def _():
        o_ref[...] = x_ref[...] + mod_ref[5:6, :] * _rms(acc[...], g_ref[3:4, :])


def _ffn(x, w_in, w_out, mod_l, g_l, tm, tf):
    bsz, seq, d = x.shape
    f = w_out.shape[0]
    nk = f // tf
    return pl.pallas_call(
        _ffn_kernel,
        grid=(bsz, seq // tm, nk),
        in_specs=[
            pl.BlockSpec((None, tm, d), lambda b, i, k: (b, i, 0)),
            pl.BlockSpec((d, tf), lambda b, i, k: (0, k)),
            pl.BlockSpec((d, tf), lambda b, i, k: (0, k + nk)),
            pl.BlockSpec((tf, d), lambda b, i, k: (k, 0)),
            pl.BlockSpec((None, 6, d), lambda b, i, k: (b, 0, 0)),
            pl.BlockSpec((4, d), lambda b, i, k: (0, 0)),
        ],
        out_specs=pl.BlockSpec((None, tm, d), lambda b, i, k: (b, i, 0)),
        out_shape=jax.ShapeDtypeStruct(x.shape, F32),
        scratch_shapes=[pltpu.VMEM((tm, d), BF16), pltpu.VMEM((tm, d), F32)],
        compiler_params=_cparams(("parallel", "parallel", "arbitrary")),
        name="ffn",
    )(x, w_in, w_in, w_out, mod_l, g_l)


def _rope_kernel(pos_ref, f_ref, cos_ref, sin_ref):
    ang = pos_ref[...].astype(F32) * f_ref[...]
    cos_ref[...] = jnp.cos(ang)
    sin_ref[...] = jnp.sin(ang)


def _rope_tables(positions):
    bsz, seq = positions.shape
    half = MLA_ROPE // 2
    inv_freq = ROPE_THETA ** (-jnp.arange(half, dtype=F32) / half)
    per_row = LANES // half
    rows = bsz * seq // per_row
    pos_rep = jnp.broadcast_to(positions[..., None], (bsz, seq, half)).reshape(rows, LANES)
    f_rep = jnp.tile(inv_freq, per_row).reshape(1, LANES)
    tr = min(rows, 512)
    cos, sin = pl.pallas_call(
        _rope_kernel,
        grid=(rows // tr,),
        in_specs=[pl.BlockSpec((tr, LANES), lambda i: (i, 0)), pl.BlockSpec((1, LANES), lambda i: (0, 0))],
        out_specs=[pl.BlockSpec((tr, LANES), lambda i: (i, 0))] * 2,
        out_shape=[jax.ShapeDtypeStruct((rows, LANES), F32)] * 2,
        compiler_params=_cparams(("parallel",)),
        name="rope_tables",
    )(pos_rep, f_rep)
    cos = cos.reshape(bsz, seq, half)
    sin = sin.reshape(bsz, seq, half)
    ones = jnp.ones((bsz, seq, MLA_NOPE), F32)
    zpad = jnp.zeros((bsz, seq, LANES - MLA_QK), F32)
    cos_t = jnp.concatenate([ones, cos, cos, zpad], axis=-1)
    sin_t = jnp.concatenate([0.0 * ones, sin, sin, zpad], axis=-1)
    return cos_t, sin_t


def _mla_proj_kernel(x_ref, cos_ref, sin_ref, mod_ref, g_ref, wcq_ref, wckv_ref, wkr_ref, qn_ref, kvn_ref,
                     wq_ref, wqs_ref, wk_ref, wv_ref, perm_ref, exp_ref, q_ref, k_ref, v_ref):
    h = (_rms(x_ref[...], g_ref[0:1, :]) * (1.0 + mod_ref[1:2, :]) + mod_ref[0:1, :]).astype(BF16)
    cq = _rms(_dot(h, wcq_ref[...]), qn_ref[...]).astype(BF16)
    ckv = _rms(_dot(h, wckv_ref[...]), kvn_ref[...]).astype(BF16)
    kr = _dot(h, wkr_ref[...])
    cos = cos_ref[...]
    sin = sin_ref[...]
    q = _dot(cq, wq_ref[...])
    qs = _dot(cq, wqs_ref[...])
    for hd in range(MLA_HEADS):
        sl = slice(hd * LANES, (hd + 1) * LANES)
        q_ref[:, sl] = (q[:, sl] * cos + qs[:, sl] * sin).astype(BF16)
    kr_rot = (kr * cos + _dot(kr.astype(BF16), perm_ref[...]) * sin).astype(BF16)
    k_ref[...] = (_dot(ckv, wk_ref[...]) + _dot(kr_rot, exp_ref[...])).astype(BF16)
    v_ref[...] = _dot(ckv, wv_ref[...]).astype(BF16)


def _mla_proj(x, cos_t, sin_t, mod_l, g_l, wts, tm):
    bsz, seq, d = x.shape
    hw = MLA_HEADS * LANES

    def full(a):
        return pl.BlockSpec(a.shape, lambda b, i: (0,) * a.ndim)

    out_sds = jax.ShapeDtypeStruct((bsz, seq, hw), BF16)
    return pl.pallas_call(
        _mla_proj_kernel,
        grid=(bsz, seq // tm),
        in_specs=[
            pl.BlockSpec((None, tm, d), lambda b, i: (b, i, 0)),
            pl.BlockSpec((None, tm, LANES), lambda b, i: (b, i, 0)),
            pl.BlockSpec((None, tm, LANES), lambda b, i: (b, i, 0)),
            pl.BlockSpec((None, 6, d), lambda b, i: (b, 0, 0)),
            pl.BlockSpec((4, d), lambda b, i: (0, 0)),
        ] + [full(w) for w in wts],
        out_specs=[pl.BlockSpec((None, tm, hw), lambda b, i: (b, i, 0))] * 3,
        out_shape=[out_sds] * 3,
        compiler_params=_cparams(("parallel", "parallel")),
        name="mla_proj",
    )(x, cos_t, sin_t, mod_l, g_l, *wts)


def _mla_weights(w_in, q_norm, kv_norm, w_uq, w_ukv, w_o):
    d = w_in.shape[0]
    half = MLA_ROPE // 2
    pad = LANES - MLA_QK
    w_cq = w_in[:, :MLA_Q_RANK]
    w_ckv = w_in[:, MLA_Q_RANK:MLA_Q_RANK + MLA_KV_RANK]
    w_kr = jnp.pad(w_in[:, MLA_Q_RANK + MLA_KV_RANK:], ((0, 0), (MLA_NOPE, pad)))
    scale = MLA_QK ** -0.5
    wq = w_uq.reshape(MLA_Q_RANK, MLA_HEADS, MLA_QK) * scale
    nope, x1, x2 = wq[..., :MLA_NOPE], wq[..., MLA_NOPE:MLA_NOPE + half], wq[..., MLA_NOPE + half:]
    zp = jnp.zeros((MLA_Q_RANK, MLA_HEADS, pad), F32)
    wq_pad = jnp.concatenate([nope, x1, x2, zp], axis=-1).reshape(MLA_Q_RANK, MLA_HEADS * LANES)
    wq_sw = jnp.concatenate([0.0 * nope, -x2, x1, zp], axis=-1).reshape(MLA_Q_RANK, MLA_HEADS * LANES)
    wkv = w_ukv.reshape(MLA_KV_RANK, MLA_HEADS, MLA_NOPE + MLA_V)
    zk = jnp.zeros((MLA_KV_RANK, MLA_HEADS, LANES - MLA_NOPE), F32)
    wk_pad = jnp.concatenate([wkv[..., :MLA_NOPE], zk], axis=-1).reshape(MLA_KV_RANK, MLA_HEADS * LANES)
    zv = jnp.zeros((MLA_KV_RANK, MLA_HEADS, LANES - MLA_V), F32)
    wv_pad = jnp.concatenate([wkv[..., MLA_NOPE:], zv], axis=-1).reshape(MLA_KV_RANK, MLA_HEADS * LANES)
    lane = jnp.arange(LANES)
    src = jnp.where((lane >= MLA_NOPE) & (lane < MLA_NOPE + half), lane + half,
                    jnp.where((lane >= MLA_NOPE + half) & (lane < MLA_QK), lane - half, -1))
    sign = jnp.where(lane < MLA_NOPE + half, -1.0, 1.0)
    perm = (lane[:, None] == src[None, :]).astype(F32) * sign[None, :]
    is_rope = (lane >= MLA_NOPE) & (lane < MLA_QK)
    expand = jnp.tile((jnp.eye(LANES, dtype=F32) * is_rope[None, :].astype(F32)), (1, MLA_HEADS))
    wo = w_o.reshape(MLA_HEADS, MLA_V, d)
    wo_pad = jnp.concatenate([wo, jnp.zeros((MLA_HEADS, LANES - MLA_V, d), F32)], axis=1).reshape(MLA_HEADS * LANES, d)
    proj_w = (w_cq.astype(BF16), w_ckv.astype(BF16), w_kr.astype(BF16),
              q_norm.reshape(1, -1), kv_norm.reshape(1, -1),
              wq_pad.astype(BF16), wq_sw.astype(BF16), wk_pad.astype(BF16), wv_pad.astype(BF16),
              perm.astype(BF16), expand.astype(BF16))
    return proj_w, wo_pad.astype(BF16)


def _attn_kernel(q_ref, k_ref, v_ref, o_ref, m_scr, l_scr, acc):
    tq = q_ref.shape[0]
    qi = pl.program_id(2)
    q = q_ref[...]
    m_scr[...] = jnp.full_like(m_scr, -jnp.inf)
    l_scr[...] = jnp.zeros_like(l_scr)
    acc[...] = jnp.zeros_like(acc)

    def tile(j, masked):
        start = pl.multiple_of(j * tq, tq)
        k = k_ref[pl.ds(start, tq), :]
        v = v_ref[pl.ds(start, tq), :]
        s = lax.dot_general(q, k, (((1,), (1,)), ((), ())), preferred_element_type=F32)
        if masked:
            row = lax.broadcasted_iota(jnp.int32, s.shape, 0)
            col = lax.broadcasted_iota(jnp.int32, s.shape, 1)
            s = jnp.where(col <= row, s, -jnp.inf)
        m_prev = m_scr[...]
        m_new = jnp.maximum(m_prev, jnp.max(s, axis=-1, keepdims=True))
        alpha = jnp.exp(m_prev - m_new)
        p = jnp.exp(s - m_new)
        l_scr[...] = alpha * l_scr[...] + jnp.sum(p, axis=-1, keepdims=True)
        acc[...] = alpha * acc[...] + _dot(p.astype(BF16), v)
        m_scr[...] = m_new

    def body(j, carry):
        tile(j, False)
        return carry

    lax.fori_loop(0, qi, body, 0)
    tile(qi, True)
    o_ref[...] = (acc[...] / l_scr[...]).astype(BF16)


def _attention(q, k, v, tq):
    bsz, seq, hw = q.shape
    n_heads = hw // LANES
    return pl.pallas_call(
        _attn_kernel,
        grid=(bsz, n_heads, seq // tq),
        in_specs=[
            pl.BlockSpec((None, tq, LANES), lambda b, h, i: (b, i, h)),
            pl.BlockSpec((None, seq, LANES), lambda b, h, i: (b, 0, h)),
            pl.BlockSpec((None, seq, LANES), lambda b, h, i: (b, 0, h)),
        ],
        out_specs=pl.BlockSpec((None, tq, LANES), lambda b, h, i: (b, i, h)),
        out_shape=jax.ShapeDtypeStruct(q.shape, BF16),
        scratch_shapes=[pltpu.VMEM((tq, 1), F32), pltpu.VMEM((tq, 1), F32), pltpu.VMEM((tq, LANES), F32)],
        compiler_params=_cparams(("parallel", "parallel", "arbitrary")),
        name="mla_attention",
    )(q, k, v)


def _attn_out_kernel(a_ref, x_ref, w_ref, mod_ref, g_ref, o_ref):
    h = _dot(a_ref[...], w_ref[...])
    o_ref[...] = x_ref[...] + mod_ref[2:3, :] * _rms(h, g_ref[1:2, :])


def _attn_out(a, x, wo_pad, mod_l, g_l, tm):
    bsz, seq, d = x.shape
    hw = a.shape[-1]
    return pl.pallas_call(
        _attn_out_kernel,
        grid=(bsz, seq // tm),
        in_specs=[
            pl.BlockSpec((None, tm, hw), lambda b, i: (b, i, 0)),
            pl.BlockSpec((None, tm, d), lambda b, i: (b, i, 0)),
            pl.BlockSpec((hw, d), lambda b, i: (0, 0)),
            pl.BlockSpec((None, 6, d), lambda b, i: (b, 0, 0)),
            pl.BlockSpec((4, d), lambda b, i: (0, 0)),
        ],
        out_specs=pl.BlockSpec((None, tm, d), lambda b, i: (b, i, 0)),
        out_shape=jax.ShapeDtypeStruct(x.shape, F32),
        compiler_params=_cparams(("parallel", "parallel")),
        name="mla_out",
    )(a, x, wo_pad, mod_l, g_l)


def _route(logits):
    lane = lax.broadcasted_iota(jnp.int32, logits.shape, 1)
    neg = -jnp.inf
    lg = jnp.where(lane < N_EXPERTS, logits, neg)
    m1 = jnp.max(lg, axis=-1, keepdims=True)
    i1 = jnp.min(jnp.where(lg == m1, lane, LANES), axis=-1, keepdims=True)
    lg2 = jnp.where(lane == i1, neg, lg)
    m2 = jnp.max(lg2, axis=-1, keepdims=True)
    i2 = jnp.min(jnp.where(lg2 == m2, lane, LANES), axis=-1, keepdims=True)
    e = jnp.exp(m2 - m1)
    w1 = 1.0 / (1.0 + e)
    w2 = e / (1.0 + e)
    return jnp.where(lane == i1, w1, 0.0) + jnp.where(lane == i2, w2, 0.0)


def _moe_kernel(x_ref, wr_ref, br_ref, wg_ref, wu_ref, wo_ref, mod_ref, g_ref, o_ref, h_scr, comb_scr, acc):
    e = pl.program_id(2)

    @pl.when(e == 0)
    def _():
        h = _rms(x_ref[...], g_ref[2:3, :]) * (1.0 + mod_ref[4:5, :]) + mod_ref[3:4, :]
        logits = jnp.dot(h, wr_ref[...], preferred_element_type=F32, precision=HIGHEST) + br_ref[...]
        comb_scr[...] = _route(logits)
        h_scr[...] = h.astype(BF16)
        acc[...] = jnp.zeros_like(acc)

    h = h_scr[...]
    g = _dot(h, wg_ref[...])
    u = _dot(h, wu_ref[...])
    act = (g * _sigmoid(g) * u).astype(BF16)
    y = _dot(act, wo_ref[...])
    comb = comb_scr[...]
    lane = lax.broadcasted_iota(jnp.int32, comb.shape, 1)
    w_e = jnp.sum(jnp.where(lane == e, comb, 0.0), axis=-1, keepdims=True)
    acc[...] += w_e * y

    @pl.when(e == pl.num_programs(2) - 1)
    def _():
        o_ref[...] = x_ref[...] + mod_ref[5:6, :] * _rms(acc[...], g_ref[3:4, :])


def _moe(x, w_router, b_router, w_in, w_out, mod_l, g_l, tm):
    bsz, seq, d = x.shape
    n_e, f, _ = w_out.shape
    return pl.pallas_call(
        _moe_kernel,
        grid=(bsz, seq // tm, n_e),
        in_specs=[
            pl.BlockSpec((None, tm, d), lambda b, i, e: (b, i, 0)),
            pl.BlockSpec((d, LANES), lambda b, i, e: (0, 0)),
            pl.BlockSpec((1, LANES), lambda b, i, e: (0, 0)),
            pl.BlockSpec((None, d, f), lambda b, i, e: (e, 0, 0)),
            pl.BlockSpec((None, d, f), lambda b, i, e: (e, 0, 1)),
            pl.BlockSpec((None, f, d), lambda b, i, e: (e, 0, 0)),
            pl.BlockSpec((None, 6, d), lambda b, i, e: (b, 0, 0)),
            pl.BlockSpec((4, d), lambda b, i, e: (0, 0)),
        ],
        out_specs=pl.BlockSpec((None, tm, d), lambda b, i, e: (b, i, 0)),
        out_shape=jax.ShapeDtypeStruct(x.shape, F32),
        scratch_shapes=[pltpu.VMEM((tm, d), BF16), pltpu.VMEM((tm, LANES), F32), pltpu.VMEM((tm, d), F32)],
        compiler_params=_cparams(("parallel", "parallel", "arbitrary")),
        name="moe",
    )(x, w_router, b_router, w_in, w_in, w_out, mod_l, g_l)


def kernel(x, c, positions, norm_g, w_ada, b_ada, ssm_a_re, ssm_a_im, ssm_log_dt, ssm_b_re, ssm_b_im, ssm_c_re, ssm_c_im, ssm_d, ssm_w_glu, ffn_w_in, ffn_w_out, mla_w_in, mla_q_norm, mla_kv_norm, mla_w_uq, mla_w_ukv, mla_w_o, moe_w_router, moe_b_router, moe_w_in, moe_w_out):
    depth = norm_g.shape[0]
    seq = x.shape[1]
    tm = min(512, seq)
    mod = _ada(c, w_ada, b_ada)
    cos_t = sin_t = None
    for i in range(depth):
        j = i // 2
        mod_l, g_l = mod[i], norm_g[i]
        if i % 2 == 0:
            ssm_w = _ssm_weights(ssm_a_re[j], ssm_a_im[j], ssm_log_dt[j], ssm_b_re[j], ssm_b_im[j],
                                 ssm_c_re[j], ssm_c_im[j], ssm_d[j])
            ucat = _ssm_pre(x, mod_l, g_l, tm)
            zcat = _ssm(ucat, *ssm_w)
            x = _ssm_post(zcat, x, ssm_w_glu[j].astype(BF16), mod_l, g_l, tm)
            f = ffn_w_out.shape[1]
            x = _ffn(x, ffn_w_in[j].astype(BF16), ffn_w_out[j].astype(BF16), mod_l, g_l, tm, f // 2)
        else:
            if cos_t is None:
                cos_t, sin_t = _rope_tables(positions)
            proj_w, wo_pad = _mla_weights(mla_w_in[j], mla_q_norm[j], mla_kv_norm[j],
                                          mla_w_uq[j], mla_w_ukv[j], mla_w_o[j])
            q, k, v = _mla_proj(x, cos_t, sin_t, mod_l, g_l, proj_w, tm)
            a = _attention(q, k, v, tm)
            x = _attn_out(a, x, wo_pad, mod_l, g_l, tm)
            wr = jnp.pad(moe_w_router[j], ((0, 0), (0, LANES - N_EXPERTS)))
            br = jnp.pad(moe_b_router[j], (0, LANES - N_EXPERTS)).reshape(1, LANES)
            x = _moe(x, wr, br, moe_w_in[j].astype(BF16), moe_w_out[j].astype(BF16), mod_l, g_l, tm)
    return x
```

```python
import functools
import math

import jax
import jax.numpy as jnp
from jax import lax
from jax.experimental import pallas as pl
from jax.experimental.pallas import tpu as pltpu

F32 = jnp.float32
BF16 = jnp.bfloat16
HIGHEST = lax.Precision.HIGHEST

RMS_EPS = 1e-6
LANES = 128
SSM_GROUP = 16
SSM_STATE = 64
SSM_L = 16
SSM_LG = LANES // SSM_GROUP
MLA_HEADS = 16
MLA_NOPE = 64
MLA_ROPE = 32
MLA_V = 64
MLA_QK = MLA_NOPE + MLA_ROPE
V_ROWS = 80
MLA_Q_RANK = 384
MLA_KV_RANK = 256
ROPE_THETA = 10000.0
N_EXPERTS = 8
VMEM_LIMIT = 56 * 1024 * 1024


def _cparams(sem):
    return pltpu.CompilerParams(dimension_semantics=sem, vmem_limit_bytes=VMEM_LIMIT)


def _rms(x, g):
    return x * lax.rsqrt(jnp.mean(x * x, axis=-1, keepdims=True) + RMS_EPS) * g


def _sigmoid(x):
    return 1.0 / (1.0 + jnp.exp(-x))


def _dot(a, b):
    return jnp.dot(a, b, preferred_element_type=F32)


def _ada_kernel(c_ref, w_ref, b_ref, o_ref):
    c = c_ref[...]
    c_act = c * _sigmoid(c)
    o_ref[...] = jnp.dot(c_act, w_ref[...], preferred_element_type=F32, precision=HIGHEST) + b_ref[...]


def _ada(c, w_ada, b_ada):
    depth, d, _ = w_ada.shape
    bsz = c.shape[0]
    rows = 8
    c_pad = jnp.pad(c, ((0, rows - bsz), (0, 0)))
    out = pl.pallas_call(
        _ada_kernel,
        grid=(depth, 6),
        in_specs=[
            pl.BlockSpec((rows, d), lambda i, j: (0, 0)),
            pl.BlockSpec((None, d, d), lambda i, j: (i, 0, j)),
            pl.BlockSpec((None, None, 1, d), lambda i, j: (i, j, 0, 0)),
        ],
        out_specs=pl.BlockSpec((None, None, rows, d), lambda i, j: (i, j, 0, 0)),
        out_shape=jax.ShapeDtypeStruct((depth, 6, rows, d), F32),
        compiler_params=_cparams(("arbitrary", "arbitrary")),
        name="ada_mod",
    )(c_pad, w_ada, b_ada.reshape(depth, 6, 1, d))
    return out[:, :, :bsz, :].transpose(0, 2, 1, 3)


def _ssm_pre_kernel(x_ref, mod_ref, g_ref, o_ref, scr):
    tm = x_ref.shape[0]
    n = tm // SSM_L
    u = _rms(x_ref[...], g_ref[0:1, :]) * (1.0 + mod_ref[1:2, :]) + mod_ref[0:1, :]
    n_lg = o_ref.shape[0]
    for lg in range(n_lg):
        scr[lg] = u[:, lg * LANES:(lg + 1) * LANES]
    for s in range(SSM_L):
        for lg in range(n_lg):
            o_ref[lg, :, s * LANES:(s + 1) * LANES] = scr[lg, pl.ds(s, n, stride=SSM_L), :].astype(BF16)


def _ssm_pre(x, mod_l, g_l, tm):
    bsz, seq, d = x.shape
    n_lg = d // LANES
    return pl.pallas_call(
        _ssm_pre_kernel,
        grid=(bsz, seq // tm),
        in_specs=[
            pl.BlockSpec((None, tm, d), lambda b, i: (b, i, 0)),
            pl.BlockSpec((None, 6, d), lambda b, i: (b, 0, 0)),
            pl.BlockSpec((4, d), lambda b, i: (0, 0)),
        ],
        out_specs=pl.BlockSpec((None, n_lg, tm // SSM_L, SSM_L * LANES), lambda b, i: (b, 0, i, 0)),
        out_shape=jax.ShapeDtypeStruct((bsz, n_lg, seq // SSM_L, SSM_L * LANES), BF16),
        scratch_shapes=[pltpu.VMEM((n_lg, tm, LANES), F32)],
        compiler_params=_cparams(("parallel", "parallel")),
        name="ssm_pre",
    )(x, mod_l, g_l)


def _gelu_tanh(y):
    return 0.5 * y * (1.0 + jnp.tanh(math.sqrt(2.0 / math.pi) * (y + 0.044715 * (y * y * y))))


def _expand_block_diag(src_ref, tile_ref, dst, row_shift, col_shift, rows_per_step=256):
    n_rows, n_cols = dst.shape
    tile = tile_ref[...]
    col_g = (lax.broadcasted_iota(jnp.int32, (rows_per_step, n_cols), 1) >> col_shift) & (SSM_LG - 1)
    for r0 in range(0, n_rows, rows_per_step):
        val = _dot(src_ref[r0:r0 + rows_per_step, :], tile)
        row_g = ((lax.broadcasted_iota(jnp.int32, (rows_per_step, n_cols), 0) + r0) >> row_shift) & (SSM_LG - 1)
        dst[r0:r0 + rows_per_step, :] = jnp.where(row_g == col_g, val, 0.0).astype(BF16)


def _ssm_kernel(u_ref, xw_ref, yw_ref, z8_ref, zx_ref, t1_ref, t2_ref, a_ref, d_ref, o_ref,
                win_s, wout_s, m8_s, mx_s, v_scr, h_scr):
    n_chunks = u_ref.shape[0]
    half = (SSM_L // 2) * LANES
    ns = a_ref.shape[1]

    @pl.when(pl.program_id(1) == 0)
    def _():
        t3_ref = t2_ref.at[0:LANES, 0:half]
        _expand_block_diag(xw_ref, t1_ref, win_s, 4, 6)
        _expand_block_diag(yw_ref, t2_ref, wout_s, 6, 4)
        _expand_block_diag(z8_ref, t3_ref, m8_s, 4, 4)
        _expand_block_diag(zx_ref, t3_ref, mx_s, 4, 4)

    uc = u_ref[...]
    v_scr[...] = _dot(uc, win_s[...])
    ar = a_ref[0:1, :]
    ai = a_ref[1:2, :]

    def step(c, carry):
        hr, hi = carry
        h_scr[pl.ds(c, 1), pl.ds(0, ns)] = hr
        h_scr[pl.ds(c, 1), pl.ds(ns, ns)] = hi
        vr = v_scr[pl.ds(c, 1), pl.ds(0, ns)]
        vi = v_scr[pl.ds(c, 1), pl.ds(ns, ns)]
        return ar * hr - ai * hi + vr, ar * hi + ai * hr + vi

    zero = jnp.zeros((1, ns), F32)
    lax.fori_loop(0, n_chunks, step, (zero, zero))
    y = _dot(h_scr[...].astype(BF16), wout_s[...])
    u_lo = uc[:, :half]
    u_hi = uc[:, half:]
    m8 = m8_s[...]
    y_lo = y[:, :half] + _dot(u_lo, m8)
    y_hi = y[:, half:] + _dot(u_hi, m8) + _dot(u_lo, mx_s[...])
    dd = d_ref[...]
    o_ref[:, :half] = _gelu_tanh(y_lo + dd[:, :half] * u_lo.astype(F32)).astype(BF16)
    o_ref[:, half:] = _gelu_tanh(y_hi + dd[:, half:] * u_hi.astype(F32)).astype(BF16)


def _ssm(ucat, xw, yw, z8, zx, t1, t2, a16, dcat):
    bsz, n_lg, n_chunks, width = ucat.shape
    ns2 = t1.shape[-1]
    half = width // 2

    def per_lg(a):
        return pl.BlockSpec((None,) + a.shape[1:], lambda g, b: (g, 0, 0))

    def const(a):
        return pl.BlockSpec(a.shape, lambda g, b: (0, 0))

    return pl.pallas_call(
        _ssm_kernel,
        grid=(n_lg, bsz),
        in_specs=[
            pl.BlockSpec((None, None, n_chunks, width), lambda g, b: (b, g, 0, 0)),
            per_lg(xw), per_lg(yw), per_lg(z8), per_lg(zx), const(t1), const(t2), per_lg(a16), per_lg(dcat),
        ],
        out_specs=pl.BlockSpec((None, None, n_chunks, width), lambda g, b: (b, g, 0, 0)),
        out_shape=jax.ShapeDtypeStruct(ucat.shape, BF16),
        scratch_shapes=[
            pltpu.VMEM((width, ns2), BF16), pltpu.VMEM((ns2, width), BF16),
            pltpu.VMEM((half, half), BF16), pltpu.VMEM((half, half), BF16),
            pltpu.VMEM((n_chunks, ns2), F32), pltpu.VMEM((n_chunks, ns2), F32),
        ],
        compiler_params=_cparams(("arbitrary", "arbitrary")),
        name="ssm_scan",
    )(ucat, xw, yw, z8, zx, t1, t2, a16, dcat)


def _ssm_post_kernel(z_ref, x_ref, w_ref, mod_ref, g_ref, o_ref, zp, rp):
    n_lg, n, _ = z_ref.shape
    d = x_ref.shape[1]
    for s in range(SSM_L):
        for lg in range(n_lg):
            zp[s * n:(s + 1) * n, lg * LANES:(lg + 1) * LANES] = z_ref[lg, :, s * LANES:(s + 1) * LANES]
    gl = _dot(zp[...], w_ref[...])
    h = gl[:, :d] * _sigmoid(gl[:, d:])
    r = mod_ref[2:3, :] * _rms(h, g_ref[1:2, :])
    for lg in range(n_lg):
        rp[lg] = r[:, lg * LANES:(lg + 1) * LANES]
    sub = 8
    for c in range(n):
        for k in range(SSM_L // sub):
            rows = pl.ds(c * SSM_L + k * sub, sub)
            for lg in range(n_lg):
                cols = slice(lg * LANES, (lg + 1) * LANES)
                o_ref[rows, cols] = x_ref[rows, cols] + rp[lg, pl.ds(k * sub * n + c, sub, stride=n), :]


def _ssm_post(zcat, x, w_glu, mod_l, g_l, tm):
    bsz, seq, d = x.shape
    n_lg = d // LANES
    return pl.pallas_call(
        _ssm_post_kernel,
        grid=(bsz, seq // tm),
        in_specs=[
            pl.BlockSpec((None, n_lg, tm // SSM_L, SSM_L * LANES), lambda b, i: (b, 0, i, 0)),
            pl.BlockSpec((None, tm, d), lambda b, i: (b, i, 0)),
            pl.BlockSpec((d, 2 * d), lambda b, i: (0, 0)),
            pl.BlockSpec((None, 6, d), lambda b, i: (b, 0, 0)),
            pl.BlockSpec((4, d), lambda b, i: (0, 0)),
        ],
        out_specs=pl.BlockSpec((None, tm, d), lambda b, i: (b, i, 0)),
        out_shape=jax.ShapeDtypeStruct(x.shape, F32),
        scratch_shapes=[pltpu.VMEM((tm, d), BF16), pltpu.VMEM((n_lg, tm, LANES), F32)],
        compiler_params=_cparams(("parallel", "parallel")),
        name="ssm_post",
    )(zcat, x, w_glu, mod_l, g_l)


def _ssm_weights(a_re, a_im, log_dt, b_re, b_im, c_re, c_im, d_skip):
    n_groups = a_re.shape[0]
    n_lg = n_groups // SSM_LG
    p, hh, ll = SSM_STATE, SSM_GROUP, SSM_L
    dt = jnp.exp(log_dt)[:, None]
    k = jnp.arange(ll + 1, dtype=F32)[:, None, None]
    mag = jnp.exp(k * (a_re * dt))
    pw_re = mag * jnp.cos(k * (a_im * dt))
    pw_im = mag * jnp.sin(k * (a_im * dt))
    den = a_re * a_re + a_im * a_im
    num_re, num_im = pw_re[1] - 1.0, pw_im[1]
    f_re = (num_re * a_re + num_im * a_im) / den
    f_im = (num_im * a_re - num_re * a_im) / den
    bb_re = f_re[..., None] * b_re - f_im[..., None] * b_im
    bb_im = f_re[..., None] * b_im + f_im[..., None] * b_re
    pb_re = pw_re[:ll, :, :, None] * bb_re[None] - pw_im[:ll, :, :, None] * bb_im[None]
    pb_im = pw_re[:ll, :, :, None] * bb_im[None] + pw_im[:ll, :, :, None] * bb_re[None]
    pb = jnp.stack([pb_re[::-1], pb_im[::-1]], axis=1)
    pb = pb.reshape(ll, 2, n_lg, SSM_LG, p, hh)
    xw = pb.transpose(2, 0, 3, 5, 1, 4).reshape(n_lg, ll * LANES, 2 * p)
    cp_re = c_re[None] * pw_re[1:, :, None, :] - c_im[None] * pw_im[1:, :, None, :]
    cp_im = c_re[None] * pw_im[1:, :, None, :] + c_im[None] * pw_re[1:, :, None, :]
    cp = jnp.stack([cp_re, -cp_im], axis=1).reshape(ll, 2, n_lg, SSM_LG, hh, p)
    yw = cp.transpose(2, 1, 3, 5, 0, 4).reshape(n_lg, 2 * SSM_LG * p, ll * hh)
    kt = (jnp.einsum('ghp,tgp,gpi->tghi', c_re, pw_re[:ll], bb_re, precision=HIGHEST)
          - jnp.einsum('ghp,tgp,gpi->tghi', c_re, pw_im[:ll], bb_im, precision=HIGHEST)
          - jnp.einsum('ghp,tgp,gpi->tghi', c_im, pw_re[:ll], bb_im, precision=HIGHEST)
          - jnp.einsum('ghp,tgp,gpi->tghi', c_im, pw_im[:ll], bb_re, precision=HIGHEST))
    kt = jnp.concatenate([kt, jnp.zeros_like(kt)], axis=0)
    hl = ll // 2
    s_in = jnp.arange(hl)[:, None]
    s_out = jnp.arange(hl)[None, :]
    lag8 = (s_out - s_in) % (2 * ll)
    lagx = s_out - s_in + hl

    def toeplitz(lag):
        t = kt[lag].reshape(hl, hl, n_lg, SSM_LG, hh, hh)
        return t.transpose(2, 0, 3, 5, 1, 4).reshape(n_lg, hl * LANES, hl * hh)

    z8 = toeplitz(lag8)
    zx = toeplitz(lagx)
    j1 = jnp.arange(2 * SSM_LG * p)
    t1 = (jnp.arange(2 * p)[:, None] == ((j1 // (SSM_LG * p)) * p + j1 % p)[None, :])
    j2 = jnp.arange(ll * LANES)
    t2 = (jnp.arange(ll * hh)[:, None] == ((j2 // LANES) * hh + j2 % hh)[None, :])
    a16 = jnp.stack([pw_re[ll], pw_im[ll]], axis=0).reshape(2, n_lg, SSM_LG * p).transpose(1, 0, 2)
    dcat = jnp.tile(d_skip.reshape(n_lg, 1, LANES), (1, 1, ll))
    return (xw.astype(BF16), yw.astype(BF16), z8.astype(BF16), zx.astype(BF16),
            t1.astype(BF16), t2.astype(BF16), a16, dcat)


def _ffn_kernel(x_ref, wg_ref, wu_ref, wo_ref, mod_ref, g_ref, o_ref, h_scr, acc):
    k = pl.program_id(2)

    @pl.when(k == 0)
    def _():
        h = _rms(x_ref[...], g_ref[2:3, :]) * (1.0 + mod_ref[4:5, :]) + mod_ref[3:4, :]
        h_scr[...] = h.astype(BF16)
        acc[...] = jnp.zeros_like(acc)

    h = h_scr[...]
    g = _dot(h, wg_ref[...])
    u = _dot(h, wu_ref[...])
    act = (g * _sigmoid(g) * u).astype(BF16)
    acc[...] += _dot(act, wo_ref[...])

    @pl.when(k == pl.num_programs(2) - 1)
    def _():
        o_ref[...] = x_ref[...] + mod_ref[5:6, :] * _rms(acc[...], g_ref[3:4, :])


def _ffn(x, w_in, w_out, mod_l, g_l, tm, tf):
    bsz, seq, d = x.shape
    f = w_out.shape[0]
    nk = f // tf
    return pl.pallas_call(
        _ffn_kernel,
        grid=(bsz, seq // tm, nk),
        in_specs=[
            pl.BlockSpec((None, tm, d), lambda b, i, k: (b, i, 0)),
            pl.BlockSpec((d, tf), lambda b, i, k: (0, k)),
            pl.BlockSpec((d, tf), lambda b, i, k: (0, k + nk)),
            pl.BlockSpec((tf, d), lambda b, i, k: (k, 0)),
            pl.BlockSpec((None, 6, d), lambda b, i, k: (b, 0, 0)),
            pl.BlockSpec((4, d), lambda b, i, k: (0, 0)),
        ],
        out_specs=pl.BlockSpec((None, tm, d), lambda b, i, k: (b, i, 0)),
        out_shape=jax.ShapeDtypeStruct(x.shape, F32),
        scratch_shapes=[pltpu.VMEM((tm, d), BF16), pltpu.VMEM((tm, d), F32)],
        compiler_params=_cparams(("parallel", "parallel", "arbitrary")),
        name="ffn",
    )(x, w_in, w_in, w_out, mod_l, g_l)


def _rope_kernel(pos_ref, f_ref, cos_ref, sin_ref):
    ang = pos_ref[...].astype(F32) * f_ref[...]
    cos_ref[...] = jnp.cos(ang)
    sin_ref[...] = jnp.sin(ang)


def _rope_tables(positions):
    bsz, seq = positions.shape
    half = MLA_ROPE // 2
    inv_freq = ROPE_THETA ** (-jnp.arange(half, dtype=F32) / half)
    per_row = LANES // half
    rows = bsz * seq // per_row
    pos_rep = jnp.broadcast_to(positions[..., None], (bsz, seq, half)).reshape(rows, LANES)
    f_rep = jnp.tile(inv_freq, per_row).reshape(1, LANES)
    tr = min(rows, 512)
    cos, sin = pl.pallas_call(
        _rope_kernel,
        grid=(rows // tr,),
        in_specs=[pl.BlockSpec((tr, LANES), lambda i: (i, 0)), pl.BlockSpec((1, LANES), lambda i: (0, 0))],
        out_specs=[pl.BlockSpec((tr, LANES), lambda i: (i, 0))] * 2,
        out_shape=[jax.ShapeDtypeStruct((rows, LANES), F32)] * 2,
        compiler_params=_cparams(("parallel",)),
        name="rope_tables",
    )(pos_rep, f_rep)
    cos = cos.reshape(bsz, seq, half)
    sin = sin.reshape(bsz, seq, half)
    ones = jnp.ones((bsz, seq, MLA_NOPE), F32)
    zpad = jnp.zeros((bsz, seq, LANES - MLA_QK), F32)
    cos_t = jnp.concatenate([ones, cos, cos, zpad], axis=-1)
    sin_t = jnp.concatenate([0.0 * ones, sin, sin, zpad], axis=-1)
    return cos_t, sin_t


def _mla_proj_kernel(x_ref, cos_ref, sin_ref, mod_ref, g_ref, wcq_ref, wckv_ref, wkr_ref, qn_ref, kvn_ref,
                     wq_ref, wqs_ref, wk_ref, wv_ref, perm_ref, exp_ref, q_ref, k_ref, v_ref):
    h = (_rms(x_ref[...], g_ref[0:1, :]) * (1.0 + mod_ref[1:2, :]) + mod_ref[0:1, :]).astype(BF16)
    cq = _rms(_dot(h, wcq_ref[...]), qn_ref[...]).astype(BF16)
    ckv = _rms(_dot(h, wckv_ref[...]), kvn_ref[...]).astype(BF16)
    kr = _dot(h, wkr_ref[...])
    cos = cos_ref[...]
    sin = sin_ref[...]
    q = _dot(cq, wq_ref[...])
    qs = _dot(cq, wqs_ref[...])
    for hd in range(MLA_HEADS):
        sl = slice(hd * LANES, (hd + 1) * LANES)
        q_ref[:, sl] = (q[:, sl] * cos + qs[:, sl] * sin).astype(BF16)
    kr_rot = (kr * cos + _dot(kr.astype(BF16), perm_ref[...]) * sin).astype(BF16)
    k_ref[...] = (_dot(ckv, wk_ref[...]) + _dot(kr_rot, exp_ref[...])).astype(BF16)
    vt = lax.dot_general(wv_ref[...], ckv, (((1,), (1,)), ((), ())), preferred_element_type=F32)
    row = lax.broadcasted_iota(jnp.int32, vt.shape, 0)
    v_ref[...] = jnp.where(row % V_ROWS == MLA_V, 1.0, vt).astype(BF16)


def _mla_proj(x, cos_t, sin_t, mod_l, g_l, wts, tm):
    bsz, seq, d = x.shape
    hw = MLA_HEADS * LANES

    def full(a):
        return pl.BlockSpec(a.shape, lambda b, i: (0,) * a.ndim)

    out_sds = jax.ShapeDtypeStruct((bsz, seq, hw), BF16)
    vt_rows = MLA_HEADS * V_ROWS
    vt_sds = jax.ShapeDtypeStruct((bsz, seq // tm, vt_rows, tm), BF16)
    return pl.pallas_call(
        _mla_proj_kernel,
        grid=(bsz, seq // tm),
        in_specs=[
            pl.BlockSpec((None, tm, d), lambda b, i: (b, i, 0)),
            pl.BlockSpec((None, tm, LANES), lambda b, i: (b, i, 0)),
            pl.BlockSpec((None, tm, LANES), lambda b, i: (b, i, 0)),
            pl.BlockSpec((None, 6, d), lambda b, i: (b, 0, 0)),
            pl.BlockSpec((4, d), lambda b, i: (0, 0)),
        ] + [full(w) for w in wts],
        out_specs=[pl.BlockSpec((None, tm, hw), lambda b, i: (b, i, 0))] * 2
        + [pl.BlockSpec((None, None, vt_rows, tm), lambda b, i: (b, i, 0, 0))],
        out_shape=[out_sds, out_sds, vt_sds],
        compiler_params=_cparams(("parallel", "parallel")),
        name="mla_proj",
    )(x, cos_t, sin_t, mod_l, g_l, *wts)


def _mla_weights(w_in, q_norm, kv_norm, w_uq, w_ukv, w_o):
    d = w_in.shape[0]
    half = MLA_ROPE // 2
    pad = LANES - MLA_QK
    w_cq = w_in[:, :MLA_Q_RANK]
    w_ckv = w_in[:, MLA_Q_RANK:MLA_Q_RANK + MLA_KV_RANK]
    w_kr = jnp.pad(w_in[:, MLA_Q_RANK + MLA_KV_RANK:], ((0, 0), (MLA_NOPE, pad)))
    scale = MLA_QK ** -0.5 * math.log2(math.e)
    wq =w_uq.reshape(MLA_Q_RANK, MLA_HEADS, MLA_QK) * scale
    nope, x1, x2 = wq[..., :MLA_NOPE], wq[..., MLA_NOPE:MLA_NOPE + half], wq[..., MLA_NOPE + half:]
    zp = jnp.zeros((MLA_Q_RANK, MLA_HEADS, pad), F32)
    wq_pad = jnp.concatenate([nope, x1, x2, zp], axis=-1).reshape(MLA_Q_RANK, MLA_HEADS * LANES)
    wq_sw = jnp.concatenate([0.0 * nope, -x2, x1, zp], axis=-1).reshape(MLA_Q_RANK, MLA_HEADS * LANES)
    wkv = w_ukv.reshape(MLA_KV_RANK, MLA_HEADS, MLA_NOPE + MLA_V)
    zk = jnp.zeros((MLA_KV_RANK, MLA_HEADS, LANES - MLA_NOPE), F32)
    wk_pad = jnp.concatenate([wkv[..., :MLA_NOPE], zk], axis=-1).reshape(MLA_KV_RANK, MLA_HEADS * LANES)
    wv_t = jnp.pad(wkv[..., MLA_NOPE:].transpose(1, 2, 0), ((0, 0), (0, V_ROWS - MLA_V), (0, 0)))
    wv_t = wv_t.reshape(MLA_HEADS * V_ROWS, MLA_KV_RANK)
    lane = jnp.arange(LANES)
    src = jnp.where((lane >= MLA_NOPE) & (lane < MLA_NOPE + half), lane + half,
                    jnp.where((lane >= MLA_NOPE + half) & (lane < MLA_QK), lane - half, -1))
    sign = jnp.where(lane < MLA_NOPE + half, -1.0, 1.0)
    perm = (lane[:, None] == src[None, :]).astype(F32) * sign[None, :]
    is_rope = (lane >= MLA_NOPE) & (lane < MLA_QK)
    expand = jnp.tile((jnp.eye(LANES, dtype=F32) * is_rope[None, :].astype(F32)), (1, MLA_HEADS))
    proj_w = (w_cq.astype(BF16), w_ckv.astype(BF16), w_kr.astype(BF16),
              q_norm.reshape(1, -1), kv_norm.reshape(1, -1),
              wq_pad.astype(BF16), wq_sw.astype(BF16), wk_pad.astype(BF16), wv_t.astype(BF16),
              perm.astype(BF16), expand.astype(BF16))
    return proj_w, w_o.astype(BF16)


def _attn_kernel(q_ref, k_ref, vt_ref, o_ref, m_scr, acc):
    tq = q_ref.shape[0]
    tk = vt_ref.shape[2]
    qi = pl.program_id(2)
    m_scr[...] = jnp.full_like(m_scr, -jnp.inf)
    acc[...] = jnp.zeros_like(acc)

    def tile(j, masked):
        start = pl.multiple_of(j * tk, tk)
        for hd in range(2):
            lanes = slice(hd * LANES, (hd + 1) * LANES)
            k = k_ref[pl.ds(start, tk), lanes]
            st = lax.dot_general(k, q_ref[:, lanes], (((1,), (1,)), ((), ())), preferred_element_type=F32)
            if masked:
                key = lax.broadcasted_iota(jnp.int32, st.shape, 0)
                qry = lax.broadcasted_iota(jnp.int32, st.shape, 1)
                st = jnp.where(key <= qry, st, -jnp.inf)
            m_prev = m_scr[hd]
            m_new = jnp.maximum(m_prev, jnp.max(st, axis=0, keepdims=True))
            alpha = jnp.exp2(m_prev - m_new)
            pt = jnp.exp2(st - m_new).astype(BF16)
            vt = vt_ref[j, hd * V_ROWS:(hd + 1) * V_ROWS, :]
            acc[hd] = alpha * acc[hd] + _dot(vt, pt)
            m_scr[hd] = m_new

    def body(j, carry):
        tile(j, False)
        return carry

    lax.fori_loop(0, qi, body, 0)
    tile(qi, True)
    outs = []
    for hd in range(2):
        a = acc[hd]
        outs.append(a[:MLA_V, :] / a[MLA_V:MLA_V + 1, :])
    o_ref[...] = jnp.concatenate(outs, axis=0).T.astype(BF16)


def _attention(q, k, vt, tq):
    bsz, seq, hw = q.shape
    n_pairs = hw // (2 * LANES)
    n_kt, _, tk = vt.shape[1:]
    assert tq == tk, "the causal diagonal tile assumes square tiles"
    return pl.pallas_call(
        _attn_kernel,
        grid=(bsz, n_pairs, seq // tq),
        in_specs=[
            pl.BlockSpec((None, tq, 2 * LANES), lambda b, h, i: (b, i, h)),
            pl.BlockSpec((None, seq, 2 * LANES), lambda b, h, i: (b, 0, h)),
            pl.BlockSpec((None, n_kt, 2 * V_ROWS, tk), lambda b, h, i: (b, 0, h, 0)),
        ],
        out_specs=pl.BlockSpec((None, tq, 2 * MLA_V), lambda b, h, i: (b, i, h)),
        out_shape=jax.ShapeDtypeStruct((bsz, seq, n_pairs * 2 * MLA_V), BF16),
        scratch_shapes=[pltpu.VMEM((2, 1, tq), F32), pltpu.VMEM((2, V_ROWS, tq), F32)],
        compiler_params=_cparams(("parallel", "parallel", "arbitrary")),
        name="mla_attention",
    )(q, k, vt)


def _attn_out_kernel(a_ref, x_ref, w_ref, mod_ref, g_ref, o_ref):
    h = _dot(a_ref[...], w_ref[...])
    o_ref[...] = x_ref[...] + mod_ref[2:3, :] * _rms(h, g_ref[1:2, :])


def _attn_out(a, x, wo_pad, mod_l, g_l, tm):
    bsz, seq, d = x.shape
    hw = a.shape[-1]
    return pl.pallas_call(
        _attn_out_kernel,
        grid=(bsz, seq // tm),
        in_specs=[
            pl.BlockSpec((None, tm, hw), lambda b, i: (b, i, 0)),
            pl.BlockSpec((None, tm, d), lambda b, i: (b, i, 0)),
            pl.BlockSpec((hw, d), lambda b, i: (0, 0)),
            pl.BlockSpec((None, 6, d), lambda b, i: (b, 0, 0)),
            pl.BlockSpec((4, d), lambda b, i: (0, 0)),
        ],
        out_specs=pl.BlockSpec((None, tm, d), lambda b, i: (b, i, 0)),
        out_shape=jax.ShapeDtypeStruct(x.shape, F32),
        compiler_params=_cparams(("parallel", "parallel")),
        name="mla_out",
    )(a, x, wo_pad, mod_l, g_l)


def _route(logits):
    lane = lax.broadcasted_iota(jnp.int32, logits.shape, 1)
    neg = -jnp.inf
    lg = jnp.where(lane < N_EXPERTS, logits, neg)
    m1 = jnp.max(lg, axis=-1, keepdims=True)
    i1 = jnp.min(jnp.where(lg == m1, lane, LANES), axis=-1, keepdims=True)
    lg2 = jnp.where(lane == i1, neg, lg)
    m2 = jnp.max(lg2, axis=-1, keepdims=True)
    i2 = jnp.min(jnp.where(lg2 == m2, lane, LANES), axis=-1, keepdims=True)
    e = jnp.exp(m2 - m1)
    w1 = 1.0 / (1.0 + e)
    w2 = e / (1.0 + e)
    return jnp.where(lane == i1, w1, 0.0) + jnp.where(lane == i2, w2, 0.0)


def _moe_kernel(x_ref, wr_ref, br_ref, wg_ref, wu_ref, wo_ref, mod_ref, g_ref, o_ref, h_scr, comb_scr, acc):
    e = pl.program_id(2)

    @pl.when(e == 0)
    def _():
        h = _rms(x_ref[...], g_ref[2:3, :]) * (1.0 + mod_ref[4:5, :]) + mod_ref[3:4, :]
        logits = jnp.dot(h, wr_ref[...], preferred_element_type=F32, precision=HIGHEST) + br_ref[...]
        comb_scr[...] = _route(logits)
        h_scr[...] = h.astype(BF16)
        acc[...] = jnp.zeros_like(acc)

    h = h_scr[...]
    g = _dot(h, wg_ref[...])
    u = _dot(h, wu_ref[...])
    act = (g * _sigmoid(g) * u).astype(BF16)
    y = _dot(act, wo_ref[...])
    comb = comb_scr[...]
    lane = lax.broadcasted_iota(jnp.int32, comb.shape, 1)
    w_e = jnp.sum(jnp.where(lane == e, comb, 0.0), axis=-1, keepdims=True)
    acc[...] += w_e * y

    @pl.when(e == pl.num_programs(2) - 1)
    def _():
        o_ref[...] = x_ref[...] + mod_ref[5:6, :] * _rms(acc[...], g_ref[3:4, :])


def _moe(x, w_router, b_router, w_in, w_out, mod_l, g_l, tm):
    bsz, seq, d = x.shape
    n_e, f, _ = w_out.shape
    return pl.pallas_call(
        _moe_kernel,
        grid=(bsz, seq // tm, n_e),
        in_specs=[
            pl.BlockSpec((None, tm, d), lambda b, i, e: (b, i, 0)),
            pl.BlockSpec((d, LANES), lambda b, i, e: (0, 0)),
            pl.BlockSpec((1, LANES), lambda b, i, e: (0, 0)),
            pl.BlockSpec((None, d, f), lambda b, i, e: (e, 0, 0)),
            pl.BlockSpec((None, d, f), lambda b, i, e: (e, 0, 1)),
            pl.BlockSpec((None, f, d), lambda b, i, e: (e, 0, 0)),
            pl.BlockSpec((None, 6, d), lambda b, i, e: (b, 0, 0)),
            pl.BlockSpec((4, d), lambda b, i, e: (0, 0)),
        ],
        out_specs=pl.BlockSpec((None, tm, d), lambda b, i, e: (b, i, 0)),
        out_shape=jax.ShapeDtypeStruct(x.shape, F32),
        scratch_shapes=[pltpu.VMEM((tm, d), BF16), pltpu.VMEM((tm, LANES), F32), pltpu.VMEM((tm, d), F32)],
        compiler_params=_cparams(("parallel", "parallel", "arbitrary")),
        name="moe",
    )(x, w_router, b_router, w_in, w_in, w_out, mod_l, g_l)


def kernel(x, c, positions, norm_g, w_ada, b_ada, ssm_a_re, ssm_a_im, ssm_log_dt, ssm_b_re, ssm_b_im, ssm_c_re, ssm_c_im, ssm_d, ssm_w_glu, ffn_w_in, ffn_w_out, mla_w_in, mla_q_norm, mla_kv_norm, mla_w_uq, mla_w_ukv, mla_w_o, moe_w_router, moe_b_router, moe_w_in, moe_w_out):
    depth = norm_g.shape[0]
    seq = x.shape[1]
    tm = min(512, seq)
    mod = _ada(c, w_ada, b_ada)
    cos_t = sin_t = None
    for i in range(depth):
        j = i // 2
        mod_l, g_l = mod[i], norm_g[i]
        if i % 2 == 0:
            ssm_w = _ssm_weights(ssm_a_re[j], ssm_a_im[j], ssm_log_dt[j], ssm_b_re[j], ssm_b_im[j],
                                 ssm_c_re[j], ssm_c_im[j], ssm_d[j])
            ucat = _ssm_pre(x, mod_l, g_l, tm)
            zcat = _ssm(ucat, *ssm_w)
            x = _ssm_post(zcat, x, ssm_w_glu[j].astype(BF16), mod_l, g_l, tm)
            f = ffn_w_out.shape[1]
            x = _ffn(x, ffn_w_in[j].astype(BF16), ffn_w_out[j].astype(BF16), mod_l, g_l, tm, f // 2)
        else:
            if cos_t is None:
                cos_t, sin_t = _rope_tables(positions)
            proj_w, wo_pad = _mla_weights(mla_w_in[j], mla_q_norm[j], mla_kv_norm[j],
                                          mla_w_uq[j], mla_w_ukv[j], mla_w_o[j])
            q, k, v = _mla_proj(x, cos_t, sin_t, mod_l, g_l, proj_w, tm)
            a = _attention(q, k, v, tm)
            x = _attn_out(a, x, wo_pad, mod_l, g_l, tm)
            wr = jnp.pad(moe_w_router[j], ((0, 0), (0, LANES - N_EXPERTS)))
            br = jnp.pad(moe_b_router[j], (0, LANES - N_EXPERTS)).reshape(1, LANES)
            x = _moe(x, wr, br, moe_w_in[j].astype(BF16), moe_w_out[j].astype(BF16), mod_l, g_l, tm)
    return x
```

```python
import functools
import math

import jax
import jax.numpy as jnp
from jax import lax
from jax.experimental import pallas as pl
from jax.experimental.pallas import tpu as pltpu

F32 = jnp.float32
BF16 = jnp.bfloat16
HIGHEST = lax.Precision.HIGHEST

RMS_EPS = 1e-6
LANES = 128
SSM_GROUP = 16
SSM_STATE = 64
SSM_L = 16
SSM_LG = LANES // SSM_GROUP
MLA_HEADS = 16
MLA_NOPE = 64
MLA_ROPE = 32
MLA_V = 64
MLA_QK = MLA_NOPE + MLA_ROPE
V_ROWS = 80
MLA_Q_RANK = 384
MLA_KV_RANK = 256
ROPE_THETA = 10000.0
N_EXPERTS = 8
MOE_TB = 1024
MOE_SUB = 128
VMEM_LIMIT = 56 * 1024 * 1024


def _cparams(sem):
    return pltpu.CompilerParams(dimension_semantics=sem, vmem_limit_bytes=VMEM_LIMIT)


def _rms(x, g):
    return x * lax.rsqrt(jnp.mean(x * x, axis=-1, keepdims=True) + RMS_EPS) * g


def _sigmoid(x):
    return 1.0 / (1.0 + jnp.exp(-x))


def _dot(a, b):
    return jnp.dot(a, b, preferred_element_type=F32)


def _ada_kernel(c_ref, w_ref, b_ref, o_ref):
    c = c_ref[...]
    c_act = c * _sigmoid(c)
    o_ref[...] = jnp.dot(c_act, w_ref[...], preferred_element_type=F32, precision=HIGHEST) + b_ref[...]


def _ada(c, w_ada, b_ada):
    depth, d, _ = w_ada.shape
    bsz = c.shape[0]
    rows = 8
    c_pad = jnp.pad(c, ((0, rows - bsz), (0, 0)))
    out = pl.pallas_call(
        _ada_kernel,
        grid=(depth, 6),
        in_specs=[
            pl.BlockSpec((rows, d), lambda i, j: (0, 0)),
            pl.BlockSpec((None, d, d), lambda i, j: (i, 0, j)),
            pl.BlockSpec((None, None, 1, d), lambda i, j: (i, j, 0, 0)),
        ],
        out_specs=pl.BlockSpec((None, None, rows, d), lambda i, j: (i, j, 0, 0)),
        out_shape=jax.ShapeDtypeStruct((depth, 6, rows, d), F32),
        compiler_params=_cparams(("arbitrary", "arbitrary")),
        name="ada_mod",
    )(c_pad, w_ada, b_ada.reshape(depth, 6, 1, d))
    return out[:, :, :bsz, :].transpose(0, 2, 1, 3)


def _ssm_pre_kernel(x_ref, mod_ref, g_ref, o_ref, scr):
    tm = x_ref.shape[0]
    n = tm // SSM_L
    u = _rms(x_ref[...], g_ref[0:1, :]) * (1.0 + mod_ref[1:2, :]) + mod_ref[0:1, :]
    n_lg = o_ref.shape[0]
    for lg in range(n_lg):
        scr[lg] = u[:, lg * LANES:(lg + 1) * LANES]
    for s in range(SSM_L):
        for lg in range(n_lg):
            o_ref[lg, :, s * LANES:(s + 1) * LANES] = scr[lg, pl.ds(s, n, stride=SSM_L), :].astype(BF16)


def _ssm_pre(x, mod_l, g_l, tm):
    bsz, seq, d = x.shape
    n_lg = d // LANES
    return pl.pallas_call(
        _ssm_pre_kernel,
        grid=(bsz, seq // tm),
        in_specs=[
            pl.BlockSpec((None, tm, d), lambda b, i: (b, i, 0)),
            pl.BlockSpec((None, 6, d), lambda b, i: (b, 0, 0)),
            pl.BlockSpec((4, d), lambda b, i: (0, 0)),
        ],
        out_specs=pl.BlockSpec((None, n_lg, tm // SSM_L, SSM_L * LANES), lambda b, i: (b, 0, i, 0)),
        out_shape=jax.ShapeDtypeStruct((bsz, n_lg, seq // SSM_L, SSM_L * LANES), BF16),
        scratch_shapes=[pltpu.VMEM((n_lg, tm, LANES), F32)],
        compiler_params=_cparams(("parallel", "parallel")),
        name="ssm_pre",
    )(x, mod_l, g_l)


def _gelu_tanh(y):
    return 0.5 * y * (1.0 + jnp.tanh(math.sqrt(2.0 / math.pi) * (y + 0.044715 * (y * y * y))))


def _expand_block_diag(src_ref, tile_ref, dst, row_shift, col_shift, rows_per_step=256):
    n_rows, n_cols = dst.shape
    tile = tile_ref[...]
    col_g = (lax.broadcasted_iota(jnp.int32, (rows_per_step, n_cols), 1) >> col_shift) & (SSM_LG - 1)
    for r0 in range(0, n_rows, rows_per_step):
        val = _dot(src_ref[r0:r0 + rows_per_step, :], tile)
        row_g = ((lax.broadcasted_iota(jnp.int32, (rows_per_step, n_cols), 0) + r0) >> row_shift) & (SSM_LG - 1)
        dst[r0:r0 + rows_per_step, :] = jnp.where(row_g == col_g, val, 0.0).astype(BF16)


def _ssm_kernel(u_ref, xw_ref, yw_ref, z8_ref, zx_ref, t1_ref, t2_ref, a_ref, d_ref, o_ref,
                win_s, wout_s, m8_s, mx_s, v_scr, h_scr):
    n_chunks = u_ref.shape[0]
    half = (SSM_L // 2) * LANES
    ns = a_ref.shape[1]

    @pl.when(pl.program_id(1) == 0)
    def _():
        t3_ref = t2_ref.at[0:LANES, 0:half]
        _expand_block_diag(xw_ref, t1_ref, win_s, 4, 6)
        _expand_block_diag(yw_ref, t2_ref, wout_s, 6, 4)
        _expand_block_diag(z8_ref, t3_ref, m8_s, 4, 4)
        _expand_block_diag(zx_ref, t3_ref, mx_s, 4, 4)

    uc = u_ref[...]
    v_scr[...] = _dot(uc, win_s[...])
    ar = a_ref[0:1, :]
    ai = a_ref[1:2, :]

    def step(c, carry):
        hr, hi = carry
        h_scr[pl.ds(c, 1), pl.ds(0, ns)] = hr
        h_scr[pl.ds(c, 1), pl.ds(ns, ns)] = hi
        vr = v_scr[pl.ds(c, 1), pl.ds(0, ns)]
        vi = v_scr[pl.ds(c, 1), pl.ds(ns, ns)]
        return ar * hr - ai * hi + vr, ar * hi + ai * hr + vi

    zero = jnp.zeros((1, ns), F32)
    lax.fori_loop(0, n_chunks, step, (zero, zero))
    y = _dot(h_scr[...].astype(BF16), wout_s[...])
    u_lo = uc[:, :half]
    u_hi = uc[:, half:]
    m8 = m8_s[...]
    y_lo = y[:, :half] + _dot(u_lo, m8)
    y_hi = y[:, half:] + _dot(u_hi, m8) + _dot(u_lo, mx_s[...])
    dd = d_ref[...]
    o_ref[:, :half] = _gelu_tanh(y_lo + dd[:, :half] * u_lo.astype(F32)).astype(BF16)
    o_ref[:, half:] = _gelu_tanh(y_hi + dd[:, half:] * u_hi.astype(F32)).astype(BF16)


def _ssm(ucat, xw, yw, z8, zx, t1, t2, a16, dcat):
    bsz, n_lg, n_chunks, width = ucat.shape
    ns2 = t1.shape[-1]
    half = width // 2

    def per_lg(a):
        return pl.BlockSpec((None,) + a.shape[1:], lambda g, b: (g, 0, 0))

    def const(a):
        return pl.BlockSpec(a.shape, lambda g, b: (0, 0))

    return pl.pallas_call(
        _ssm_kernel,
        grid=(n_lg, bsz),
        in_specs=[
            pl.BlockSpec((None, None, n_chunks, width), lambda g, b: (b, g, 0, 0)),
            per_lg(xw), per_lg(yw), per_lg(z8), per_lg(zx), const(t1), const(t2), per_lg(a16), per_lg(dcat),
        ],
        out_specs=pl.BlockSpec((None, None, n_chunks, width), lambda g, b: (b, g, 0, 0)),
        out_shape=jax.ShapeDtypeStruct(ucat.shape, BF16),
        scratch_shapes=[
            pltpu.VMEM((width, ns2), BF16), pltpu.VMEM((ns2, width), BF16),
            pltpu.VMEM((half, half), BF16), pltpu.VMEM((half, half), BF16),
            pltpu.VMEM((n_chunks, ns2), F32), pltpu.VMEM((n_chunks, ns2), F32),
        ],
        compiler_params=_cparams(("arbitrary", "arbitrary")),
        name="ssm_scan",
    )(ucat, xw, yw, z8, zx, t1, t2, a16, dcat)


def _ssm_post_kernel(z_ref, x_ref, w_ref, mod_ref, g_ref, o_ref, zp, rp):
    n_lg, n, _ = z_ref.shape
    d = x_ref.shape[1]
    for s in range(SSM_L):
        for lg in range(n_lg):
            zp[s * n:(s + 1) * n, lg * LANES:(lg + 1) * LANES] = z_ref[lg, :, s * LANES:(s + 1) * LANES]
    gl = _dot(zp[...], w_ref[...])
    h = gl[:, :d] * _sigmoid(gl[:, d:])
    r = mod_ref[2:3, :] * _rms(h, g_ref[1:2, :])
    for lg in range(n_lg):
        rp[lg] = r[:, lg * LANES:(lg + 1) * LANES]
    sub = 8
    for c in range(n):
        for k in range(SSM_L // sub):
            rows = pl.ds(c * SSM_L + k * sub, sub)
            for lg in range(n_lg):
                cols = slice(lg * LANES, (lg + 1) * LANES)
                o_ref[rows, cols] = x_ref[rows, cols] + rp[lg, pl.ds(k * sub * n + c, sub, stride=n), :]


def _ssm_post(zcat, x, w_glu, mod_l, g_l, tm):
    bsz, seq, d = x.shape
    n_lg = d // LANES
    return pl.pallas_call(
        _ssm_post_kernel,
        grid=(bsz, seq // tm),
        in_specs=[
            pl.BlockSpec((None, n_lg, tm // SSM_L, SSM_L * LANES), lambda b, i: (b, 0, i, 0)),
            pl.BlockSpec((None, tm, d), lambda b, i: (b, i, 0)),
            pl.BlockSpec((d, 2 * d), lambda b, i: (0, 0)),
            pl.BlockSpec((None, 6, d), lambda b, i: (b, 0, 0)),
            pl.BlockSpec((4, d), lambda b, i: (0, 0)),
        ],
        out_specs=pl.BlockSpec((None, tm, d), lambda b, i: (b, i, 0)),
        out_shape=jax.ShapeDtypeStruct(x.shape, F32),
        scratch_shapes=[pltpu.VMEM((tm, d), BF16), pltpu.VMEM((n_lg, tm, LANES), F32)],
        compiler_params=_cparams(("parallel", "parallel")),
        name="ssm_post",
    )(zcat, x, w_glu, mod_l, g_l)


def _ssm_weights(a_re, a_im, log_dt, b_re, b_im, c_re, c_im, d_skip):
    n_groups = a_re.shape[0]
    n_lg = n_groups // SSM_LG
    p, hh, ll = SSM_STATE, SSM_GROUP, SSM_L
    dt = jnp.exp(log_dt)[:, None]
    k = jnp.arange(ll + 1, dtype=F32)[:, None, None]
    mag = jnp.exp(k * (a_re * dt))
    pw_re = mag * jnp.cos(k * (a_im * dt))
    pw_im = mag * jnp.sin(k * (a_im * dt))
    den = a_re * a_re + a_im * a_im
    num_re, num_im = pw_re[1] - 1.0, pw_im[1]
    f_re = (num_re * a_re + num_im * a_im) / den
    f_im = (num_im * a_re - num_re * a_im) / den
    bb_re = f_re[..., None] * b_re - f_im[..., None] * b_im
    bb_im = f_re[..., None] * b_im + f_im[..., None] * b_re
    pb_re = pw_re[:ll, :, :, None] * bb_re[None] - pw_im[:ll, :, :, None] * bb_im[None]
    pb_im = pw_re[:ll, :, :, None] * bb_im[None] + pw_im[:ll, :, :, None] * bb_re[None]
    pb = jnp.stack([pb_re[::-1], pb_im[::-1]], axis=1)
    pb = pb.reshape(ll, 2, n_lg, SSM_LG, p, hh)
    xw = pb.transpose(2, 0, 3, 5, 1, 4).reshape(n_lg, ll * LANES, 2 * p)
    cp_re = c_re[None] * pw_re[1:, :, None, :] - c_im[None] * pw_im[1:, :, None, :]
    cp_im = c_re[None] * pw_im[1:, :, None, :] + c_im[None] * pw_re[1:, :, None, :]
    cp = jnp.stack([cp_re, -cp_im], axis=1).reshape(ll, 2, n_lg, SSM_LG, hh, p)
    yw = cp.transpose(2, 1, 3, 5, 0, 4).reshape(n_lg, 2 * SSM_LG * p, ll * hh)
    kt = (jnp.einsum('ghp,tgp,gpi->tghi', c_re, pw_re[:ll], bb_re, precision=HIGHEST)
          - jnp.einsum('ghp,tgp,gpi->tghi', c_re, pw_im[:ll], bb_im, precision=HIGHEST)
          - jnp.einsum('ghp,tgp,gpi->tghi', c_im, pw_re[:ll], bb_im, precision=HIGHEST)
          - jnp.einsum('ghp,tgp,gpi->tghi', c_im, pw_im[:ll], bb_re, precision=HIGHEST))
    kt = jnp.concatenate([kt, jnp.zeros_like(kt)], axis=0)
    hl = ll // 2
    s_in = jnp.arange(hl)[:, None]
    s_out = jnp.arange(hl)[None, :]
    lag8 = (s_out - s_in) % (2 * ll)
    lagx = s_out - s_in + hl

    def toeplitz(lag):
        t = kt[lag].reshape(hl, hl, n_lg, SSM_LG, hh, hh)
        return t.transpose(2, 0, 3, 5, 1, 4).reshape(n_lg, hl * LANES, hl * hh)

    z8 = toeplitz(lag8)
    zx = toeplitz(lagx)
    j1 = jnp.arange(2 * SSM_LG * p)
    t1 = (jnp.arange(2 * p)[:, None] == ((j1 // (SSM_LG * p)) * p + j1 % p)[None, :])
    j2 = jnp.arange(ll * LANES)
    t2 = (jnp.arange(ll * hh)[:, None] == ((j2 // LANES) * hh + j2 % hh)[None, :])
    a16 = jnp.stack([pw_re[ll], pw_im[ll]], axis=0).reshape(2, n_lg, SSM_LG * p).transpose(1, 0, 2)
    dcat = jnp.tile(d_skip.reshape(n_lg, 1, LANES), (1, 1, ll))
    return (xw.astype(BF16), yw.astype(BF16), z8.astype(BF16), zx.astype(BF16),
            t1.astype(BF16), t2.astype(BF16), a16, dcat)


def _ffn_kernel(x_ref, wg_ref, wu_ref, wo_ref, mod_ref, g_ref, o_ref, h_scr, acc):
    k = pl.program_id(2)

    @pl.when(k == 0)
    def _():
        h = _rms(x_ref[...], g_ref[2:3, :]) * (1.0 + mod_ref[4:5, :]) + mod_ref[3:4, :]
        h_scr[...] = h.astype(BF16)
        acc[...] = jnp.zeros_like(acc)

    h = h_scr[...]
    g = _dot(h, wg_ref[...])
    u = _dot(h, wu_ref[...])
    act = (g * _sigmoid(g) * u).astype(BF16)
    acc[...] += _dot(act, wo_ref[...])

    @pl.when(k == pl.num_programs(2) - 1)
    def _():
        o_ref[...] = x_ref[...] + mod_ref[5:6, :] * _rms(acc[...], g_ref[3:4, :])


def _ffn(x, w_in, w_out, mod_l, g_l, tm, tf):
    bsz, seq, d = x.shape
    f = w_out.shape[0]
    nk = f // tf
    return pl.pallas_call(
        _ffn_kernel,
        grid=(bsz, seq // tm, nk),
        in_specs=[
            pl.BlockSpec((None, tm, d), lambda b, i, k: (b, i, 0)),
            pl.BlockSpec((d, tf), lambda b, i, k: (0, k)),
            pl.BlockSpec((d, tf), lambda b, i, k: (0, k + nk)),
            pl.BlockSpec((tf, d), lambda b, i, k: (k, 0)),
            pl.BlockSpec((None, 6, d), lambda b, i, k: (b, 0, 0)),
            pl.BlockSpec((4, d), lambda b, i, k: (0, 0)),
        ],
        out_specs=pl.BlockSpec((None, tm, d), lambda b, i, k: (b, i, 0)),
        out_shape=jax.ShapeDtypeStruct(x.shape, F32),
        scratch_shapes=[pltpu.VMEM((tm, d), BF16), pltpu.VMEM((tm, d), F32)],
        compiler_params=_cparams(("parallel", "parallel", "arbitrary")),
        name="ffn",
    )(x, w_in, w_in, w_out, mod_l, g_l)


def _rope_kernel(pos_ref, f_ref, cos_ref, sin_ref):
    ang = pos_ref[...].astype(F32) * f_ref[...]
    cos_ref[...] = jnp.cos(ang)
    sin_ref[...] = jnp.sin(ang)


def _rope_tables(positions):
    bsz, seq = positions.shape
    half = MLA_ROPE // 2
    inv_freq = ROPE_THETA ** (-jnp.arange(half, dtype=F32) / half)
    per_row = LANES // half
    rows = bsz * seq // per_row
    pos_rep = jnp.broadcast_to(positions[..., None], (bsz, seq, half)).reshape(rows, LANES)
    f_rep = jnp.tile(inv_freq, per_row).reshape(1, LANES)
    tr = min(rows, 512)
    cos, sin = pl.pallas_call(
        _rope_kernel,
        grid=(rows // tr,),
        in_specs=[pl.BlockSpec((tr, LANES), lambda i: (i, 0)), pl.BlockSpec((1, LANES), lambda i: (0, 0))],
        out_specs=[pl.BlockSpec((tr, LANES), lambda i: (i, 0))] * 2,
        out_shape=[jax.ShapeDtypeStruct((rows, LANES), F32)] * 2,
        compiler_params=_cparams(("parallel",)),
        name="rope_tables",
    )(pos_rep, f_rep)
    cos = cos.reshape(bsz, seq, half)
    sin = sin.reshape(bsz, seq, half)
    ones = jnp.ones((bsz, seq, MLA_NOPE), F32)
    zpad = jnp.zeros((bsz, seq, LANES - MLA_QK), F32)
    cos_t = jnp.concatenate([ones, cos, cos, zpad], axis=-1)
    sin_t = jnp.concatenate([0.0 * ones, sin, sin, zpad], axis=-1)
    return cos_t, sin_t


def _mla_proj_kernel(x_ref, cos_ref, sin_ref, mod_ref, g_ref, wcq_ref, wckv_ref, wkr_ref, qn_ref, kvn_ref,
                     wq_ref, wqs_ref, wk_ref, wv_ref, perm_ref, exp_ref, q_ref, k_ref, v_ref):
    h = (_rms(x_ref[...], g_ref[0:1, :]) * (1.0 + mod_ref[1:2, :]) + mod_ref[0:1, :]).astype(BF16)
    cq = _rms(_dot(h, wcq_ref[...]), qn_ref[...]).astype(BF16)
    ckv = _rms(_dot(h, wckv_ref[...]), kvn_ref[...]).astype(BF16)
    kr = _dot(h, wkr_ref[...])
    cos = cos_ref[...]
    sin = sin_ref[...]
    q = _dot(cq, wq_ref[...])
    qs = _dot(cq, wqs_ref[...])
    for hd in range(MLA_HEADS):
        sl = slice(hd * LANES, (hd + 1) * LANES)
        q_ref[:, sl] = (q[:, sl] * cos + qs[:, sl] * sin).astype(BF16)
    kr_rot = (kr * cos + _dot(kr.astype(BF16), perm_ref[...]) * sin).astype(BF16)
    k_ref[...] = (_dot(ckv, wk_ref[...]) + _dot(kr_rot, exp_ref[...])).astype(BF16)
    vt = lax.dot_general(wv_ref[...], ckv, (((1,), (1,)), ((), ())), preferred_element_type=F32)
    row = lax.broadcasted_iota(jnp.int32, vt.shape, 0)
    v_ref[...] = jnp.where(row % V_ROWS == MLA_V, 1.0, vt).astype(BF16)


def _mla_proj(x, cos_t, sin_t, mod_l, g_l, wts, tm):
    bsz, seq, d = x.shape
    hw = MLA_HEADS * LANES

    def full(a):
        return pl.BlockSpec(a.shape, lambda b, i: (0,) * a.ndim)

    out_sds = jax.ShapeDtypeStruct((bsz, seq, hw), BF16)
    vt_rows = MLA_HEADS * V_ROWS
    vt_sds = jax.ShapeDtypeStruct((bsz, seq // tm, vt_rows, tm), BF16)
    return pl.pallas_call(
        _mla_proj_kernel,
        grid=(bsz, seq // tm),
        in_specs=[
            pl.BlockSpec((None, tm, d), lambda b, i: (b, i, 0)),
            pl.BlockSpec((None, tm, LANES), lambda b, i: (b, i, 0)),
            pl.BlockSpec((None, tm, LANES), lambda b, i: (b, i, 0)),
            pl.BlockSpec((None, 6, d), lambda b, i: (b, 0, 0)),
            pl.BlockSpec((4, d), lambda b, i: (0, 0)),
        ] + [full(w) for w in wts],
        out_specs=[pl.BlockSpec((None, tm, hw), lambda b, i: (b, i, 0))] * 2
        + [pl.BlockSpec((None, None, vt_rows, tm), lambda b, i: (b, i, 0, 0))],
        out_shape=[out_sds, out_sds, vt_sds],
        compiler_params=_cparams(("parallel", "parallel")),
        name="mla_proj",
    )(x, cos_t, sin_t, mod_l, g_l, *wts)


def _mla_weights(w_in, q_norm, kv_norm, w_uq, w_ukv, w_o):
    d = w_in.shape[0]
    half = MLA_ROPE // 2
    pad = LANES - MLA_QK
    w_cq = w_in[:, :MLA_Q_RANK]
    w_ckv = w_in[:, MLA_Q_RANK:MLA_Q_RANK + MLA_KV_RANK]
    w_kr = jnp.pad(w_in[:, MLA_Q_RANK + MLA_KV_RANK:], ((0, 0), (MLA_NOPE, pad)))
    scale = MLA_QK ** -0.5 * math.log2(math.e)
    wq =w_uq.reshape(MLA_Q_RANK, MLA_HEADS, MLA_QK) * scale
    nope, x1, x2 = wq[..., :MLA_NOPE], wq[..., MLA_NOPE:MLA_NOPE + half], wq[..., MLA_NOPE + half:]
    zp = jnp.zeros((MLA_Q_RANK, MLA_HEADS, pad), F32)
    wq_pad = jnp.concatenate([nope, x1, x2, zp], axis=-1).reshape(MLA_Q_RANK, MLA_HEADS * LANES)
    wq_sw = jnp.concatenate([0.0 * nope, -x2, x1, zp], axis=-1).reshape(MLA_Q_RANK, MLA_HEADS * LANES)
    wkv = w_ukv.reshape(MLA_KV_RANK, MLA_HEADS, MLA_NOPE + MLA_V)
    zk = jnp.zeros((MLA_KV_RANK, MLA_HEADS, LANES - MLA_NOPE), F32)
    wk_pad = jnp.concatenate([wkv[..., :MLA_NOPE], zk], axis=-1).reshape(MLA_KV_RANK, MLA_HEADS * LANES)
    wv_t = jnp.pad(wkv[..., MLA_NOPE:].transpose(1, 2, 0), ((0, 0), (0, V_ROWS - MLA_V), (0, 0)))
    wv_t = wv_t.reshape(MLA_HEADS * V_ROWS, MLA_KV_RANK)
    lane = jnp.arange(LANES)
    src = jnp.where((lane >= MLA_NOPE) & (lane < MLA_NOPE + half), lane + half,
                    jnp.where((lane >= MLA_NOPE + half) & (lane < MLA_QK), lane - half, -1))
    sign = jnp.where(lane < MLA_NOPE + half, -1.0, 1.0)
    perm = (lane[:, None] == src[None, :]).astype(F32) * sign[None, :]
    is_rope = (lane >= MLA_NOPE) & (lane < MLA_QK)
    expand = jnp.tile((jnp.eye(LANES, dtype=F32) * is_rope[None, :].astype(F32)), (1, MLA_HEADS))
    proj_w = (w_cq.astype(BF16), w_ckv.astype(BF16), w_kr.astype(BF16),
              q_norm.reshape(1, -1), kv_norm.reshape(1, -1),
              wq_pad.astype(BF16), wq_sw.astype(BF16), wk_pad.astype(BF16), wv_t.astype(BF16),
              perm.astype(BF16), expand.astype(BF16))
    return proj_w, w_o.astype(BF16)


def _attn_kernel(q_ref, k_ref, vt_ref, o_ref, m_scr, mt_scr, acc, s0_scr, s1_scr):
    tk = vt_ref.shape[2]
    qi = pl.program_id(2)
    s_scr = (s0_scr, s1_scr)
    m_scr[...] = jnp.full_like(m_scr, -jnp.inf)
    acc[...] = jnp.zeros_like(acc)

    def scores(hd, j):
        lanes = slice(hd * LANES, (hd + 1) * LANES)
        k = k_ref[pl.ds(pl.multiple_of(j * tk, tk), tk), lanes]
        st = lax.dot_general(k, q_ref[:, lanes], (((1,), (1,)), ((), ())), preferred_element_type=F32)
        s_scr[hd][...] = st
        mt_scr[hd] = jnp.max(st, axis=0, keepdims=True)

    def accumulate(hd, j, masked):
        st = s_scr[hd][...]
        if masked:
            key = lax.broadcasted_iota(jnp.int32, st.shape, 0)
            qry = lax.broadcasted_iota(jnp.int32, st.shape, 1)
            st = jnp.where(key <= qry, st, -jnp.inf)
            mt = jnp.max(st, axis=0, keepdims=True)
        else:
            mt = mt_scr[hd]
        m_prev = m_scr[hd]
        m_new = jnp.maximum(m_prev, mt)
        alpha = jnp.exp2(m_prev - m_new)
        pt = jnp.exp2(st - m_new).astype(BF16)
        vt = vt_ref[j, hd * V_ROWS:(hd + 1) * V_ROWS, :]
        acc[hd] = alpha * acc[hd] + _dot(vt, pt)
        m_scr[hd] = m_new

    scores(0, 0)

    def body(j, carry):
        scores(1, j)
        accumulate(0, j, False)
        scores(0, j + 1)
        accumulate(1, j, False)
        return carry

    def body_pair(jj, carry):
        body(2 * jj, carry)
        return body(2 * jj + 1, carry)

    lax.fori_loop(0, qi // 2, body_pair, 0)

    @pl.when(qi % 2 == 1)
    def _():
        body(qi - 1, 0)

    scores(1, qi)
    accumulate(0, qi, True)
    accumulate(1, qi, True)
    outs = []
    for hd in range(2):
        a = acc[hd]
        outs.append(a[:MLA_V, :] / a[MLA_V:MLA_V + 1, :])
    o_ref[...] = jnp.concatenate(outs, axis=0).T.astype(BF16)


def _attention(q, k, vt, tq):
    bsz, seq, hw = q.shape
    n_pairs = hw // (2 * LANES)
    n_kt, _, tk = vt.shape[1:]
    assert tq == tk, "the causal diagonal tile assumes square tiles"
    return pl.pallas_call(
        _attn_kernel,
        grid=(bsz, n_pairs, seq // tq),
        in_specs=[
            pl.BlockSpec((None, tq, 2 * LANES), lambda b, h, i: (b, i, h)),
            pl.BlockSpec((None, seq, 2 * LANES), lambda b, h, i: (b, 0, h)),
            pl.BlockSpec((None, n_kt, 2 * V_ROWS, tk), lambda b, h, i: (b, 0, h, 0)),
        ],
        out_specs=pl.BlockSpec((None, tq, 2 * MLA_V), lambda b, h, i: (b, i, h)),
        out_shape=jax.ShapeDtypeStruct((bsz, seq, n_pairs * 2 * MLA_V), BF16),
        scratch_shapes=[pltpu.VMEM((2, 1, tq), F32), pltpu.VMEM((2, 1, tq), F32), pltpu.VMEM((2, V_ROWS, tq), F32),
                        pltpu.VMEM((tk, tq), F32), pltpu.VMEM((tk, tq), F32)],
        compiler_params=_cparams(("parallel", "parallel", "arbitrary")),
        name="mla_attention",
    )(q, k, vt)


def _attn_out_kernel(a_ref, x_ref, w_ref, mod_ref, g_ref, o_ref):
    h = _dot(a_ref[...], w_ref[...])
    o_ref[...] = x_ref[...] + mod_ref[2:3, :] * _rms(h, g_ref[1:2, :])


def _attn_out(a, x, wo_pad, mod_l, g_l, tm):
    bsz, seq, d = x.shape
    hw = a.shape[-1]
    return pl.pallas_call(
        _attn_out_kernel,
        grid=(bsz, seq // tm),
        in_specs=[
            pl.BlockSpec((None, tm, hw), lambda b, i: (b, i, 0)),
            pl.BlockSpec((None, tm, d), lambda b, i: (b, i, 0)),
            pl.BlockSpec((hw, d), lambda b, i: (0, 0)),
            pl.BlockSpec((None, 6, d), lambda b, i: (b, 0, 0)),
            pl.BlockSpec((4, d), lambda b, i: (0, 0)),
        ],
        out_specs=pl.BlockSpec((None, tm, d), lambda b, i: (b, i, 0)),
        out_shape=jax.ShapeDtypeStruct(x.shape, F32),
        compiler_params=_cparams(("parallel", "parallel")),
        name="mla_out",
    )(a, x, wo_pad, mod_l, g_l)


def _route(logits):
    lane = lax.broadcasted_iota(jnp.int32, logits.shape, 1)
    neg = -jnp.inf
    lg = jnp.where(lane < N_EXPERTS, logits, neg)
    m1 = jnp.max(lg, axis=-1, keepdims=True)
    i1 = jnp.min(jnp.where(lg == m1, lane, LANES), axis=-1, keepdims=True)
    lg2 = jnp.where(lane == i1, neg, lg)
    m2 = jnp.max(lg2, axis=-1, keepdims=True)
    i2 = jnp.min(jnp.where(lg2 == m2, lane, LANES), axis=-1, keepdims=True)
    e = jnp.exp(m2 - m1)
    w1 = 1.0 / (1.0 + e)
    w2 = e / (1.0 + e)
    first = lane == i1
    second = lane == i2
    comb = jnp.where(first, w1, 0.0) + jnp.where(second, w2, 0.0)
    sel = jnp.where(first, 1.0, 0.0) + jnp.where(second, 1.0, 0.0)
    return comb, sel


def _moe_kernel(x_ref, wr_ref, br_ref, tri_ref, wg_ref, wu_ref, wo_ref, mod_ref, g_ref, o_ref,
                h_scr, sel_scr, rank_scr, w_scr, acc):
    e = pl.program_id(1)
    tb = x_ref.shape[0]

    @pl.when(e == 0)
    def _():
        h = _rms(x_ref[...], g_ref[2:3, :]) * (1.0 + mod_ref[4:5, :]) + mod_ref[3:4, :]
        logits = jnp.dot(h, wr_ref[...], preferred_element_type=F32, precision=HIGHEST) + br_ref[...]
        comb, sel = _route(logits)
        sel_t = sel.T
        sel_scr[...] = sel_t
        w_scr[...] = comb.T
        rank_scr[...] = _dot(sel_t.astype(BF16), tri_ref[...])
        h_scr[...] = h.astype(BF16)
        acc[...] = jnp.zeros_like(acc)

    sel_row = sel_scr[pl.ds(e, 1), :]
    rank_row = rank_scr[pl.ds(e, 1), :]
    w_row = w_scr[pl.ds(e, 1), :]
    n_rows = jnp.sum(sel_row).astype(jnp.int32)

    def step(r, carry):
        slot = (lax.broadcasted_iota(jnp.int32, (MOE_SUB, tb), 0) + r * MOE_SUB).astype(F32)
        hit = rank_row == slot
        onehot = jnp.where(hit, sel_row, 0.0).astype(BF16)
        xs = _dot(onehot, h_scr[...]).astype(BF16)
        g = _dot(xs, wg_ref[...])
        u = _dot(xs, wu_ref[...])
        act = (g * _sigmoid(g) * u).astype(BF16)
        y = _dot(act, wo_ref[...])
        w_col = jnp.sum(jnp.where(hit, w_row, 0.0), axis=1, keepdims=True)
        yw = (y * w_col).astype(BF16)
        acc[...] += lax.dot_general(onehot, yw, (((0,), (0,)), ((), ())), preferred_element_type=F32)
        return carry

    lax.fori_loop(0, (n_rows + MOE_SUB - 1) // MOE_SUB, step, 0)

    @pl.when(e == pl.num_programs(1) - 1)
    def _():
        o_ref[...] = x_ref[...] + mod_ref[5:6, :] * _rms(acc[...], g_ref[3:4, :])


def _moe_layer(x, w_router, b_router, w_in, w_out, mod_l, g_l):
    bsz, seq, d = x.shape
    n_e, f, _ = w_out.shape
    tb = min(MOE_TB, seq)
    blocks_per_batch = seq // tb
    wr = jnp.pad(w_router, ((0, 0), (0, LANES - n_e)))
    br = jnp.pad(b_router, (0, LANES - n_e)).reshape(1, LANES)
    pos = jnp.arange(tb)
    tri = (pos[:, None] < pos[None, :]).astype(BF16)
    w_in = w_in.astype(BF16)
    out = pl.pallas_call(
        _moe_kernel,
        grid=(bsz * blocks_per_batch, n_e),
        in_specs=[
            pl.BlockSpec((tb, d), lambda i, e: (i, 0)),
            pl.BlockSpec((d, LANES), lambda i, e: (0, 0)),
            pl.BlockSpec((1, LANES), lambda i, e: (0, 0)),
            pl.BlockSpec((tb, tb), lambda i, e: (0, 0)),
            pl.BlockSpec((None, d, f), lambda i, e: (e, 0, 0)),
            pl.BlockSpec((None, d, f), lambda i, e: (e, 0, 1)),
            pl.BlockSpec((None, f, d), lambda i, e: (e, 0, 0)),
            pl.BlockSpec((None, 6, d), lambda i, e: (i // blocks_per_batch, 0, 0)),
            pl.BlockSpec((4, d), lambda i, e: (0, 0)),
        ],
        out_specs=pl.BlockSpec((tb, d), lambda i, e: (i, 0)),
        out_shape=jax.ShapeDtypeStruct((bsz * seq, d), F32),
        scratch_shapes=[pltpu.VMEM((tb, d), BF16), pltpu.VMEM((LANES, tb), F32), pltpu.VMEM((LANES, tb), F32),
                        pltpu.VMEM((LANES, tb), F32), pltpu.VMEM((tb, d), F32)],
        compiler_params=_cparams(("parallel", "arbitrary")),
        name="moe",
    )(x.reshape(bsz * seq, d), wr, br, tri, w_in, w_in, w_out.astype(BF16), mod_l, g_l)
    return out.reshape(bsz, seq, d)


def kernel(x, c, positions, norm_g, w_ada, b_ada, ssm_a_re, ssm_a_im, ssm_log_dt, ssm_b_re, ssm_b_im, ssm_c_re, ssm_c_im, ssm_d, ssm_w_glu, ffn_w_in, ffn_w_out, mla_w_in, mla_q_norm, mla_kv_norm, mla_w_uq, mla_w_ukv, mla_w_o, moe_w_router, moe_b_router, moe_w_in, moe_w_out):
    depth = norm_g.shape[0]
    seq = x.shape[1]
    tm = min(512, seq)
    mod = _ada(c, w_ada, b_ada)
    cos_t = sin_t = None
    for i in range(depth):
        j = i // 2
        mod_l, g_l = mod[i], norm_g[i]
        if i % 2 == 0:
            ssm_w = _ssm_weights(ssm_a_re[j], ssm_a_im[j], ssm_log_dt[j], ssm_b_re[j], ssm_b_im[j],
                                 ssm_c_re[j], ssm_c_im[j], ssm_d[j])
            ucat = _ssm_pre(x, mod_l, g_l, tm)
            zcat = _ssm(ucat, *ssm_w)
            x = _ssm_post(zcat, x, ssm_w_glu[j].astype(BF16), mod_l, g_l, tm)
            f = ffn_w_out.shape[1]
            x = _ffn(x, ffn_w_in[j].astype(BF16), ffn_w_out[j].astype(BF16), mod_l, g_l, tm, f // 2)
        else:
            if cos_t is None:
                cos_t, sin_t = _rope_tables(positions)
            proj_w, wo_pad = _mla_weights(mla_w_in[j], mla_q_norm[j], mla_kv_norm[j],
                                          mla_w_uq[j], mla_w_ukv[j], mla_w_o[j])
            q, k, v = _mla_proj(x, cos_t, sin_t, mod_l, g_l, proj_w, tm)
            a = _attention(q, k, v, tm)
            x = _attn_out(a, x, wo_pad, mod_l, g_l, tm)
            x = _moe_layer(x, moe_w_router[j], moe_b_router[j], moe_w_in[j], moe_w_out[j], mod_l, g_l)
    return x
```

```python
import functools
import math

import jax
import jax.numpy as jnp
from jax import lax
from jax.experimental import pallas as pl
from jax.experimental.pallas import tpu as pltpu

F32 = jnp.float32
BF16 = jnp.bfloat16
HIGHEST = lax.Precision.HIGHEST

RMS_EPS = 1e-6
LANES = 128
SSM_GROUP = 16
SSM_STATE = 64
SSM_L = 16
SSM_LG = LANES // SSM_GROUP
MLA_HEADS = 16
MLA_NOPE = 64
MLA_ROPE = 32
MLA_V = 64
MLA_QK = MLA_NOPE + MLA_ROPE
V_ROWS = 80
MLA_Q_RANK = 384
MLA_KV_RANK = 256
ROPE_THETA = 10000.0
N_EXPERTS = 8
MOE_TB = 1024
MOE_SUB = 144
VMEM_LIMIT = 56 * 1024 * 1024


def _cparams(sem):
    return pltpu.CompilerParams(dimension_semantics=sem, vmem_limit_bytes=VMEM_LIMIT)


def _rms(x, g):
    return x * lax.rsqrt(jnp.mean(x * x, axis=-1, keepdims=True) + RMS_EPS) * g


def _sigmoid(x):
    return 1.0 / (1.0 + jnp.exp(-x))


def _dot(a, b):
    return jnp.dot(a, b, preferred_element_type=F32)


def _ada_kernel(c_ref, w_ref, b_ref, o_ref):
    c = c_ref[...]
    c_act = c * _sigmoid(c)
    o_ref[...] = jnp.dot(c_act, w_ref[...], preferred_element_type=F32, precision=HIGHEST) + b_ref[...]


def _ada(c, w_ada, b_ada):
    depth, d, _ = w_ada.shape
    bsz = c.shape[0]
    rows = 8
    c_pad = jnp.pad(c, ((0, rows - bsz), (0, 0)))
    out = pl.pallas_call(
        _ada_kernel,
        grid=(depth, 6),
        in_specs=[
            pl.BlockSpec((rows, d), lambda i, j: (0, 0)),
            pl.BlockSpec((None, d, d), lambda i, j: (i, 0, j)),
            pl.BlockSpec((None, None, 1, d), lambda i, j: (i, j, 0, 0)),
        ],
        out_specs=pl.BlockSpec((None, None, rows, d), lambda i, j: (i, j, 0, 0)),
        out_shape=jax.ShapeDtypeStruct((depth, 6, rows, d), F32),
        compiler_params=_cparams(("arbitrary", "arbitrary")),
        name="ada_mod",
    )(c_pad, w_ada, b_ada.reshape(depth, 6, 1, d))
    return out[:, :, :bsz, :].transpose(0, 2, 1, 3)


def _ssm_pre_kernel(x_ref, mod_ref, g_ref, o_ref, scr):
    tm = x_ref.shape[0]
    n = tm // SSM_L
    u = _rms(x_ref[...], g_ref[0:1, :]) * (1.0 + mod_ref[1:2, :]) + mod_ref[0:1, :]
    n_lg = o_ref.shape[0]
    for lg in range(n_lg):
        scr[lg] = u[:, lg * LANES:(lg + 1) * LANES]
    for s in range(SSM_L):
        for lg in range(n_lg):
            o_ref[lg, :, s * LANES:(s + 1) * LANES] = scr[lg, pl.ds(s, n, stride=SSM_L), :].astype(BF16)


def _ssm_pre(x, mod_l, g_l, tm):
    bsz, seq, d = x.shape
    n_lg = d // LANES
    return pl.pallas_call(
        _ssm_pre_kernel,
        grid=(bsz, seq // tm),
        in_specs=[
            pl.BlockSpec((None, tm, d), lambda b, i: (b, i, 0)),
            pl.BlockSpec((None, 6, d), lambda b, i: (b, 0, 0)),
            pl.BlockSpec((4, d), lambda b, i: (0, 0)),
        ],
        out_specs=pl.BlockSpec((None, n_lg, tm // SSM_L, SSM_L * LANES), lambda b, i: (b, 0, i, 0)),
        out_shape=jax.ShapeDtypeStruct((bsz, n_lg, seq // SSM_L, SSM_L * LANES), BF16),
        scratch_shapes=[pltpu.VMEM((n_lg, tm, LANES), F32)],
        compiler_params=_cparams(("parallel", "parallel")),
        name="ssm_pre",
    )(x, mod_l, g_l)


def _gelu_tanh(y):
    return 0.5 * y * (1.0 + jnp.tanh(math.sqrt(2.0 / math.pi) * (y + 0.044715 * (y * y * y))))


def _same_group(shape, row_shift, col_shift):
    row_g = (lax.broadcasted_iota(jnp.int32, shape, 0) >> row_shift) & (SSM_LG - 1)
    col_g = (lax.broadcasted_iota(jnp.int32, shape, 1) >> col_shift) & (SSM_LG - 1)
    return row_g == col_g


def _ssm_expand_weights(xw_ref, yw_ref, cc_ref, t1_ref, t1t_ref, win_s, wout_s, m8_s, mx_s):
    hl = SSM_L // 2
    in_mask = _same_group((LANES, t1_ref.shape[1]), 4, 6)
    out_mask = _same_group((t1_ref.shape[1], LANES), 6, 4)
    k_mask = _same_group((LANES, LANES), 4, 4)
    m8_s[...] = jnp.zeros_like(m8_s)
    for s in range(SSM_L):
        rows = slice(s * LANES, (s + 1) * LANES)
        val = lax.dot_general(xw_ref[s], t1_ref[...], (((0,), (0,)), ((), ())), preferred_element_type=F32)
        win_s[rows, :] = jnp.where(in_mask, val, 0.0).astype(BF16)
        val = lax.dot_general(t1t_ref[...], yw_ref[s], (((1,), (1,)), ((), ())), preferred_element_type=F32)
        wout_s[:, rows] = jnp.where(out_mask, val, 0.0).astype(BF16)
    for tau in range(SSM_L):
        k_tau = lax.dot_general(xw_ref[SSM_L - 1 - tau], cc_ref[...], (((0,), (1,)), ((), ())),
                                preferred_element_type=F32)
        kbd = jnp.where(k_mask, k_tau, 0.0).astype(BF16)
        for a in range(hl):
            if a + tau < hl:
                m8_s[a * LANES:(a + 1) * LANES, (a + tau) * LANES:(a + tau + 1) * LANES] = kbd
            b = a + tau - hl
            if 0 <= b < hl:
                mx_s[a * LANES:(a + 1) * LANES, b * LANES:(b + 1) * LANES] = kbd


def _ssm_kernel(u_ref, xw_ref, yw_ref, cc_ref, t1_ref, t1t_ref, a_ref, d_ref, o_ref,
                win_s, wout_s, m8_s, mx_s, v_scr, h_scr):
    n_chunks = u_ref.shape[0]
    half = (SSM_L // 2) * LANES
    ns = a_ref.shape[1]

    @pl.when(pl.program_id(1) == 0)
    def _():
        _ssm_expand_weights(xw_ref, yw_ref, cc_ref, t1_ref, t1t_ref, win_s, wout_s, m8_s, mx_s)

    uc = u_ref[...]
    v_scr[...] = _dot(uc, win_s[...])
    ar = a_ref[0:1, :]
    ai = a_ref[1:2, :]

    def step(c, carry):
        hr, hi = carry
        h_scr[pl.ds(c, 1), pl.ds(0, ns)] = hr
        h_scr[pl.ds(c, 1), pl.ds(ns, ns)] = hi
        vr = v_scr[pl.ds(c, 1), pl.ds(0, ns)]
        vi = v_scr[pl.ds(c, 1), pl.ds(ns, ns)]
        return ar * hr - ai * hi + vr, ar * hi + ai * hr + vi

    zero = jnp.zeros((1, ns), F32)
    lax.fori_loop(0, n_chunks, step, (zero, zero))
    y = _dot(h_scr[...].astype(BF16), wout_s[...])
    u_lo = uc[:, :half]
    u_hi = uc[:, half:]
    m8 = m8_s[...]
    y_lo = y[:, :half] + _dot(u_lo, m8)
    y_hi = y[:, half:] + _dot(u_hi, m8) + _dot(u_lo, mx_s[...])
    dd = d_ref[...]
    o_ref[:, :half] = _gelu_tanh(y_lo + dd[:, :half] * u_lo.astype(F32)).astype(BF16)
    o_ref[:, half:] = _gelu_tanh(y_hi + dd[:, half:] * u_hi.astype(F32)).astype(BF16)


def _ssm(ucat, xw, yw, cc, t1, t1t, a16, dcat):
    bsz, n_lg, n_chunks, width = ucat.shape
    ns2 = t1.shape[-1]
    half = width // 2

    def per_lg(a):
        return pl.BlockSpec((None,) + a.shape[1:], lambda g, b: (g,) + (0,) * (a.ndim - 1))

    def const(a):
        return pl.BlockSpec(a.shape, lambda g, b: (0, 0))

    return pl.pallas_call(
        _ssm_kernel,
        grid=(n_lg, bsz),
        in_specs=[
            pl.BlockSpec((None, None, n_chunks, width), lambda g, b: (b, g, 0, 0)),
            per_lg(xw), per_lg(yw), per_lg(cc), const(t1), const(t1t), per_lg(a16), per_lg(dcat),
        ],
        out_specs=pl.BlockSpec((None, None, n_chunks, width), lambda g, b: (b, g, 0, 0)),
        out_shape=jax.ShapeDtypeStruct(ucat.shape, BF16),
        scratch_shapes=[
            pltpu.VMEM((width, ns2), BF16), pltpu.VMEM((ns2, width), BF16),
            pltpu.VMEM((half, half), BF16), pltpu.VMEM((half, half), BF16),
            pltpu.VMEM((n_chunks, ns2), F32), pltpu.VMEM((n_chunks, ns2), F32),
        ],
        compiler_params=_cparams(("arbitrary", "arbitrary")),
        name="ssm_scan",
    )(ucat, xw, yw, cc, t1, t1t, a16, dcat)


def _ssm_post_kernel(z_ref, x_ref, w_ref, mod_ref, g_ref, o_ref, zp, rp):
    n_lg, n, _ = z_ref.shape
    d = x_ref.shape[1]
    for s in range(SSM_L):
        for lg in range(n_lg):
            zp[s * n:(s + 1) * n, lg * LANES:(lg + 1) * LANES] = z_ref[lg, :, s * LANES:(s + 1) * LANES]
    gl = _dot(zp[...], w_ref[...])
    h = gl[:, :d] * _sigmoid(gl[:, d:])
    r = mod_ref[2:3, :] * _rms(h, g_ref[1:2, :])
    for lg in range(n_lg):
        rp[lg] = r[:, lg * LANES:(lg + 1) * LANES]
    sub = 8
    for c in range(n):
        for k in range(SSM_L // sub):
            rows = pl.ds(c * SSM_L + k * sub, sub)
            for lg in range(n_lg):
                cols = slice(lg * LANES, (lg + 1) * LANES)
                o_ref[rows, cols] = x_ref[rows, cols] + rp[lg, pl.ds(k * sub * n + c, sub, stride=n), :]


def _ssm_post(zcat, x, w_glu, mod_l, g_l, tm):
    bsz, seq, d = x.shape
    n_lg = d // LANES
    return pl.pallas_call(
        _ssm_post_kernel,
        grid=(bsz, seq // tm),
        in_specs=[
            pl.BlockSpec((None, n_lg, tm // SSM_L, SSM_L * LANES), lambda b, i: (b, 0, i, 0)),
            pl.BlockSpec((None, tm, d), lambda b, i: (b, i, 0)),
            pl.BlockSpec((d, 2 * d), lambda b, i: (0, 0)),
            pl.BlockSpec((None, 6, d), lambda b, i: (b, 0, 0)),
            pl.BlockSpec((4, d), lambda b, i: (0, 0)),
        ],
        out_specs=pl.BlockSpec((None, tm, d), lambda b, i: (b, i, 0)),
        out_shape=jax.ShapeDtypeStruct(x.shape, F32),
        scratch_shapes=[pltpu.VMEM((tm, d), BF16), pltpu.VMEM((n_lg, tm, LANES), F32)],
        compiler_params=_cparams(("parallel", "parallel")),
        name="ssm_post",
    )(zcat, x, w_glu, mod_l, g_l)


def _ssm_weights(a_re, a_im, log_dt, b_re, b_im, c_re, c_im, d_skip):
    n_groups = a_re.shape[0]
    n_lg = n_groups // SSM_LG
    p, hh, ll = SSM_STATE, SSM_GROUP, SSM_L
    dt = jnp.exp(log_dt)[:, None]
    k = jnp.arange(ll + 1, dtype=F32)[:, None, None]
    mag = jnp.exp(k * (a_re * dt))
    pw_re = mag * jnp.cos(k * (a_im * dt))
    pw_im = mag * jnp.sin(k * (a_im * dt))
    den = a_re * a_re + a_im * a_im
    num_re, num_im = pw_re[1] - 1.0, pw_im[1]
    f_re = (num_re * a_re + num_im * a_im) / den
    f_im = (num_im * a_re - num_re * a_im) / den
    bb_re = f_re[..., None] * b_re - f_im[..., None] * b_im
    bb_im = f_re[..., None] * b_im + f_im[..., None] * b_re
    b2_re, b2_im = (x.reshape(n_lg, SSM_LG, p, hh).transpose(0, 2, 1, 3).reshape(n_lg, p, LANES)
                    for x in (bb_re, bb_im))

    def lanes_g(x):
        x = x.reshape(x.shape[0], n_lg, SSM_LG, p).transpose(0, 1, 3, 2)
        return jnp.repeat(x, hh, axis=-1)

    l_re, l_im = lanes_g(pw_re[:ll][::-1]), lanes_g(pw_im[:ll][::-1])
    xw = jnp.concatenate([l_re * b2_re - l_im * b2_im, l_re * b2_im + l_im * b2_re], axis=2)
    xw = xw.transpose(1, 0, 2, 3)
    cp_re = c_re[None] * pw_re[1:, :, None, :] - c_im[None] * pw_im[1:, :, None, :]
    cp_im = c_re[None] * pw_im[1:, :, None, :] + c_im[None] * pw_re[1:, :, None, :]
    yw = jnp.concatenate([cp_re, -cp_im], axis=-1).reshape(ll, n_lg, LANES, 2 * p).transpose(1, 0, 2, 3)
    cc = jnp.concatenate([c_re, -c_im], axis=-1).reshape(n_lg, LANES, 2 * p)
    j1 = jnp.arange(2 * SSM_LG * p)
    t1 = (jnp.arange(2 * p)[:, None] == ((j1 // (SSM_LG * p)) * p + j1 % p)[None, :]).astype(BF16)
    a16 = jnp.stack([pw_re[ll], pw_im[ll]], axis=0).reshape(2, n_lg, SSM_LG * p).transpose(1, 0, 2)
    dcat = jnp.tile(d_skip.reshape(n_lg, 1, LANES), (1, 1, ll))
    return xw.astype(BF16), yw.astype(BF16), cc.astype(BF16), t1, t1.T, a16, dcat


def _ffn_kernel(x_ref, wg_ref, wu_ref, wo_ref, mod_ref, g_ref, o_ref, h_scr, acc):
    k = pl.program_id(2)

    @pl.when(k == 0)
    def _():
        h = _rms(x_ref[...], g_ref[2:3, :]) * (1.0 + mod_ref[4:5, :]) + mod_ref[3:4, :]
        h_scr[...] = h.astype(BF16)
        acc[...] = jnp.zeros_like(acc)

    h = h_scr[...]
    g = _dot(h, wg_ref[...])
    u = _dot(h, wu_ref[...])
    act = (g * _sigmoid(g) * u).astype(BF16)
    acc[...] += _dot(act, wo_ref[...])

    @pl.when(k == pl.num_programs(2) - 1)
    def _():
        o_ref[...] = x_ref[...] + mod_ref[5:6, :] * _rms(acc[...], g_ref[3:4, :])


def _ffn(x, w_in, w_out, mod_l, g_l, tm, tf):
    bsz, seq, d = x.shape
    f = w_out.shape[0]
    nk = f // tf
    return pl.pallas_call(
        _ffn_kernel,
        grid=(bsz, seq // tm, nk),
        in_specs=[
            pl.BlockSpec((None, tm, d), lambda b, i, k: (b, i, 0)),
            pl.BlockSpec((d, tf), lambda b, i, k: (0, k)),
            pl.BlockSpec((d, tf), lambda b, i, k: (0, k + nk)),
            pl.BlockSpec((tf, d), lambda b, i, k: (k, 0)),
            pl.BlockSpec((None, 6, d), lambda b, i, k: (b, 0, 0)),
            pl.BlockSpec((4, d), lambda b, i, k: (0, 0)),
        ],
        out_specs=pl.BlockSpec((None, tm, d), lambda b, i, k: (b, i, 0)),
        out_shape=jax.ShapeDtypeStruct(x.shape, F32),
        scratch_shapes=[pltpu.VMEM((tm, d), BF16), pltpu.VMEM((tm, d), F32)],
        compiler_params=_cparams(("parallel", "parallel", "arbitrary")),
        name="ffn",
    )(x, w_in, w_in, w_out, mod_l, g_l)


def _rope_kernel(pos_ref, f_ref, cos_ref, sin_ref):
    ang = pos_ref[...].astype(F32) * f_ref[...]
    cos_ref[...] = jnp.cos(ang)
    sin_ref[...] = jnp.sin(ang)


def _rope_tables(positions):
    bsz, seq = positions.shape
    half = MLA_ROPE // 2
    inv_freq = ROPE_THETA ** (-jnp.arange(half, dtype=F32) / half)
    per_row = LANES // half
    rows = bsz * seq // per_row
    pos_rep = jnp.broadcast_to(positions[..., None], (bsz, seq, half)).reshape(rows, LANES)
    f_rep = jnp.tile(inv_freq, per_row).reshape(1, LANES)
    tr = min(rows, 512)
    cos, sin = pl.pallas_call(
        _rope_kernel,
        grid=(rows // tr,),
        in_specs=[pl.BlockSpec((tr, LANES), lambda i: (i, 0)), pl.BlockSpec((1, LANES), lambda i: (0, 0))],
        out_specs=[pl.BlockSpec((tr, LANES), lambda i: (i, 0))] * 2,
        out_shape=[jax.ShapeDtypeStruct((rows, LANES), F32)] * 2,
        compiler_params=_cparams(("parallel",)),
        name="rope_tables",
    )(pos_rep, f_rep)
    cos = cos.reshape(bsz, seq, half)
    sin = sin.reshape(bsz, seq, half)
    ones = jnp.ones((bsz, seq, MLA_NOPE), F32)
    zpad = jnp.zeros((bsz, seq, LANES - MLA_QK), F32)
    cos_t = jnp.concatenate([ones, cos, cos, zpad], axis=-1)
    sin_t = jnp.concatenate([0.0 * ones, sin, sin, zpad], axis=-1)
    return cos_t, sin_t


def _mla_proj_kernel(x_ref, cos_ref, sin_ref, mod_ref, g_ref, wcq_ref, wckv_ref, wkr_ref, qn_ref, kvn_ref,
                     wq_ref, wqs_ref, wk_ref, wv_ref, perm_ref, exp_ref, q_ref, k_ref, v_ref):
    h = (_rms(x_ref[...], g_ref[0:1, :]) * (1.0 + mod_ref[1:2, :]) + mod_ref[0:1, :]).astype(BF16)
    cq = _rms(_dot(h, wcq_ref[...]), qn_ref[...]).astype(BF16)
    ckv = _rms(_dot(h, wckv_ref[...]), kvn_ref[...]).astype(BF16)
    kr = _dot(h, wkr_ref[...])
    cos = cos_ref[...]
    sin = sin_ref[...]
    q = _dot(cq, wq_ref[...])
    qs = _dot(cq, wqs_ref[...])
    for hd in range(MLA_HEADS):
        sl = slice(hd * LANES, (hd + 1) * LANES)
        q_ref[:, sl] = (q[:, sl] * cos + qs[:, sl] * sin).astype(BF16)
    kr_rot = (kr * cos + _dot(kr.astype(BF16), perm_ref[...]) * sin).astype(BF16)
    k_ref[...] = (_dot(ckv, wk_ref[...]) + _dot(kr_rot, exp_ref[...])).astype(BF16)
    vt = lax.dot_general(wv_ref[...], ckv, (((1,), (1,)), ((), ())), preferred_element_type=F32)
    row = lax.broadcasted_iota(jnp.int32, vt.shape, 0)
    v_ref[...] = jnp.where(row % V_ROWS == MLA_V, 1.0, vt).astype(BF16)


def _mla_proj(x, cos_t, sin_t, mod_l, g_l, wts, tm):
    bsz, seq, d = x.shape
    hw = MLA_HEADS * LANES

    def full(a):
        return pl.BlockSpec(a.shape, lambda b, i: (0,) * a.ndim)

    out_sds = jax.ShapeDtypeStruct((bsz, seq, hw), BF16)
    vt_rows = MLA_HEADS * V_ROWS
    vt_sds = jax.ShapeDtypeStruct((bsz, seq // tm, vt_rows, tm), BF16)
    return pl.pallas_call(
        _mla_proj_kernel,
        grid=(bsz, seq // tm),
        in_specs=[
            pl.BlockSpec((None, tm, d), lambda b, i: (b, i, 0)),
            pl.BlockSpec((None, tm, LANES), lambda b, i: (b, i, 0)),
            pl.BlockSpec((None, tm, LANES), lambda b, i: (b, i, 0)),
            pl.BlockSpec((None, 6, d), lambda b, i: (b, 0, 0)),
            pl.BlockSpec((4, d), lambda b, i: (0, 0)),
        ] + [full(w) for w in wts],
        out_specs=[pl.BlockSpec((None, tm, hw), lambda b, i: (b, i, 0))] * 2
        + [pl.BlockSpec((None, None, vt_rows, tm), lambda b, i: (b, i, 0, 0))],
        out_shape=[out_sds, out_sds, vt_sds],
        compiler_params=_cparams(("parallel", "parallel")),
        name="mla_proj",
    )(x, cos_t, sin_t, mod_l, g_l, *wts)


def _mla_weights(w_in, q_norm, kv_norm, w_uq, w_ukv, w_o):
    d = w_in.shape[0]
    half = MLA_ROPE // 2
    pad = LANES - MLA_QK
    w_cq = w_in[:, :MLA_Q_RANK]
    w_ckv = w_in[:, MLA_Q_RANK:MLA_Q_RANK + MLA_KV_RANK]
    w_kr = jnp.pad(w_in[:, MLA_Q_RANK + MLA_KV_RANK:], ((0, 0), (MLA_NOPE, pad)))
    scale = MLA_QK ** -0.5 * math.log2(math.e)
    wq =w_uq.reshape(MLA_Q_RANK, MLA_HEADS, MLA_QK) * scale
    nope, x1, x2 = wq[..., :MLA_NOPE], wq[..., MLA_NOPE:MLA_NOPE + half], wq[..., MLA_NOPE + half:]
    zp = jnp.zeros((MLA_Q_RANK, MLA_HEADS, pad), F32)
    wq_pad = jnp.concatenate([nope, x1, x2, zp], axis=-1).reshape(MLA_Q_RANK, MLA_HEADS * LANES)
    wq_sw = jnp.concatenate([0.0 * nope, -x2, x1, zp], axis=-1).reshape(MLA_Q_RANK, MLA_HEADS * LANES)
    wkv = w_ukv.reshape(MLA_KV_RANK, MLA_HEADS, MLA_NOPE + MLA_V)
    zk = jnp.zeros((MLA_KV_RANK, MLA_HEADS, LANES - MLA_NOPE), F32)
    wk_pad = jnp.concatenate([wkv[..., :MLA_NOPE], zk], axis=-1).reshape(MLA_KV_RANK, MLA_HEADS * LANES)
    wv_t = jnp.pad(wkv[..., MLA_NOPE:].transpose(1, 2, 0), ((0, 0), (0, V_ROWS - MLA_V), (0, 0)))
    wv_t = wv_t.reshape(MLA_HEADS * V_ROWS, MLA_KV_RANK)
    lane = jnp.arange(LANES)
    src = jnp.where((lane >= MLA_NOPE) & (lane < MLA_NOPE + half), lane + half,
                    jnp.where((lane >= MLA_NOPE + half) & (lane < MLA_QK), lane - half, -1))
    sign = jnp.where(lane < MLA_NOPE + half, -1.0, 1.0)
    perm = (lane[:, None] == src[None, :]).astype(F32) * sign[None, :]
    is_rope = (lane >= MLA_NOPE) & (lane < MLA_QK)
    expand = jnp.tile((jnp.eye(LANES, dtype=F32) * is_rope[None, :].astype(F32)), (1, MLA_HEADS))
    proj_w = (w_cq.astype(BF16), w_ckv.astype(BF16), w_kr.astype(BF16),
              q_norm.reshape(1, -1), kv_norm.reshape(1, -1),
              wq_pad.astype(BF16), wq_sw.astype(BF16), wk_pad.astype(BF16), wv_t.astype(BF16),
              perm.astype(BF16), expand.astype(BF16))
    return proj_w, w_o.astype(BF16)


def _attn_kernel(q_ref, k_ref, vt_ref, o_ref, m_scr, mt_scr, acc, s0_scr, s1_scr):
    tk = vt_ref.shape[2]
    qi = pl.program_id(2)
    s_scr = (s0_scr, s1_scr)
    m_scr[...] = jnp.full_like(m_scr, -jnp.inf)
    acc[...] = jnp.zeros_like(acc)

    def scores(hd, j):
        lanes = slice(hd * LANES, (hd + 1) * LANES)
        k = k_ref[pl.ds(pl.multiple_of(j * tk, tk), tk), lanes]
        st = lax.dot_general(k, q_ref[:, lanes], (((1,), (1,)), ((), ())), preferred_element_type=F32)
        s_scr[hd][...] = st
        mt_scr[hd] = jnp.max(st, axis=0, keepdims=True)

    def accumulate(hd, j, masked):
        st = s_scr[hd][...]
        if masked:
            key = lax.broadcasted_iota(jnp.int32, st.shape, 0)
            qry = lax.broadcasted_iota(jnp.int32, st.shape, 1)
            st = jnp.where(key <= qry, st, -jnp.inf)
            mt = jnp.max(st, axis=0, keepdims=True)
        else:
            mt = mt_scr[hd]
        m_prev = m_scr[hd]
        m_new = jnp.maximum(m_prev, mt)
        alpha = jnp.exp2(m_prev - m_new)
        pt = jnp.exp2(st - m_new).astype(BF16)
        vt = vt_ref[j, hd * V_ROWS:(hd + 1) * V_ROWS, :]
        acc[hd] = alpha * acc[hd] + _dot(vt, pt)
        m_scr[hd] = m_new

    scores(0, 0)

    def body(j, carry):
        scores(1, j)
        accumulate(0, j, False)
        scores(0, j + 1)
        accumulate(1, j, False)
        return carry

    def body_pair(jj, carry):
        body(2 * jj, carry)
        return body(2 * jj + 1, carry)

    lax.fori_loop(0, qi // 2, body_pair, 0)

    @pl.when(qi % 2 == 1)
    def _():
        body(qi - 1, 0)

    scores(1, qi)
    accumulate(0, qi, True)
    accumulate(1, qi, True)
    outs = []
    for hd in range(2):
        a = acc[hd]
        outs.append(a[:MLA_V, :] / a[MLA_V:MLA_V + 1, :])
    o_ref[...] = jnp.concatenate(outs, axis=0).T.astype(BF16)


def _attention(q, k, vt, tq):
    bsz, seq, hw = q.shape
    n_pairs = hw // (2 * LANES)
    n_kt, _, tk = vt.shape[1:]
    assert tq == tk, "the causal diagonal tile assumes square tiles"
    return pl.pallas_call(
        _attn_kernel,
        grid=(bsz, n_pairs, seq // tq),
        in_specs=[
            pl.BlockSpec((None, tq, 2 * LANES), lambda b, h, i: (b, i, h)),
            pl.BlockSpec((None, seq, 2 * LANES), lambda b, h, i: (b, 0, h)),
            pl.BlockSpec((None, n_kt, 2 * V_ROWS, tk), lambda b, h, i: (b, 0, h, 0)),
        ],
        out_specs=pl.BlockSpec((None, tq, 2 * MLA_V), lambda b, h, i: (b, i, h)),
        out_shape=jax.ShapeDtypeStruct((bsz, seq, n_pairs * 2 * MLA_V), BF16),
        scratch_shapes=[pltpu.VMEM((2, 1, tq), F32), pltpu.VMEM((2, 1, tq), F32), pltpu.VMEM((2, V_ROWS, tq), F32),
                        pltpu.VMEM((tk, tq), F32), pltpu.VMEM((tk, tq), F32)],
        compiler_params=_cparams(("parallel", "parallel", "arbitrary")),
        name="mla_attention",
    )(q, k, vt)


def _attn_out_kernel(a_ref, x_ref, w_ref, mod_ref, g_ref, o_ref):
    h = _dot(a_ref[...], w_ref[...])
    o_ref[...] = x_ref[...] + mod_ref[2:3, :] * _rms(h, g_ref[1:2, :])


def _attn_out(a, x, wo_pad, mod_l, g_l, tm):
    bsz, seq, d = x.shape
    hw = a.shape[-1]
    return pl.pallas_call(
        _attn_out_kernel,
        grid=(bsz, seq // tm),
        in_specs=[
            pl.BlockSpec((None, tm, hw), lambda b, i: (b, i, 0)),
            pl.BlockSpec((None, tm, d), lambda b, i: (b, i, 0)),
            pl.BlockSpec((hw, d), lambda b, i: (0, 0)),
            pl.BlockSpec((None, 6, d), lambda b, i: (b, 0, 0)),
            pl.BlockSpec((4, d), lambda b, i: (0, 0)),
        ],
        out_specs=pl.BlockSpec((None, tm, d), lambda b, i: (b, i, 0)),
        out_shape=jax.ShapeDtypeStruct(x.shape, F32),
        compiler_params=_cparams(("parallel", "parallel")),
        name="mla_out",
    )(a, x, wo_pad, mod_l, g_l)


def _route(logits):
    lane = lax.broadcasted_iota(jnp.int32, logits.shape, 1)
    neg = -jnp.inf
    lg = jnp.where(lane < N_EXPERTS, logits, neg)
    m1 = jnp.max(lg, axis=-1, keepdims=True)
    i1 = jnp.min(jnp.where(lg == m1, lane, LANES), axis=-1, keepdims=True)
    lg2 = jnp.where(lane == i1, neg, lg)
    m2 = jnp.max(lg2, axis=-1, keepdims=True)
    i2 = jnp.min(jnp.where(lg2 == m2, lane, LANES), axis=-1, keepdims=True)
    e = jnp.exp(m2 - m1)
    w1 = 1.0 / (1.0 + e)
    w2 = e / (1.0 + e)
    first = lane == i1
    second = lane == i2
    comb = jnp.where(first, w1, 0.0) + jnp.where(second, w2, 0.0)
    sel = jnp.where(first, 1.0, 0.0) + jnp.where(second, 1.0, 0.0)
    return comb, sel


def _moe_kernel(x_ref, wr_ref, br_ref, tri_ref, wg_ref, wu_ref, wo_ref, mod_ref, g_ref, o_ref,
                h_scr, sel_scr, rank_scr, w_scr, acc):
    e = pl.program_id(1)
    tb = x_ref.shape[0]

    @pl.when(e == 0)
    def _():
        h = _rms(x_ref[...], g_ref[2:3, :]) * (1.0 + mod_ref[4:5, :]) + mod_ref[3:4, :]
        h_hi = h.astype(BF16)
        h_lo = (h - h_hi.astype(F32)).astype(BF16)
        logits = _dot(h_hi, wr_ref[0]) + _dot(h_hi, wr_ref[1]) + _dot(h_lo, wr_ref[0]) + br_ref[...]
        comb, sel = _route(logits)
        sel_t = sel.T
        sel_scr[...] = sel_t
        w_scr[...] = comb.T
        rank_scr[...] = _dot(sel_t.astype(BF16), tri_ref[...])
        h_scr[...] = h_hi
        acc[...] = jnp.zeros_like(acc)

    sel_row = sel_scr[pl.ds(e, 1), :]
    rank_row = rank_scr[pl.ds(e, 1), :]
    w_row = w_scr[pl.ds(e, 1), :]
    n_rows = jnp.sum(sel_row).astype(jnp.int32)

    def step(r, carry):
        slot = (lax.broadcasted_iota(jnp.int32, (MOE_SUB, tb), 0) + r * MOE_SUB).astype(F32)
        hit = rank_row == slot
        onehot = jnp.where(hit, sel_row, 0.0).astype(BF16)
        xs = _dot(onehot, h_scr[...]).astype(BF16)
        g = _dot(xs, wg_ref[...])
        u = _dot(xs, wu_ref[...])
        act = (g * _sigmoid(g) * u).astype(BF16)
        y = _dot(act, wo_ref[...])
        w_col = jnp.sum(jnp.where(hit, w_row, 0.0), axis=1, keepdims=True)
        yw = (y * w_col).astype(BF16)
        acc[...] += lax.dot_general(onehot, yw, (((0,), (0,)), ((), ())), preferred_element_type=F32)
        return carry

    lax.fori_loop(0, (n_rows + MOE_SUB - 1) // MOE_SUB, step, 0)

    @pl.when(e == pl.num_programs(1) - 1)
    def _():
        o_ref[...] = x_ref[...] + mod_ref[5:6, :] * _rms(acc[...], g_ref[3:4, :])


def _moe_layer(x, w_router, b_router, w_in, w_out, mod_l, g_l):
    bsz, seq, d = x.shape
    n_e, f, _ = w_out.shape
    tb = min(MOE_TB, seq)
    blocks_per_batch = seq // tb
    wr = jnp.pad(w_router, ((0, 0), (0, LANES - n_e)))
    wr_hi = wr.astype(BF16)
    wr = jnp.stack([wr_hi, (wr - wr_hi.astype(F32)).astype(BF16)])
    br =jnp.pad(b_router, (0, LANES - n_e)).reshape(1, LANES)
    pos = jnp.arange(tb)
    tri = (pos[:, None] < pos[None, :]).astype(BF16)
    w_in = w_in.astype(BF16)
    out = pl.pallas_call(
        _moe_kernel,
        grid=(bsz * blocks_per_batch, n_e),
        in_specs=[
            pl.BlockSpec((tb, d), lambda i, e: (i, 0)),
            pl.BlockSpec((2, d, LANES), lambda i, e: (0, 0, 0)),
            pl.BlockSpec((1, LANES), lambda i, e: (0, 0)),
            pl.BlockSpec((tb, tb), lambda i, e: (0, 0)),
            pl.BlockSpec((None, d, f), lambda i, e: (e, 0, 0)),
            pl.BlockSpec((None, d, f), lambda i, e: (e, 0, 1)),
            pl.BlockSpec((None, f, d), lambda i, e: (e, 0, 0)),
            pl.BlockSpec((None, 6, d), lambda i, e: (i // blocks_per_batch, 0, 0)),
            pl.BlockSpec((4, d), lambda i, e: (0, 0)),
        ],
        out_specs=pl.BlockSpec((tb, d), lambda i, e: (i, 0)),
        out_shape=jax.ShapeDtypeStruct((bsz * seq, d), F32),
        scratch_shapes=[pltpu.VMEM((tb, d), BF16), pltpu.VMEM((LANES, tb), F32), pltpu.VMEM((LANES, tb), F32),
                        pltpu.VMEM((LANES, tb), F32), pltpu.VMEM((tb, d), F32)],
        compiler_params=_cparams(("parallel", "arbitrary")),
        name="moe",
    )(x.reshape(bsz * seq, d), wr, br, tri, w_in, w_in, w_out.astype(BF16), mod_l, g_l)
    return out.reshape(bsz, seq, d)


def kernel(x, c, positions, norm_g, w_ada, b_ada, ssm_a_re, ssm_a_im, ssm_log_dt, ssm_b_re, ssm_b_im, ssm_c_re, ssm_c_im, ssm_d, ssm_w_glu, ffn_w_in, ffn_w_out, mla_w_in, mla_q_norm, mla_kv_norm, mla_w_uq, mla_w_ukv, mla_w_o, moe_w_router, moe_b_router, moe_w_in, moe_w_out):
    depth = norm_g.shape[0]
    seq = x.shape[1]
    tm = min(512, seq)
    mod = _ada(c, w_ada, b_ada)
    cos_t = sin_t = None
    for i in range(depth):
        j = i // 2
        mod_l, g_l = mod[i], norm_g[i]
        if i % 2 == 0:
            ssm_w = _ssm_weights(ssm_a_re[j], ssm_a_im[j], ssm_log_dt[j], ssm_b_re[j], ssm_b_im[j],
                                 ssm_c_re[j], ssm_c_im[j], ssm_d[j])
            ucat = _ssm_pre(x, mod_l, g_l, tm)
            zcat = _ssm(ucat, *ssm_w)
            x = _ssm_post(zcat, x, ssm_w_glu[j].astype(BF16), mod_l, g_l, tm)
            f = ffn_w_out.shape[1]
            x = _ffn(x, ffn_w_in[j].astype(BF16), ffn_w_out[j].astype(BF16), mod_l, g_l, tm, f // 2)
        else:
            if cos_t is None:
                cos_t, sin_t = _rope_tables(positions)
            proj_w, wo_pad = _mla_weights(mla_w_in[j], mla_q_norm[j], mla_kv_norm[j],
                                          mla_w_uq[j], mla_w_ukv[j], mla_w_o[j])
            q, k, v = _mla_proj(x, cos_t, sin_t, mod_l, g_l, proj_w, tm)
            a = _attention(q, k, v, tm)
            x = _attn_out(a, x, wo_pad, mod_l, g_l, tm)
            x = _moe_layer(x, moe_w_router[j], moe_b_router[j], moe_w_in[j], moe_w_out[j], mod_l, g_l)
    return x
```

```python
import functools
import math

import jax
import jax.numpy as jnp
from jax import lax
from jax.experimental import pallas as pl
from jax.experimental.pallas import tpu as pltpu

F32 = jnp.float32
BF16 = jnp.bfloat16
HIGHEST = lax.Precision.HIGHEST

RMS_EPS = 1e-6
LANES = 128
SSM_GROUP = 16
SSM_STATE = 64
SSM_L = 16
SSM_LG = LANES // SSM_GROUP
MLA_HEADS = 16
MLA_NOPE = 64
MLA_ROPE = 32
MLA_V = 64
MLA_QK = MLA_NOPE + MLA_ROPE
V_ROWS = 80
MLA_Q_RANK = 384
MLA_KV_RANK = 256
ROPE_THETA = 10000.0
N_EXPERTS = 8
MOE_TB = 1024
MOE_SUB = 288
VMEM_LIMIT = 56 * 1024 * 1024


def _cparams(sem):
    return pltpu.CompilerParams(dimension_semantics=sem, vmem_limit_bytes=VMEM_LIMIT)


def _rms(x, g):
    return x * lax.rsqrt(jnp.mean(x * x, axis=-1, keepdims=True) + RMS_EPS) * g


def _sigmoid(x):
    return 1.0 / (1.0 + jnp.exp(-x))


def _dot(a, b):
    return jnp.dot(a, b, preferred_element_type=F32)


def _ada_kernel(c_ref, w_ref, b_ref, o_ref):
    c = c_ref[...]
    c_act = c * _sigmoid(c)
    o_ref[...] = jnp.dot(c_act, w_ref[...], preferred_element_type=F32, precision=HIGHEST) + b_ref[...]


def _ada(c, w_ada, b_ada):
    depth, d, _ = w_ada.shape
    bsz = c.shape[0]
    rows = 8
    c_pad = jnp.pad(c, ((0, rows - bsz), (0, 0)))
    out = pl.pallas_call(
        _ada_kernel,
        grid=(depth, 6),
        in_specs=[
            pl.BlockSpec((rows, d), lambda i, j: (0, 0)),
            pl.BlockSpec((None, d, d), lambda i, j: (i, 0, j)),
            pl.BlockSpec((None, None, 1, d), lambda i, j: (i, j, 0, 0)),
        ],
        out_specs=pl.BlockSpec((None, None, rows, d), lambda i, j: (i, j, 0, 0)),
        out_shape=jax.ShapeDtypeStruct((depth, 6, rows, d), F32),
        compiler_params=_cparams(("arbitrary", "arbitrary")),
        name="ada_mod",
    )(c_pad, w_ada, b_ada.reshape(depth, 6, 1, d))
    return out[:, :, :bsz, :].transpose(0, 2, 1, 3)


def _ssm_pre_kernel(x_ref, mod_ref, g_ref, o_ref, scr):
    tm = x_ref.shape[0]
    n = tm // SSM_L
    u = _rms(x_ref[...], g_ref[0:1, :]) * (1.0 + mod_ref[1:2, :]) + mod_ref[0:1, :]
    n_lg = o_ref.shape[0]
    for lg in range(n_lg):
        scr[lg] = u[:, lg * LANES:(lg + 1) * LANES]
    for s in range(SSM_L):
        for lg in range(n_lg):
            o_ref[lg, :, s * LANES:(s + 1) * LANES] = scr[lg, pl.ds(s, n, stride=SSM_L), :].astype(BF16)


def _ssm_pre(x, mod_l, g_l, tm):
    bsz, seq, d = x.shape
    n_lg = d // LANES
    return pl.pallas_call(
        _ssm_pre_kernel,
        grid=(bsz, seq // tm),
        in_specs=[
            pl.BlockSpec((None, tm, d), lambda b, i: (b, i, 0)),
            pl.BlockSpec((None, 6, d), lambda b, i: (b, 0, 0)),
            pl.BlockSpec((4, d), lambda b, i: (0, 0)),
        ],
        out_specs=pl.BlockSpec((None, n_lg, tm // SSM_L, SSM_L * LANES), lambda b, i: (b, 0, i, 0)),
        out_shape=jax.ShapeDtypeStruct((bsz, n_lg, seq // SSM_L, SSM_L * LANES), BF16),
        scratch_shapes=[pltpu.VMEM((n_lg, tm, LANES), F32)],
        compiler_params=_cparams(("parallel", "parallel")),
        name="ssm_pre",
    )(x, mod_l, g_l)


def _gelu_tanh(y):
    return 0.5 * y * (1.0 + jnp.tanh(math.sqrt(2.0 / math.pi) * (y + 0.044715 * (y * y * y))))


def _same_group(shape, row_shift, col_shift):
    row_g = (lax.broadcasted_iota(jnp.int32, shape, 0) >> row_shift) & (SSM_LG - 1)
    col_g = (lax.broadcasted_iota(jnp.int32, shape, 1) >> col_shift) & (SSM_LG - 1)
    return row_g == col_g


def _ssm_expand_weights(xw_ref, yw_ref, cc_ref, t1_ref, t1t_ref, win_s, wout_s, m8_s, mx_s):
    hl = SSM_L // 2
    in_mask = _same_group((LANES, t1_ref.shape[1]), 4, 6)
    out_mask = _same_group((t1_ref.shape[1], LANES), 6, 4)
    k_mask = _same_group((LANES, LANES), 4, 4)
    m8_s[...] = jnp.zeros_like(m8_s)
    for s in range(SSM_L):
        rows = slice(s * LANES, (s + 1) * LANES)
        val = lax.dot_general(xw_ref[s], t1_ref[...], (((0,), (0,)), ((), ())), preferred_element_type=F32)
        win_s[rows, :] = jnp.where(in_mask, val, 0.0).astype(BF16)
        val = lax.dot_general(t1t_ref[...], yw_ref[s], (((1,), (1,)), ((), ())), preferred_element_type=F32)
        wout_s[:, rows] = jnp.where(out_mask, val, 0.0).astype(BF16)
    for tau in range(SSM_L):
        k_tau = lax.dot_general(xw_ref[SSM_L - 1 - tau], cc_ref[...], (((0,), (1,)), ((), ())),
                                preferred_element_type=F32)
        kbd = jnp.where(k_mask, k_tau, 0.0).astype(BF16)
        for a in range(hl):
            if a + tau < hl:
                m8_s[a * LANES:(a + 1) * LANES, (a + tau) * LANES:(a + tau + 1) * LANES] = kbd
            b = a + tau - hl
            if 0 <= b < hl:
                mx_s[a * LANES:(a + 1) * LANES, b * LANES:(b + 1) * LANES] = kbd


def _ssm_kernel(u_ref, xw_ref, yw_ref, cc_ref, t1_ref, t1t_ref, a_ref, d_ref, o_ref,
                win_s, wout_s, m8_s, mx_s, v_scr, h_scr):
    n_chunks = u_ref.shape[0]
    half = (SSM_L // 2) * LANES
    ns = a_ref.shape[1]

    @pl.when(pl.program_id(1) == 0)
    def _():
        _ssm_expand_weights(xw_ref, yw_ref, cc_ref, t1_ref, t1t_ref, win_s, wout_s, m8_s, mx_s)

    uc = u_ref[...]
    v_scr[...] = _dot(uc, win_s[...])
    ar = a_ref[0:1, :]
    ai = a_ref[1:2, :]

    def step(c, carry):
        hr, hi = carry
        h_scr[pl.ds(c, 1), pl.ds(0, ns)] = hr
        h_scr[pl.ds(c, 1), pl.ds(ns, ns)] = hi
        vr = v_scr[pl.ds(c, 1), pl.ds(0, ns)]
        vi = v_scr[pl.ds(c, 1), pl.ds(ns, ns)]
        return ar * hr - ai * hi + vr, ar * hi + ai * hr + vi

    zero = jnp.zeros((1, ns), F32)
    lax.fori_loop(0, n_chunks, step, (zero, zero))
    y = _dot(h_scr[...].astype(BF16), wout_s[...])
    u_lo = uc[:, :half]
    u_hi = uc[:, half:]
    m8 = m8_s[...]
    y_lo = y[:, :half] + _dot(u_lo, m8)
    y_hi = y[:, half:] + _dot(u_hi, m8) + _dot(u_lo, mx_s[...])
    dd = d_ref[...]
    o_ref[:, :half] = _gelu_tanh(y_lo + dd[:, :half] * u_lo.astype(F32)).astype(BF16)
    o_ref[:, half:] = _gelu_tanh(y_hi + dd[:, half:] * u_hi.astype(F32)).astype(BF16)


def _ssm(ucat, xw, yw, cc, t1, t1t, a16, dcat):
    bsz, n_lg, n_chunks, width = ucat.shape
    ns2 = t1.shape[-1]
    half = width // 2

    def per_lg(a):
        return pl.BlockSpec((None,) + a.shape[1:], lambda g, b: (g,) + (0,) * (a.ndim - 1))

    def const(a):
        return pl.BlockSpec(a.shape, lambda g, b: (0, 0))

    return pl.pallas_call(
        _ssm_kernel,
        grid=(n_lg, bsz),
        in_specs=[
            pl.BlockSpec((None, None, n_chunks, width), lambda g, b: (b, g, 0, 0)),
            per_lg(xw), per_lg(yw), per_lg(cc), const(t1), const(t1t), per_lg(a16), per_lg(dcat),
        ],
        out_specs=pl.BlockSpec((None, None, n_chunks, width), lambda g, b: (b, g, 0, 0)),
        out_shape=jax.ShapeDtypeStruct(ucat.shape, BF16),
        scratch_shapes=[
            pltpu.VMEM((width, ns2), BF16), pltpu.VMEM((ns2, width), BF16),
            pltpu.VMEM((half, half), BF16), pltpu.VMEM((half, half), BF16),
            pltpu.VMEM((n_chunks, ns2), F32), pltpu.VMEM((n_chunks, ns2), F32),
        ],
        compiler_params=_cparams(("arbitrary", "arbitrary")),
        name="ssm_scan",
    )(ucat, xw, yw, cc, t1, t1t, a16, dcat)


def _ssm_post_kernel(z_ref, x_ref, w_ref, mod_ref, g_ref, o_ref, zp, rp):
    n_lg, n, _ = z_ref.shape
    d = x_ref.shape[1]
    for s in range(SSM_L):
        for lg in range(n_lg):
            zp[s * n:(s + 1) * n, lg * LANES:(lg + 1) * LANES] = z_ref[lg, :, s * LANES:(s + 1) * LANES]
    gl = _dot(zp[...], w_ref[...])
    h = gl[:, :d] * _sigmoid(gl[:, d:])
    r = mod_ref[2:3, :] * _rms(h, g_ref[1:2, :])
    for lg in range(n_lg):
        rp[lg] = r[:, lg * LANES:(lg + 1) * LANES]
    sub = 8
    for c in range(n):
        for k in range(SSM_L // sub):
            rows = pl.ds(c * SSM_L + k * sub, sub)
            for lg in range(n_lg):
                cols = slice(lg * LANES, (lg + 1) * LANES)
                o_ref[rows, cols] = x_ref[rows, cols] + rp[lg, pl.ds(k * sub * n + c, sub, stride=n), :]


def _ssm_post(zcat, x, w_glu, mod_l, g_l, tm):
    bsz, seq, d = x.shape
    n_lg = d // LANES
    return pl.pallas_call(
        _ssm_post_kernel,
        grid=(bsz, seq // tm),
        in_specs=[
            pl.BlockSpec((None, n_lg, tm // SSM_L, SSM_L * LANES), lambda b, i: (b, 0, i, 0)),
            pl.BlockSpec((None, tm, d), lambda b, i: (b, i, 0)),
            pl.BlockSpec((d, 2 * d), lambda b, i: (0, 0)),
            pl.BlockSpec((None, 6, d), lambda b, i: (b, 0, 0)),
            pl.BlockSpec((4, d), lambda b, i: (0, 0)),
        ],
        out_specs=pl.BlockSpec((None, tm, d), lambda b, i: (b, i, 0)),
        out_shape=jax.ShapeDtypeStruct(x.shape, F32),
        scratch_shapes=[pltpu.VMEM((tm, d), BF16), pltpu.VMEM((n_lg, tm, LANES), F32)],
        compiler_params=_cparams(("parallel", "parallel")),
        name="ssm_post",
    )(zcat, x, w_glu, mod_l, g_l)


def _ssm_weights(a_re, a_im, log_dt, b_re, b_im, c_re, c_im, d_skip):
    n_groups = a_re.shape[0]
    n_lg = n_groups // SSM_LG
    p, hh, ll = SSM_STATE, SSM_GROUP, SSM_L
    dt = jnp.exp(log_dt)[:, None]
    k = jnp.arange(ll + 1, dtype=F32)[:, None, None]
    mag = jnp.exp(k * (a_re * dt))
    pw_re = mag * jnp.cos(k * (a_im * dt))
    pw_im = mag * jnp.sin(k * (a_im * dt))
    den = a_re * a_re + a_im * a_im
    num_re, num_im = pw_re[1] - 1.0, pw_im[1]
    f_re = (num_re * a_re + num_im * a_im) / den
    f_im = (num_im * a_re - num_re * a_im) / den
    bb_re = f_re[..., None] * b_re - f_im[..., None] * b_im
    bb_im = f_re[..., None] * b_im + f_im[..., None] * b_re
    b2_re, b2_im = (x.reshape(n_lg, SSM_LG, p, hh).transpose(0, 2, 1, 3).reshape(n_lg, p, LANES)
                    for x in (bb_re, bb_im))

    def lanes_g(x):
        x = x.reshape(x.shape[0], n_lg, SSM_LG, p).transpose(0, 1, 3, 2)
        return jnp.repeat(x, hh, axis=-1)

    l_re, l_im = lanes_g(pw_re[:ll][::-1]), lanes_g(pw_im[:ll][::-1])
    xw = jnp.concatenate([l_re * b2_re - l_im * b2_im, l_re * b2_im + l_im * b2_re], axis=2)
    xw = xw.transpose(1, 0, 2, 3)
    cp_re = c_re[None] * pw_re[1:, :, None, :] - c_im[None] * pw_im[1:, :, None, :]
    cp_im = c_re[None] * pw_im[1:, :, None, :] + c_im[None] * pw_re[1:, :, None, :]
    yw = jnp.concatenate([cp_re, -cp_im], axis=-1).reshape(ll, n_lg, LANES, 2 * p).transpose(1, 0, 2, 3)
    cc = jnp.concatenate([c_re, -c_im], axis=-1).reshape(n_lg, LANES, 2 * p)
    j1 = jnp.arange(2 * SSM_LG * p)
    t1 = (jnp.arange(2 * p)[:, None] == ((j1 // (SSM_LG * p)) * p + j1 % p)[None, :]).astype(BF16)
    a16 = jnp.stack([pw_re[ll], pw_im[ll]], axis=0).reshape(2, n_lg, SSM_LG * p).transpose(1, 0, 2)
    dcat = jnp.tile(d_skip.reshape(n_lg, 1, LANES), (1, 1, ll))
    return xw.astype(BF16), yw.astype(BF16), cc.astype(BF16), t1, t1.T, a16, dcat


def _ffn_kernel(x_ref, wg_ref, wu_ref, wo_ref, mod_ref, g_ref, o_ref, h_scr, acc):
    k = pl.program_id(2)

    @pl.when(k == 0)
    def _():
        h = _rms(x_ref[...], g_ref[2:3, :]) * (1.0 + mod_ref[4:5, :]) + mod_ref[3:4, :]
        h_scr[...] = h.astype(BF16)
        acc[...] = jnp.zeros_like(acc)

    h = h_scr[...]
    g = _dot(h, wg_ref[...])
    u = _dot(h, wu_ref[...])
    act = (g * _sigmoid(g) * u).astype(BF16)
    acc[...] += _dot(act, wo_ref[...])

    @pl.when(k == pl.num_programs(2) - 1)
    def _():
        o_ref[...] = x_ref[...] + mod_ref[5:6, :] * _rms(acc[...], g_ref[3:4, :])


def _ffn(x, w_in, w_out, mod_l, g_l, tm, tf):
    bsz, seq, d = x.shape
    f = w_out.shape[0]
    nk = f // tf
    return pl.pallas_call(
        _ffn_kernel,
        grid=(bsz, seq // tm, nk),
        in_specs=[
            pl.BlockSpec((None, tm, d), lambda b, i, k: (b, i, 0)),
            pl.BlockSpec((d, tf), lambda b, i, k: (0, k)),
            pl.BlockSpec((d, tf), lambda b, i, k: (0, k + nk)),
            pl.BlockSpec((tf, d), lambda b, i, k: (k, 0)),
            pl.BlockSpec((None, 6, d), lambda b, i, k: (b, 0, 0)),
            pl.BlockSpec((4, d), lambda b, i, k: (0, 0)),
        ],
        out_specs=pl.BlockSpec((None, tm, d), lambda b, i, k: (b, i, 0)),
        out_shape=jax.ShapeDtypeStruct(x.shape, F32),
        scratch_shapes=[pltpu.VMEM((tm, d), BF16), pltpu.VMEM((tm, d), F32)],
        compiler_params=_cparams(("parallel", "parallel", "arbitrary")),
        name="ffn",
    )(x, w_in, w_in, w_out, mod_l, g_l)


def _rope_kernel(pos_ref, f_ref, cos_ref, sin_ref):
    ang = pos_ref[...].astype(F32) * f_ref[...]
    cos_ref[...] = jnp.cos(ang)
    sin_ref[...] = jnp.sin(ang)


def _rope_tables(positions):
    bsz, seq = positions.shape
    half = MLA_ROPE // 2
    inv_freq = ROPE_THETA ** (-jnp.arange(half, dtype=F32) / half)
    per_row = LANES // half
    rows = bsz * seq // per_row
    pos_rep = jnp.broadcast_to(positions[..., None], (bsz, seq, half)).reshape(rows, LANES)
    f_rep = jnp.tile(inv_freq, per_row).reshape(1, LANES)
    tr = min(rows, 512)
    cos, sin = pl.pallas_call(
        _rope_kernel,
        grid=(rows // tr,),
        in_specs=[pl.BlockSpec((tr, LANES), lambda i: (i, 0)), pl.BlockSpec((1, LANES), lambda i: (0, 0))],
        out_specs=[pl.BlockSpec((tr, LANES), lambda i: (i, 0))] * 2,
        out_shape=[jax.ShapeDtypeStruct((rows, LANES), F32)] * 2,
        compiler_params=_cparams(("parallel",)),
        name="rope_tables",
    )(pos_rep, f_rep)
    cos = cos.reshape(bsz, seq, half)
    sin = sin.reshape(bsz, seq, half)
    ones = jnp.ones((bsz, seq, MLA_NOPE), F32)
    zpad = jnp.zeros((bsz, seq, LANES - MLA_QK), F32)
    z16 = jnp.zeros((bsz, seq, half), F32)
    cos_t = jnp.concatenate([ones, cos, cos, zpad], axis=-1)
    sin_lo = jnp.concatenate([0.0 * ones, -sin, z16, zpad], axis=-1)
    sin_hi = jnp.concatenate([0.0 * ones, z16, sin, zpad], axis=-1)
    return cos_t, sin_lo, sin_hi


def _mla_proj_kernel(x_ref, cos_ref, sin_lo_ref, sin_hi_ref, mod_ref, g_ref, wcq_ref, wckv_ref, wkr_ref,
                     qn_ref, kvn_ref, wq_ref, wk_ref, wv_ref, exp_ref, q_ref, k_ref, v_ref):
    h = (_rms(x_ref[...], g_ref[0:1, :]) * (1.0 + mod_ref[1:2, :]) + mod_ref[0:1, :]).astype(BF16)
    cq = _rms(_dot(h, wcq_ref[...]), qn_ref[...]).astype(BF16)
    ckv = _rms(_dot(h, wckv_ref[...]), kvn_ref[...]).astype(BF16)
    kr = _dot(h, wkr_ref[...])
    cos = cos_ref[...]
    sin_lo = sin_lo_ref[...]
    sin_hi = sin_hi_ref[...]
    half = MLA_ROPE // 2

    def rope(t):
        width = t.shape[1]
        up = pltpu.roll(t, half, axis=1)
        down = pltpu.roll(t, width - half, axis=1)
        return [t[:, s:s + LANES] * cos + down[:, s:s + LANES] * sin_lo + up[:, s:s + LANES] * sin_hi
                for s in range(0, width, LANES)]

    q = _dot(cq, wq_ref[...])
    for hd, q_hd in enumerate(rope(q)):
        q_ref[:, hd * LANES:(hd + 1) * LANES] = q_hd.astype(BF16)
    kr_rot = rope(kr)[0].astype(BF16)
    k_ref[...] = (_dot(ckv, wk_ref[...]) + _dot(kr_rot, exp_ref[...])).astype(BF16)
    vt = lax.dot_general(wv_ref[...], ckv, (((1,), (1,)), ((), ())), preferred_element_type=F32)
    row = lax.broadcasted_iota(jnp.int32, vt.shape, 0)
    v_ref[...] = jnp.where(row % V_ROWS == MLA_V, 1.0, vt).astype(BF16)


def _mla_proj(x, rope_t, mod_l, g_l, wts, tm):
    bsz, seq, d = x.shape
    hw = MLA_HEADS * LANES

    def full(a):
        return pl.BlockSpec(a.shape, lambda b, i: (0,) * a.ndim)

    out_sds = jax.ShapeDtypeStruct((bsz, seq, hw), BF16)
    vt_rows = MLA_HEADS * V_ROWS
    vt_sds = jax.ShapeDtypeStruct((bsz, seq // tm, vt_rows, tm), BF16)
    return pl.pallas_call(
        _mla_proj_kernel,
        grid=(bsz, seq // tm),
        in_specs=[
            pl.BlockSpec((None, tm, d), lambda b, i: (b, i, 0)),
            pl.BlockSpec((None, tm, LANES), lambda b, i: (b, i, 0)),
            pl.BlockSpec((None, tm, LANES), lambda b, i: (b, i, 0)),
            pl.BlockSpec((None, tm, LANES), lambda b, i: (b, i, 0)),
            pl.BlockSpec((None, 6, d), lambda b, i: (b, 0, 0)),
            pl.BlockSpec((4, d), lambda b, i: (0, 0)),
        ] + [full(w) for w in wts],
        out_specs=[pl.BlockSpec((None, tm, hw), lambda b, i: (b, i, 0))] * 2
        + [pl.BlockSpec((None, None, vt_rows, tm), lambda b, i: (b, i, 0, 0))],
        out_shape=[out_sds, out_sds, vt_sds],
        compiler_params=_cparams(("parallel", "parallel")),
        name="mla_proj",
    )(x, *rope_t, mod_l, g_l, *wts)


def _mla_weights(w_in, q_norm, kv_norm, w_uq, w_ukv, w_o):
    d = w_in.shape[0]
    half = MLA_ROPE // 2
    pad = LANES - MLA_QK
    w_cq = w_in[:, :MLA_Q_RANK]
    w_ckv = w_in[:, MLA_Q_RANK:MLA_Q_RANK + MLA_KV_RANK]
    w_kr = jnp.pad(w_in[:, MLA_Q_RANK + MLA_KV_RANK:], ((0, 0), (MLA_NOPE, pad)))
    scale = MLA_QK ** -0.5 * math.log2(math.e)
    wq = w_uq.reshape(MLA_Q_RANK, MLA_HEADS, MLA_QK) * scale
    wq_pad = jnp.pad(wq, ((0, 0), (0, 0), (0, pad))).reshape(MLA_Q_RANK, MLA_HEADS * LANES)
    wkv =w_ukv.reshape(MLA_KV_RANK, MLA_HEADS, MLA_NOPE + MLA_V)
    zk = jnp.zeros((MLA_KV_RANK, MLA_HEADS, LANES - MLA_NOPE), F32)
    wk_pad = jnp.concatenate([wkv[..., :MLA_NOPE], zk], axis=-1).reshape(MLA_KV_RANK, MLA_HEADS * LANES)
    wv_t = jnp.pad(wkv[..., MLA_NOPE:].transpose(1, 2, 0), ((0, 0), (0, V_ROWS - MLA_V), (0, 0)))
    wv_t = wv_t.reshape(MLA_HEADS * V_ROWS, MLA_KV_RANK)
    lane = jnp.arange(LANES)
    is_rope = (lane >= MLA_NOPE) & (lane < MLA_QK)
    expand = jnp.tile((jnp.eye(LANES, dtype=F32) * is_rope[None, :].astype(F32)), (1, MLA_HEADS))
    proj_w = (w_cq.astype(BF16), w_ckv.astype(BF16), w_kr.astype(BF16),
              q_norm.reshape(1, -1), kv_norm.reshape(1, -1),
              wq_pad.astype(BF16), wk_pad.astype(BF16), wv_t.astype(BF16), expand.astype(BF16))
    return proj_w, w_o.astype(BF16)


def _attn_kernel(q_ref, k_ref, vt_ref, o_ref, m_scr, mt_scr, acc, s0_scr, s1_scr, qt_scr):
    tk = vt_ref.shape[2]
    qi = pl.program_id(2)
    s_scr = (s0_scr, s1_scr)
    qt_scr[...] = q_ref[...].astype(F32).T.astype(BF16)
    m_scr[...] = jnp.full_like(m_scr, -jnp.inf)
    acc[...] = jnp.zeros_like(acc)

    def scores(hd, j):
        lanes = slice(hd * LANES, (hd + 1) * LANES)
        k = k_ref[pl.ds(pl.multiple_of(j * tk, tk), tk), lanes]
        st = _dot(k, qt_scr[lanes, :])
        s_scr[hd][...] = st
        mt_scr[hd] = jnp.max(st, axis=0, keepdims=True)

    def accumulate(hd, j, masked):
        st = s_scr[hd][...]
        if masked:
            key = lax.broadcasted_iota(jnp.int32, st.shape, 0)
            qry = lax.broadcasted_iota(jnp.int32, st.shape, 1)
            st = jnp.where(key <= qry, st, -jnp.inf)
            mt = jnp.max(st, axis=0, keepdims=True)
        else:
            mt = mt_scr[hd]
        m_prev = m_scr[hd]
        m_new = jnp.maximum(m_prev, mt)
        alpha = jnp.exp2(m_prev - m_new)
        pt = jnp.exp2(st - m_new).astype(BF16)
        vt = vt_ref[j, hd * V_ROWS:(hd + 1) * V_ROWS, :]
        acc[hd] = alpha * acc[hd] + _dot(vt, pt)
        m_scr[hd] = m_new

    scores(0, 0)

    def body(j, carry):
        scores(1, j)
        accumulate(0, j, False)
        scores(0, j + 1)
        accumulate(1, j, False)
        return carry

    def body_pair(jj, carry):
        body(2 * jj, carry)
        return body(2 * jj + 1, carry)

    lax.fori_loop(0, qi // 2, body_pair, 0)

    @pl.when(qi % 2 == 1)
    def _():
        body(qi - 1, 0)

    scores(1, qi)
    accumulate(0, qi, True)
    accumulate(1, qi, True)
    outs = []
    for hd in range(2):
        a = acc[hd]
        outs.append(a[:MLA_V, :] / a[MLA_V:MLA_V + 1, :])
    o_ref[...] = jnp.concatenate(outs, axis=0).T.astype(BF16)


def _attention(q, k, vt, tq):
    bsz, seq, hw = q.shape
    n_pairs = hw // (2 * LANES)
    n_kt, _, tk = vt.shape[1:]
    assert tq == tk, "the causal diagonal tile assumes square tiles"
    return pl.pallas_call(
        _attn_kernel,
        grid=(bsz, n_pairs, seq // tq),
        in_specs=[
            pl.BlockSpec((None, tq, 2 * LANES), lambda b, h, i: (b, i, h)),
            pl.BlockSpec((None, seq, 2 * LANES), lambda b, h, i: (b, 0, h)),
            pl.BlockSpec((None, n_kt, 2 * V_ROWS, tk), lambda b, h, i: (b, 0, h, 0)),
        ],
        out_specs=pl.BlockSpec((None, tq, 2 * MLA_V), lambda b, h, i: (b, i, h)),
        out_shape=jax.ShapeDtypeStruct((bsz, seq, n_pairs * 2 * MLA_V), BF16),
        scratch_shapes=[pltpu.VMEM((2, 1, tq), F32), pltpu.VMEM((2, 1, tq), F32), pltpu.VMEM((2, V_ROWS, tq), F32),
                        pltpu.VMEM((tk, tq), F32), pltpu.VMEM((tk, tq), F32), pltpu.VMEM((2 * LANES, tq), BF16)],
        compiler_params=_cparams(("parallel", "parallel", "arbitrary")),
        name="mla_attention",
    )(q, k, vt)


def _attn_out_kernel(a_ref, x_ref, w_ref, mod_ref, g_ref, o_ref):
    h = _dot(a_ref[...], w_ref[...])
    o_ref[...] = x_ref[...] + mod_ref[2:3, :] * _rms(h, g_ref[1:2, :])


def _attn_out(a, x, wo_pad, mod_l, g_l, tm):
    bsz, seq, d = x.shape
    hw = a.shape[-1]
    return pl.pallas_call(
        _attn_out_kernel,
        grid=(bsz, seq // tm),
        in_specs=[
            pl.BlockSpec((None, tm, hw), lambda b, i: (b, i, 0)),
            pl.BlockSpec((None, tm, d), lambda b, i: (b, i, 0)),
            pl.BlockSpec((hw, d), lambda b, i: (0, 0)),
            pl.BlockSpec((None, 6, d), lambda b, i: (b, 0, 0)),
            pl.BlockSpec((4, d), lambda b, i: (0, 0)),
        ],
        out_specs=pl.BlockSpec((None, tm, d), lambda b, i: (b, i, 0)),
        out_shape=jax.ShapeDtypeStruct(x.shape, F32),
        compiler_params=_cparams(("parallel", "parallel")),
        name="mla_out",
    )(a, x, wo_pad, mod_l, g_l)


def _route(logits):
    lane = lax.broadcasted_iota(jnp.int32, logits.shape, 1)
    neg = -jnp.inf
    lg = jnp.where(lane < N_EXPERTS, logits, neg)
    m1 = jnp.max(lg, axis=-1, keepdims=True)
    i1 = jnp.min(jnp.where(lg == m1, lane, LANES), axis=-1, keepdims=True)
    lg2 = jnp.where(lane == i1, neg, lg)
    m2 = jnp.max(lg2, axis=-1, keepdims=True)
    i2 = jnp.min(jnp.where(lg2 == m2, lane, LANES), axis=-1, keepdims=True)
    e = jnp.exp(m2 - m1)
    w1 = 1.0 / (1.0 + e)
    w2 = e / (1.0 + e)
    first = lane == i1
    second = lane == i2
    comb = jnp.where(first, w1, 0.0) + jnp.where(second, w2, 0.0)
    sel = jnp.where(first, 1.0, 0.0) + jnp.where(second, 1.0, 0.0)
    return comb, sel


def _moe_kernel(x_ref, wr_ref, br_ref, tri_ref, wg_ref, wu_ref, wo_ref, mod_ref, g_ref, o_ref,
                h_scr, sel_scr, rank_scr, w_scr, acc):
    e = pl.program_id(1)
    tb = x_ref.shape[0]

    @pl.when(e == 0)
    def _():
        h = _rms(x_ref[...], g_ref[2:3, :]) * (1.0 + mod_ref[4:5, :]) + mod_ref[3:4, :]
        h_hi = h.astype(BF16)
        h_lo = (h - h_hi.astype(F32)).astype(BF16)
        logits = _dot(h_hi, wr_ref[0]) + _dot(h_hi, wr_ref[1]) + _dot(h_lo, wr_ref[0]) + br_ref[...]
        comb, sel = _route(logits)
        sel_t = sel.T
        sel_scr[...] = sel_t
        w_scr[...] = comb.T
        rank_scr[...] = _dot(sel_t.astype(BF16), tri_ref[...])
        h_scr[...] = h_hi
        acc[...] = jnp.zeros_like(acc)

    sel_row = sel_scr[pl.ds(e, 1), :]
    rank_row = rank_scr[pl.ds(e, 1), :]
    w_row = w_scr[pl.ds(e, 1), :]
    n_rows = jnp.sum(sel_row).astype(jnp.int32)

    def step(r, carry):
        slot = (lax.broadcasted_iota(jnp.int32, (MOE_SUB, tb), 0) + r * MOE_SUB).astype(F32)
        hit = rank_row == slot
        onehot = jnp.where(hit, sel_row, 0.0).astype(BF16)
        xs = _dot(onehot, h_scr[...]).astype(BF16)
        g = _dot(xs, wg_ref[...])
        u = _dot(xs, wu_ref[...])
        act = (g * _sigmoid(g) * u).astype(BF16)
        y = _dot(act, wo_ref[...])
        w_col = jnp.sum(jnp.where(hit, w_row, 0.0), axis=1, keepdims=True)
        yw = (y * w_col).astype(BF16)
        acc[...] += lax.dot_general(onehot, yw, (((0,), (0,)), ((), ())), preferred_element_type=F32)
        return carry

    lax.fori_loop(0, (n_rows + MOE_SUB - 1) // MOE_SUB, step, 0)

    @pl.when(e == pl.num_programs(1) - 1)
    def _():
        o_ref[...] = x_ref[...] + mod_ref[5:6, :] * _rms(acc[...], g_ref[3:4, :])


def _moe_layer(x, w_router, b_router, w_in, w_out, mod_l, g_l):
    bsz, seq, d = x.shape
    n_e, f, _ = w_out.shape
    tb = min(MOE_TB, seq)
    blocks_per_batch = seq // tb
    wr = jnp.pad(w_router, ((0, 0), (0, LANES - n_e)))
    wr_hi = wr.astype(BF16)
    wr = jnp.stack([wr_hi, (wr - wr_hi.astype(F32)).astype(BF16)])
    br =jnp.pad(b_router, (0, LANES - n_e)).reshape(1, LANES)
    pos = jnp.arange(tb)
    tri = (pos[:, None] < pos[None, :]).astype(BF16)
    w_in = w_in.astype(BF16)
    out = pl.pallas_call(
        _moe_kernel,
        grid=(bsz * blocks_per_batch, n_e),
        in_specs=[
            pl.BlockSpec((tb, d), lambda i, e: (i, 0)),
            pl.BlockSpec((2, d, LANES), lambda i, e: (0, 0, 0)),
            pl.BlockSpec((1, LANES), lambda i, e: (0, 0)),
            pl.BlockSpec((tb, tb), lambda i, e: (0, 0)),
            pl.BlockSpec((None, d, f), lambda i, e: (e, 0, 0)),
            pl.BlockSpec((None, d, f), lambda i, e: (e, 0, 1)),
            pl.BlockSpec((None, f, d), lambda i, e: (e, 0, 0)),
            pl.BlockSpec((None, 6, d), lambda i, e: (i // blocks_per_batch, 0, 0)),
            pl.BlockSpec((4, d), lambda i, e: (0, 0)),
        ],
        out_specs=pl.BlockSpec((tb, d), lambda i, e: (i, 0)),
        out_shape=jax.ShapeDtypeStruct((bsz * seq, d), F32),
        scratch_shapes=[pltpu.VMEM((tb, d), BF16), pltpu.VMEM((LANES, tb), F32), pltpu.VMEM((LANES, tb), F32),
                        pltpu.VMEM((LANES, tb), F32), pltpu.VMEM((tb, d), F32)],
        compiler_params=_cparams(("parallel", "arbitrary")),
        name="moe",
    )(x.reshape(bsz * seq, d), wr, br, tri, w_in, w_in, w_out.astype(BF16), mod_l, g_l)
    return out.reshape(bsz, seq, d)


def kernel(x, c, positions, norm_g, w_ada, b_ada, ssm_a_re, ssm_a_im, ssm_log_dt, ssm_b_re, ssm_b_im, ssm_c_re, ssm_c_im, ssm_d, ssm_w_glu, ffn_w_in, ffn_w_out, mla_w_in, mla_q_norm, mla_kv_norm, mla_w_uq, mla_w_ukv, mla_w_o, moe_w_router, moe_b_router, moe_w_in, moe_w_out):
    depth = norm_g.shape[0]
    seq = x.shape[1]
    tm = min(512, seq)
    mod = _ada(c, w_ada, b_ada)
    rope_t = None
    for i in range(depth):
        j = i // 2
        mod_l, g_l = mod[i], norm_g[i]
        if i % 2 == 0:
            ssm_w = _ssm_weights(ssm_a_re[j], ssm_a_im[j], ssm_log_dt[j], ssm_b_re[j], ssm_b_im[j],
                                 ssm_c_re[j], ssm_c_im[j], ssm_d[j])
            ucat = _ssm_pre(x, mod_l, g_l, tm)
            zcat = _ssm(ucat, *ssm_w)
            x = _ssm_post(zcat, x, ssm_w_glu[j].astype(BF16), mod_l, g_l, tm)
            f = ffn_w_out.shape[1]
            x = _ffn(x, ffn_w_in[j].astype(BF16), ffn_w_out[j].astype(BF16), mod_l, g_l, tm, f // 2)
        else:
            if rope_t is None:
                rope_t = _rope_tables(positions)
            proj_w, wo_pad = _mla_weights(mla_w_in[j], mla_q_norm[j], mla_kv_norm[j],
                                          mla_w_uq[j], mla_w_ukv[j], mla_w_o[j])
            q, k, v = _mla_proj(x, rope_t, mod_l, g_l, proj_w, tm)
            a = _attention(q, k, v, tm)
            x = _attn_out(a, x, wo_pad, mod_l, g_l, tm)
            x = _moe_layer(x, moe_w_router[j], moe_b_router[j], moe_w_in[j], moe_w_out[j], mod_l, g_l)
    return x
```

```python
import functools
import math

import jax
import jax.numpy as jnp
from jax import lax
from jax.experimental import pallas as pl
from jax.experimental.pallas import tpu as pltpu

F32 = jnp.float32
BF16 = jnp.bfloat16
HIGHEST = lax.Precision.HIGHEST

RMS_EPS = 1e-6
LANES = 128
SSM_GROUP = 16
SSM_STATE = 64
SSM_L = 16
SSM_LG = LANES // SSM_GROUP
STRIDE_PAD = 4
MLA_HEADS = 16
MLA_NOPE = 64
MLA_ROPE = 32
MLA_V = 64
MLA_QK = MLA_NOPE + MLA_ROPE
V_ROWS = 80
MLA_Q_RANK = 384
MLA_KV_RANK = 256
ROPE_THETA = 10000.0
N_EXPERTS = 8
MOE_TB = 1024
MOE_WIN = 256
MOE_SUB = 80
MOE_GROUP = 2
VMEM_LIMIT = 56 * 1024 * 1024


def _cparams(sem):
    return pltpu.CompilerParams(dimension_semantics=sem, vmem_limit_bytes=VMEM_LIMIT)


def _rms(x, g):
    return x * lax.rsqrt(jnp.mean(x * x, axis=-1, keepdims=True) + RMS_EPS) * g


def _sigmoid(x):
    return 1.0 / (1.0 + jnp.exp(-x))


def _dot(a, b):
    return jnp.dot(a, b, preferred_element_type=F32)


def _ada_kernel(c_ref, w_ref, b_ref, o_ref):
    c = c_ref[...]
    c_act = c * _sigmoid(c)
    o_ref[...] = jnp.dot(c_act, w_ref[...], preferred_element_type=F32, precision=HIGHEST) + b_ref[...]


def _ada(c, w_ada, b_ada):
    depth, d, _ = w_ada.shape
    bsz = c.shape[0]
    rows = 8
    c_pad = jnp.pad(c, ((0, rows - bsz), (0, 0)))
    out = pl.pallas_call(
        _ada_kernel,
        grid=(depth, 6),
        in_specs=[
            pl.BlockSpec((rows, d), lambda i, j: (0, 0)),
            pl.BlockSpec((None, d, d), lambda i, j: (i, 0, j)),
            pl.BlockSpec((None, None, 1, d), lambda i, j: (i, j, 0, 0)),
        ],
        out_specs=pl.BlockSpec((None, None, rows, d), lambda i, j: (i, j, 0, 0)),
        out_shape=jax.ShapeDtypeStruct((depth, 6, rows, d), F32),
        compiler_params=_cparams(("arbitrary", "arbitrary")),
        name="ada_mod",
    )(c_pad, w_ada, b_ada.reshape(depth, 6, 1, d))
    return out[:, :, :bsz, :].transpose(0, 2, 1, 3)


def _ssm_pre_kernel(x_ref, mod_ref, g_ref, o_ref, scr):
    tm = x_ref.shape[0]
    n = tm // SSM_L
    u = _rms(x_ref[...], g_ref[0:1, :]) * (1.0 + mod_ref[1:2, :]) + mod_ref[0:1, :]
    n_lg = o_ref.shape[0]
    pitch = SSM_L + STRIDE_PAD
    for c in range(n):
        for lg in range(n_lg):
            scr[lg, c * pitch:c * pitch + SSM_L, :] = u[c * SSM_L:(c + 1) * SSM_L, lg * LANES:(lg + 1) * LANES]
    for s in range(SSM_L):
        for lg in range(n_lg):
            o_ref[lg, :, s * LANES:(s + 1) * LANES] = scr[lg, pl.ds(s, n, stride=pitch), :].astype(BF16)


def _ssm_pre(x, mod_l, g_l, tm):
    bsz, seq, d = x.shape
    n_lg = d // LANES
    return pl.pallas_call(
        _ssm_pre_kernel,
        grid=(bsz, seq // tm),
        in_specs=[
            pl.BlockSpec((None, tm, d), lambda b, i: (b, i, 0)),
            pl.BlockSpec((None, 6, d), lambda b, i: (b, 0, 0)),
            pl.BlockSpec((4, d), lambda b, i: (0, 0)),
        ],
        out_specs=pl.BlockSpec((None, n_lg, tm // SSM_L, SSM_L * LANES), lambda b, i: (b, 0, i, 0)),
        out_shape=jax.ShapeDtypeStruct((bsz, n_lg, seq // SSM_L, SSM_L * LANES), BF16),
        scratch_shapes=[pltpu.VMEM((n_lg, (tm // SSM_L) * (SSM_L + STRIDE_PAD), LANES), F32)],
        compiler_params=_cparams(("parallel", "parallel")),
        name="ssm_pre",
    )(x, mod_l, g_l)


def _gelu_tanh(y):
    return 0.5 * y * (1.0 + jnp.tanh(math.sqrt(2.0 / math.pi) * (y + 0.044715 * (y * y * y))))


def _same_group(shape, row_shift, col_shift):
    row_g = (lax.broadcasted_iota(jnp.int32, shape, 0) >> row_shift) & (SSM_LG - 1)
    col_g = (lax.broadcasted_iota(jnp.int32, shape, 1) >> col_shift) & (SSM_LG - 1)
    return row_g == col_g


def _ssm_expand_weights(xw_ref, yw_ref, cc_ref, t1_ref, t1t_ref, win_s, wout_s, m8_s, mx_s):
    hl = SSM_L // 2
    in_mask = _same_group((LANES, t1_ref.shape[1]), 4, 6)
    out_mask = _same_group((t1_ref.shape[1], LANES), 6, 4)
    k_mask = _same_group((LANES, LANES), 4, 4)
    m8_s[...] = jnp.zeros_like(m8_s)
    for s in range(SSM_L):
        rows = slice(s * LANES, (s + 1) * LANES)
        val = lax.dot_general(xw_ref[s], t1_ref[...], (((0,), (0,)), ((), ())), preferred_element_type=F32)
        win_s[rows, :] = jnp.where(in_mask, val, 0.0).astype(BF16)
        val = lax.dot_general(t1t_ref[...], yw_ref[s], (((1,), (1,)), ((), ())), preferred_element_type=F32)
        wout_s[:, rows] = jnp.where(out_mask, val, 0.0).astype(BF16)
    for tau in range(SSM_L):
        k_tau = lax.dot_general(xw_ref[SSM_L - 1 - tau], cc_ref[...], (((0,), (1,)), ((), ())),
                                preferred_element_type=F32)
        kbd = jnp.where(k_mask, k_tau, 0.0).astype(BF16)
        for a in range(hl):
            if a + tau < hl:
                m8_s[a * LANES:(a + 1) * LANES, (a + tau) * LANES:(a + tau + 1) * LANES] = kbd
            b = a + tau - hl
            if 0 <= b < hl:
                mx_s[a * LANES:(a + 1) * LANES, b * LANES:(b + 1) * LANES] = kbd


def _ssm_kernel(u_ref, xw_ref, yw_ref, cc_ref, t1_ref, t1t_ref, a_ref, d_ref, o_ref,
                win_s, wout_s, m8_s, mx_s, v_scr, h_scr):
    n_chunks = u_ref.shape[0]
    half = (SSM_L // 2) * LANES
    ns = a_ref.shape[1]

    @pl.when(pl.program_id(1) == 0)
    def _():
        _ssm_expand_weights(xw_ref, yw_ref, cc_ref, t1_ref, t1t_ref, win_s, wout_s, m8_s, mx_s)

    uc = u_ref[...]
    v_scr[...] = _dot(uc, win_s[...])
    ar = a_ref[0:1, :]
    ai = a_ref[1:2, :]

    def step(c, carry):
        hr, hi = carry
        h_scr[pl.ds(c, 1), pl.ds(0, ns)] = hr
        h_scr[pl.ds(c, 1), pl.ds(ns, ns)] = hi
        vr = v_scr[pl.ds(c, 1), pl.ds(0, ns)]
        vi = v_scr[pl.ds(c, 1), pl.ds(ns, ns)]
        return ar * hr - ai * hi + vr, ar * hi + ai * hr + vi

    zero = jnp.zeros((1, ns), F32)
    lax.fori_loop(0, n_chunks, step, (zero, zero))
    y = _dot(h_scr[...].astype(BF16), wout_s[...])
    u_lo = uc[:, :half]
    u_hi = uc[:, half:]
    m8 = m8_s[...]
    y_lo = y[:, :half] + _dot(u_lo, m8)
    y_hi = y[:, half:] + _dot(u_hi, m8) + _dot(u_lo, mx_s[...])
    dd = d_ref[...]
    o_ref[:, :half] = _gelu_tanh(y_lo + dd[:, :half] * u_lo.astype(F32)).astype(BF16)
    o_ref[:, half:] = _gelu_tanh(y_hi + dd[:, half:] * u_hi.astype(F32)).astype(BF16)


def _ssm(ucat, xw, yw, cc, t1, t1t, a16, dcat):
    bsz, n_lg, n_chunks, width = ucat.shape
    ns2 = t1.shape[-1]
    half = width // 2

    def per_lg(a):
        return pl.BlockSpec((None,) + a.shape[1:], lambda g, b: (g,) + (0,) * (a.ndim - 1))

    def const(a):
        return pl.BlockSpec(a.shape, lambda g, b: (0, 0))

    return pl.pallas_call(
        _ssm_kernel,
        grid=(n_lg, bsz),
        in_specs=[
            pl.BlockSpec((None, None, n_chunks, width), lambda g, b: (b, g, 0, 0)),
            per_lg(xw), per_lg(yw), per_lg(cc), const(t1), const(t1t), per_lg(a16), per_lg(dcat),
        ],
        out_specs=pl.BlockSpec((None, None, n_chunks, width), lambda g, b: (b, g, 0, 0)),
        out_shape=jax.ShapeDtypeStruct(ucat.shape, BF16),
        scratch_shapes=[
            pltpu.VMEM((width, ns2), BF16), pltpu.VMEM((ns2, width), BF16),
            pltpu.VMEM((half, half), BF16), pltpu.VMEM((half, half), BF16),
            pltpu.VMEM((n_chunks, ns2), F32), pltpu.VMEM((n_chunks, ns2), F32),
        ],
        compiler_params=_cparams(("arbitrary", "arbitrary")),
        name="ssm_scan",
    )(ucat, xw, yw, cc, t1, t1t, a16, dcat)


def _ssm_post_kernel(z_ref, x_ref, w_ref, mod_ref, g_ref, o_ref, zp, rp):
    n_lg, n, _ = z_ref.shape
    d = x_ref.shape[1]
    for s in range(SSM_L):
        for lg in range(n_lg):
            zp[s * n:(s + 1) * n, lg * LANES:(lg + 1) * LANES] = z_ref[lg, :, s * LANES:(s + 1) * LANES]
    gl = _dot(zp[...], w_ref[...])
    h = gl[:, :d] * _sigmoid(gl[:, d:])
    r = mod_ref[2:3, :] * _rms(h, g_ref[1:2, :])
    pitch = n + STRIDE_PAD
    for s in range(SSM_L):
        for lg in range(n_lg):
            rp[lg, s * pitch:s * pitch + n, :] = r[s * n:(s + 1) * n, lg * LANES:(lg + 1) * LANES]
    sub = 8
    for c in range(n):
        for k in range(SSM_L // sub):
            rows = pl.ds(c * SSM_L + k * sub, sub)
            for lg in range(n_lg):
                cols = slice(lg * LANES, (lg + 1) * LANES)
                o_ref[rows, cols] = x_ref[rows, cols] + rp[lg, pl.ds(k * sub * pitch + c, sub, stride=pitch), :]


def _ssm_post(zcat, x, w_glu, mod_l, g_l, tm):
    bsz, seq, d = x.shape
    n_lg = d // LANES
    return pl.pallas_call(
        _ssm_post_kernel,
        grid=(bsz, seq // tm),
        in_specs=[
            pl.BlockSpec((None, n_lg, tm // SSM_L, SSM_L * LANES), lambda b, i: (b, 0, i, 0)),
            pl.BlockSpec((None, tm, d), lambda b, i: (b, i, 0)),
            pl.BlockSpec((d, 2 * d), lambda b, i: (0, 0)),
            pl.BlockSpec((None, 6, d), lambda b, i: (b, 0, 0)),
            pl.BlockSpec((4, d), lambda b, i: (0, 0)),
        ],
        out_specs=pl.BlockSpec((None, tm, d), lambda b, i: (b, i, 0)),
        out_shape=jax.ShapeDtypeStruct(x.shape, F32),
        scratch_shapes=[pltpu.VMEM((tm, d), BF16),
                        pltpu.VMEM((n_lg, SSM_L * (tm // SSM_L + STRIDE_PAD), LANES), F32)],
        compiler_params=_cparams(("parallel", "parallel")),
        name="ssm_post",
    )(zcat, x, w_glu, mod_l, g_l)


def _ssm_weights(a_re, a_im, log_dt, b_re, b_im, c_re, c_im, d_skip):
    n_groups = a_re.shape[0]
    n_lg = n_groups // SSM_LG
    p, hh, ll = SSM_STATE, SSM_GROUP, SSM_L
    dt = jnp.exp(log_dt)[:, None]
    k = jnp.arange(ll + 1, dtype=F32)[:, None, None]
    mag = jnp.exp(k * (a_re * dt))
    pw_re = mag * jnp.cos(k * (a_im * dt))
    pw_im = mag * jnp.sin(k * (a_im * dt))
    den = a_re * a_re + a_im * a_im
    num_re, num_im = pw_re[1] - 1.0, pw_im[1]
    f_re = (num_re * a_re + num_im * a_im) / den
    f_im = (num_im * a_re - num_re * a_im) / den
    bb_re = f_re[..., None] * b_re - f_im[..., None] * b_im
    bb_im = f_re[..., None] * b_im + f_im[..., None] * b_re
    b2_re, b2_im = (x.reshape(n_lg, SSM_LG, p, hh).transpose(0, 2, 1, 3).reshape(n_lg, p, LANES)
                    for x in (bb_re, bb_im))

    def lanes_g(x):
        x = x.reshape(x.shape[0], n_lg, SSM_LG, p).transpose(0, 1, 3, 2)
        return jnp.repeat(x, hh, axis=-1)

    l_re, l_im = lanes_g(pw_re[:ll][::-1]), lanes_g(pw_im[:ll][::-1])
    xw = jnp.concatenate([l_re * b2_re - l_im * b2_im, l_re * b2_im + l_im * b2_re], axis=2)
    xw = xw.transpose(1, 0, 2, 3)
    cp_re = c_re[None] * pw_re[1:, :, None, :] - c_im[None] * pw_im[1:, :, None, :]
    cp_im = c_re[None] * pw_im[1:, :, None, :] + c_im[None] * pw_re[1:, :, None, :]
    yw = jnp.concatenate([cp_re, -cp_im], axis=-1).reshape(ll, n_lg, LANES, 2 * p).transpose(1, 0, 2, 3)
    cc = jnp.concatenate([c_re, -c_im], axis=-1).reshape(n_lg, LANES, 2 * p)
    j1 = jnp.arange(2 * SSM_LG * p)
    t1 = (jnp.arange(2 * p)[:, None] == ((j1 // (SSM_LG * p)) * p + j1 % p)[None, :]).astype(BF16)
    a16 = jnp.stack([pw_re[ll], pw_im[ll]], axis=0).reshape(2, n_lg, SSM_LG * p).transpose(1, 0, 2)
    dcat = jnp.tile(d_skip.reshape(n_lg, 1, LANES), (1, 1, ll))
    return xw.astype(BF16), yw.astype(BF16), cc.astype(BF16), t1, t1.T, a16, dcat


def _ffn_kernel(x_ref, wg_ref, wu_ref, wo_ref, mod_ref, g_ref, o_ref, h_scr, acc):
    k = pl.program_id(2)

    @pl.when(k == 0)
    def _():
        h = _rms(x_ref[...], g_ref[2:3, :]) * (1.0 + mod_ref[4:5, :]) + mod_ref[3:4, :]
        h_scr[...] = h.astype(BF16)
        acc[...] = jnp.zeros_like(acc)

    h = h_scr[...]
    g = _dot(h, wg_ref[...])
    u = _dot(h, wu_ref[...])
    act = (g * _sigmoid(g) * u).astype(BF16)
    acc[...] += _dot(act, wo_ref[...])

    @pl.when(k == pl.num_programs(2) - 1)
    def _():
        o_ref[...] = x_ref[...] + mod_ref[5:6, :] * _rms(acc[...], g_ref[3:4, :])


def _ffn(x, w_in, w_out, mod_l, g_l, tm, tf):
    bsz, seq, d = x.shape
    f = w_out.shape[0]
    nk = f // tf
    return pl.pallas_call(
        _ffn_kernel,
        grid=(bsz, seq // tm, nk),
        in_specs=[
            pl.BlockSpec((None, tm, d), lambda b, i, k: (b, i, 0)),
            pl.BlockSpec((d, tf), lambda b, i, k: (0, k)),
            pl.BlockSpec((d, tf), lambda b, i, k: (0, k + nk)),
            pl.BlockSpec((tf, d), lambda b, i, k: (k, 0)),
            pl.BlockSpec((None, 6, d), lambda b, i, k: (b, 0, 0)),
            pl.BlockSpec((4, d), lambda b, i, k: (0, 0)),
        ],
        out_specs=pl.BlockSpec((None, tm, d), lambda b, i, k: (b, i, 0)),
        out_shape=jax.ShapeDtypeStruct(x.shape, F32),
        scratch_shapes=[pltpu.VMEM((tm, d), BF16), pltpu.VMEM((tm, d), F32)],
        compiler_params=_cparams(("parallel", "parallel", "arbitrary")),
        name="ffn",
    )(x, w_in, w_in, w_out, mod_l, g_l)


def _rope_kernel(pos_ref, f_ref, cos_ref, sin_ref):
    ang = pos_ref[...].astype(F32) * f_ref[...]
    cos_ref[...] = jnp.cos(ang)
    sin_ref[...] = jnp.sin(ang)


def _rope_tables(positions):
    bsz, seq = positions.shape
    half = MLA_ROPE // 2
    inv_freq = ROPE_THETA ** (-jnp.arange(half, dtype=F32) / half)
    per_row = LANES // half
    rows = bsz * seq // per_row
    pos_rep = jnp.broadcast_to(positions[..., None], (bsz, seq, half)).reshape(rows, LANES)
    f_rep = jnp.tile(inv_freq, per_row).reshape(1, LANES)
    tr = min(rows, 512)
    cos, sin = pl.pallas_call(
        _rope_kernel,
        grid=(rows // tr,),
        in_specs=[pl.BlockSpec((tr, LANES), lambda i: (i, 0)), pl.BlockSpec((1, LANES), lambda i: (0, 0))],
        out_specs=[pl.BlockSpec((tr, LANES), lambda i: (i, 0))] * 2,
        out_shape=[jax.ShapeDtypeStruct((rows, LANES), F32)] * 2,
        compiler_params=_cparams(("parallel",)),
        name="rope_tables",
    )(pos_rep, f_rep)
    cos = cos.reshape(bsz, seq, half)
    sin = sin.reshape(bsz, seq, half)
    ones = jnp.ones((bsz, seq, MLA_NOPE), F32)
    zpad = jnp.zeros((bsz, seq, LANES - MLA_QK), F32)
    z16 = jnp.zeros((bsz, seq, half), F32)
    cos_t = jnp.concatenate([ones, cos, cos, zpad], axis=-1)
    sin_lo = jnp.concatenate([0.0 * ones, -sin, z16, zpad], axis=-1)
    sin_hi = jnp.concatenate([0.0 * ones, z16, sin, zpad], axis=-1)
    return cos_t, sin_lo, sin_hi


def _mla_proj_kernel(x_ref, cos_ref, sin_lo_ref, sin_hi_ref, mod_ref, g_ref, wcq_ref, wckv_ref, wkr_ref,
                     qn_ref, kvn_ref, wq_ref, wk_ref, wv_ref, exp_ref, q_ref, k_ref, v_ref):
    h = (_rms(x_ref[...], g_ref[0:1, :]) * (1.0 + mod_ref[1:2, :]) + mod_ref[0:1, :]).astype(BF16)
    cq = _rms(_dot(h, wcq_ref[...]), qn_ref[...]).astype(BF16)
    ckv = _rms(_dot(h, wckv_ref[...]), kvn_ref[...]).astype(BF16)
    kr = _dot(h, wkr_ref[...])
    cos = cos_ref[...]
    sin_lo = sin_lo_ref[...]
    sin_hi = sin_hi_ref[...]
    half = MLA_ROPE // 2

    def rope(t):
        width = t.shape[1]
        up = pltpu.roll(t, half, axis=1)
        down = pltpu.roll(t, width - half, axis=1)
        return [t[:, s:s + LANES] * cos + down[:, s:s + LANES] * sin_lo + up[:, s:s + LANES] * sin_hi
                for s in range(0, width, LANES)]

    q = _dot(cq, wq_ref[...])
    for hd, q_hd in enumerate(rope(q)):
        q_ref[:, hd * LANES:(hd + 1) * LANES] = q_hd.astype(BF16)
    kr_rot = rope(kr)[0].astype(BF16)
    k_ref[...] = (_dot(ckv, wk_ref[...]) + _dot(kr_rot, exp_ref[...])).astype(BF16)
    vt = lax.dot_general(wv_ref[...], ckv, (((1,), (1,)), ((), ())), preferred_element_type=F32)
    row = lax.broadcasted_iota(jnp.int32, vt.shape, 0)
    v_ref[...] = jnp.where(row % V_ROWS == MLA_V, 1.0, vt).astype(BF16)


def _mla_proj(x, rope_t, mod_l, g_l, wts, tm):
    bsz, seq, d = x.shape
    hw = MLA_HEADS * LANES

    def full(a):
        return pl.BlockSpec(a.shape, lambda b, i: (0,) * a.ndim)

    out_sds = jax.ShapeDtypeStruct((bsz, seq, hw), BF16)
    vt_rows = MLA_HEADS * V_ROWS
    vt_sds = jax.ShapeDtypeStruct((bsz, seq // tm, vt_rows, tm), BF16)
    return pl.pallas_call(
        _mla_proj_kernel,
        grid=(bsz, seq // tm),
        in_specs=[
            pl.BlockSpec((None, tm, d), lambda b, i: (b, i, 0)),
            pl.BlockSpec((None, tm, LANES), lambda b, i: (b, i, 0)),
            pl.BlockSpec((None, tm, LANES), lambda b, i: (b, i, 0)),
            pl.BlockSpec((None, tm, LANES), lambda b, i: (b, i, 0)),
            pl.BlockSpec((None, 6, d), lambda b, i: (b, 0, 0)),
            pl.BlockSpec((4, d), lambda b, i: (0, 0)),
        ] + [full(w) for w in wts],
        out_specs=[pl.BlockSpec((None, tm, hw), lambda b, i: (b, i, 0))] * 2
        + [pl.BlockSpec((None, None, vt_rows, tm), lambda b, i: (b, i, 0, 0))],
        out_shape=[out_sds, out_sds, vt_sds],
        compiler_params=_cparams(("parallel", "parallel")),
        name="mla_proj",
    )(x, *rope_t, mod_l, g_l, *wts)


def _mla_weights(w_in, q_norm, kv_norm, w_uq, w_ukv, w_o):
    d = w_in.shape[0]
    half = MLA_ROPE // 2
    pad = LANES - MLA_QK
    w_cq = w_in[:, :MLA_Q_RANK]
    w_ckv = w_in[:, MLA_Q_RANK:MLA_Q_RANK + MLA_KV_RANK]
    w_kr = jnp.pad(w_in[:, MLA_Q_RANK + MLA_KV_RANK:], ((0, 0), (MLA_NOPE, pad)))
    scale = MLA_QK ** -0.5 * math.log2(math.e)
    wq = w_uq.reshape(MLA_Q_RANK, MLA_HEADS, MLA_QK) * scale
    wq_pad = jnp.pad(wq, ((0, 0), (0, 0), (0, pad))).reshape(MLA_Q_RANK, MLA_HEADS * LANES)
    wkv =w_ukv.reshape(MLA_KV_RANK, MLA_HEADS, MLA_NOPE + MLA_V)
    zk = jnp.zeros((MLA_KV_RANK, MLA_HEADS, LANES - MLA_NOPE), F32)
    wk_pad = jnp.concatenate([wkv[..., :MLA_NOPE], zk], axis=-1).reshape(MLA_KV_RANK, MLA_HEADS * LANES)
    wv_t = jnp.pad(wkv[..., MLA_NOPE:].transpose(1, 2, 0), ((0, 0), (0, V_ROWS - MLA_V), (0, 0)))
    wv_t = wv_t.reshape(MLA_HEADS * V_ROWS, MLA_KV_RANK)
    lane = jnp.arange(LANES)
    is_rope = (lane >= MLA_NOPE) & (lane < MLA_QK)
    expand = jnp.tile((jnp.eye(LANES, dtype=F32) * is_rope[None, :].astype(F32)), (1, MLA_HEADS))
    proj_w = (w_cq.astype(BF16), w_ckv.astype(BF16), w_kr.astype(BF16),
              q_norm.reshape(1, -1), kv_norm.reshape(1, -1),
              wq_pad.astype(BF16), wk_pad.astype(BF16), wv_t.astype(BF16), expand.astype(BF16))
    return proj_w, w_o.astype(BF16)


def _attn_kernel(q_ref, k_ref, vt_ref, o_ref, m_scr, mt_scr, acc, s0_scr, s1_scr, qt_scr):
    tk = vt_ref.shape[2]
    qi = pl.program_id(2)
    s_scr = (s0_scr, s1_scr)
    qt_scr[...] = q_ref[...].astype(F32).T.astype(BF16)
    m_scr[...] = jnp.full_like(m_scr, -jnp.inf)
    acc[...] = jnp.zeros_like(acc)

    def scores(hd, j):
        lanes = slice(hd * LANES, (hd + 1) * LANES)
        k = k_ref[pl.ds(pl.multiple_of(j * tk, tk), tk), lanes]
        st = _dot(k, qt_scr[lanes, :])
        s_scr[hd][...] = st
        mt_scr[hd] = jnp.max(st, axis=0, keepdims=True)

    def accumulate(hd, j, masked):
        st = s_scr[hd][...]
        if masked:
            key = lax.broadcasted_iota(jnp.int32, st.shape, 0)
            qry = lax.broadcasted_iota(jnp.int32, st.shape, 1)
            st = jnp.where(key <= qry, st, -jnp.inf)
            mt = jnp.max(st, axis=0, keepdims=True)
        else:
            mt = mt_scr[hd]
        m_prev = m_scr[hd]
        m_new = jnp.maximum(m_prev, mt)
        alpha = jnp.exp2(m_prev - m_new)
        pt = jnp.exp2(st - m_new).astype(BF16)
        vt = vt_ref[j, hd * V_ROWS:(hd + 1) * V_ROWS, :]
        acc[hd] = alpha * acc[hd] + _dot(vt, pt)
        m_scr[hd] = m_new

    scores(0, 0)

    def body(j, carry):
        scores(1, j)
        accumulate(0, j, False)
        scores(0, j + 1)
        accumulate(1, j, False)
        return carry

    def body_pair(jj, carry):
        body(2 * jj, carry)
        return body(2 * jj + 1, carry)

    lax.fori_loop(0, qi // 2, body_pair, 0)

    @pl.when(qi % 2 == 1)
    def _():
        body(qi - 1, 0)

    scores(1, qi)
    accumulate(0, qi, True)
    accumulate(1, qi, True)
    outs = []
    for hd in range(2):
        a = acc[hd]
        outs.append(a[:MLA_V, :] / a[MLA_V:MLA_V + 1, :])
    o_ref[...] = jnp.concatenate(outs, axis=0).T.astype(BF16)


def _attention(q, k, vt, tq):
    bsz, seq, hw = q.shape
    n_pairs = hw // (2 * LANES)
    n_kt, _, tk = vt.shape[1:]
    assert tq == tk, "the causal diagonal tile assumes square tiles"
    return pl.pallas_call(
        _attn_kernel,
        grid=(bsz, n_pairs, seq // tq),
        in_specs=[
            pl.BlockSpec((None, tq, 2 * LANES), lambda b, h, i: (b, i, h)),
            pl.BlockSpec((None, seq, 2 * LANES), lambda b, h, i: (b, 0, h)),
            pl.BlockSpec((None, n_kt, 2 * V_ROWS, tk), lambda b, h, i: (b, 0, h, 0)),
        ],
        out_specs=pl.BlockSpec((None, tq, 2 * MLA_V), lambda b, h, i: (b, i, h)),
        out_shape=jax.ShapeDtypeStruct((bsz, seq, n_pairs * 2 * MLA_V), BF16),
        scratch_shapes=[pltpu.VMEM((2, 1, tq), F32), pltpu.VMEM((2, 1, tq), F32), pltpu.VMEM((2, V_ROWS, tq), F32),
                        pltpu.VMEM((tk, tq), F32), pltpu.VMEM((tk, tq), F32), pltpu.VMEM((2 * LANES, tq), BF16)],
        compiler_params=_cparams(("parallel", "parallel", "arbitrary")),
        name="mla_attention",
    )(q, k, vt)


def _attn_out_kernel(a_ref, x_ref, w_ref, mod_ref, g_ref, o_ref):
    h = _dot(a_ref[...], w_ref[...])
    o_ref[...] = x_ref[...] + mod_ref[2:3, :] * _rms(h, g_ref[1:2, :])


def _attn_out(a, x, wo_pad, mod_l, g_l, tm):
    bsz, seq, d = x.shape
    hw = a.shape[-1]
    return pl.pallas_call(
        _attn_out_kernel,
        grid=(bsz, seq // tm),
        in_specs=[
            pl.BlockSpec((None, tm, hw), lambda b, i: (b, i, 0)),
            pl.BlockSpec((None, tm, d), lambda b, i: (b, i, 0)),
            pl.BlockSpec((hw, d), lambda b, i: (0, 0)),
            pl.BlockSpec((None, 6, d), lambda b, i: (b, 0, 0)),
            pl.BlockSpec((4, d), lambda b, i: (0, 0)),
        ],
        out_specs=pl.BlockSpec((None, tm, d), lambda b, i: (b, i, 0)),
        out_shape=jax.ShapeDtypeStruct(x.shape, F32),
        compiler_params=_cparams(("parallel", "parallel")),
        name="mla_out",
    )(a, x, wo_pad, mod_l, g_l)


def _route(logits):
    lane = lax.broadcasted_iota(jnp.int32, logits.shape, 1)
    neg = -jnp.inf
    lg = jnp.where(lane < N_EXPERTS, logits, neg)
    m1 = jnp.max(lg, axis=-1, keepdims=True)
    i1 = jnp.min(jnp.where(lg == m1, lane, LANES), axis=-1, keepdims=True)
    lg2 = jnp.where(lane == i1, neg, lg)
    m2 = jnp.max(lg2, axis=-1, keepdims=True)
    i2 = jnp.min(jnp.where(lg2 == m2, lane, LANES), axis=-1, keepdims=True)
    e = jnp.exp(m2 - m1)
    w1 = 1.0 / (1.0 + e)
    w2 = e / (1.0 + e)
    first = lane == i1
    second = lane == i2
    comb = jnp.where(first, w1, 0.0) + jnp.where(second, w2, 0.0)
    sel = jnp.where(first, 1.0, 0.0) + jnp.where(second, 1.0, 0.0)
    return comb, sel


def _moe_kernel(x_ref, wr_ref, br_ref, tri_ref, wg_ref, wu_ref, wo_ref, mod_ref, g_ref, o_ref,
                h_scr, sel_scr, rank_scr, w_scr, acc):
    e = pl.program_id(1)
    tb = x_ref.shape[0]

    @pl.when(e == 0)
    def _():
        h = _rms(x_ref[...], g_ref[2:3, :]) * (1.0 + mod_ref[4:5, :]) + mod_ref[3:4, :]
        h_hi = h.astype(BF16)
        h_lo = (h - h_hi.astype(F32)).astype(BF16)
        logits = _dot(h_hi, wr_ref[0]) + _dot(h_hi, wr_ref[1]) + _dot(h_lo, wr_ref[0]) + br_ref[...]
        comb, sel = _route(logits)
        sel_t = sel.T
        sel_scr[...] = sel_t
        w_scr[...] = comb.T
        rank_scr[...] = _dot(sel_t.astype(BF16), tri_ref[...])
        h_scr[...] = h_hi
        acc[...] = jnp.zeros_like(acc)

    sel_row = sel_scr[pl.ds(e, 1), :]
    rank_row = rank_scr[pl.ds(e, 1), :]
    w_row = w_scr[pl.ds(e, 1), :]

    def run_group(first_win):
        wins = [slice((first_win + i) * MOE_WIN, (first_win + i + 1) * MOE_WIN) for i in range(MOE_GROUP)]
        n_rows = [jnp.sum(sel_row[:, w]).astype(jnp.int32) for w in wins]
        n_steps = (functools.reduce(jnp.maximum, n_rows) + MOE_SUB - 1) // MOE_SUB

        def step(r, carry):
            slot = (lax.broadcasted_iota(jnp.int32, (MOE_SUB, MOE_WIN), 0) + r * MOE_SUB).astype(F32)
            hits = [rank_row[:, w] == slot for w in wins]
            onehots = [jnp.where(hit, sel_row[:, w], 0.0).astype(BF16) for hit, w in zip(hits, wins)]
            xs = jnp.concatenate([_dot(oh, h_scr[w, :]) for oh, w in zip(onehots, wins)], axis=0).astype(BF16)
            g = _dot(xs, wg_ref[...])
            u = _dot(xs, wu_ref[...])
            act = (g * _sigmoid(g) * u).astype(BF16)
            y = _dot(act, wo_ref[...])
            for i, (hit, oh, w) in enumerate(zip(hits, onehots, wins)):
                w_col = jnp.sum(jnp.where(hit, w_row[:, w], 0.0), axis=1, keepdims=True)
                yw = (y[i * MOE_SUB:(i + 1) * MOE_SUB, :] * w_col).astype(BF16)
                acc[w, :] += lax.dot_general(oh, yw, (((0,), (0,)), ((), ())), preferred_element_type=F32)
            return carry

        lax.fori_loop(0, n_steps, step, 0)

    for first_win in range(0, tb // MOE_WIN, MOE_GROUP):
        run_group(first_win)

    @pl.when(e == pl.num_programs(1) - 1)
    def _():
        o_ref[...] = x_ref[...] + mod_ref[5:6, :] * _rms(acc[...], g_ref[3:4, :])


def _moe_layer(x, w_router, b_router, w_in, w_out, mod_l, g_l):
    bsz, seq, d = x.shape
    n_e, f, _ = w_out.shape
    tb = min(MOE_TB, seq)
    blocks_per_batch = seq // tb
    wr = jnp.pad(w_router, ((0, 0), (0, LANES - n_e)))
    wr_hi = wr.astype(BF16)
    wr = jnp.stack([wr_hi, (wr - wr_hi.astype(F32)).astype(BF16)])
    br =jnp.pad(b_router, (0, LANES - n_e)).reshape(1, LANES)
    pos = jnp.arange(tb)
    win = pos // MOE_WIN
    tri = ((pos[:, None] < pos[None, :]) & (win[:, None] == win[None, :])).astype(BF16)
    w_in = w_in.astype(BF16)
    out = pl.pallas_call(
        _moe_kernel,
        grid=(bsz * blocks_per_batch, n_e),
        in_specs=[
            pl.BlockSpec((tb, d), lambda i, e: (i, 0)),
            pl.BlockSpec((2, d, LANES), lambda i, e: (0, 0, 0)),
            pl.BlockSpec((1, LANES), lambda i, e: (0, 0)),
            pl.BlockSpec((tb, tb), lambda i, e: (0, 0)),
            pl.BlockSpec((None, d, f), lambda i, e: (e, 0, 0)),
            pl.BlockSpec((None, d, f), lambda i, e: (e, 0, 1)),
            pl.BlockSpec((None, f, d), lambda i, e: (e, 0, 0)),
            pl.BlockSpec((None, 6, d), lambda i, e: (i // blocks_per_batch, 0, 0)),
            pl.BlockSpec((4, d), lambda i, e: (0, 0)),
        ],
        out_specs=pl.BlockSpec((tb, d), lambda i, e: (i, 0)),
        out_shape=jax.ShapeDtypeStruct((bsz * seq, d), F32),
        scratch_shapes=[pltpu.VMEM((tb, d), BF16), pltpu.VMEM((LANES, tb), F32), pltpu.VMEM((LANES, tb), F32),
                        pltpu.VMEM((LANES, tb), F32), pltpu.VMEM((tb, d), F32)],
        compiler_params=_cparams(("parallel", "arbitrary")),
        name="moe",
    )(x.reshape(bsz * seq, d), wr, br, tri, w_in, w_in, w_out.astype(BF16), mod_l, g_l)
    return out.reshape(bsz, seq, d)


def kernel(x, c, positions, norm_g, w_ada, b_ada, ssm_a_re, ssm_a_im, ssm_log_dt, ssm_b_re, ssm_b_im, ssm_c_re, ssm_c_im, ssm_d, ssm_w_glu, ffn_w_in, ffn_w_out, mla_w_in, mla_q_norm, mla_kv_norm, mla_w_uq, mla_w_ukv, mla_w_o, moe_w_router, moe_b_router, moe_w_in, moe_w_out):
    depth = norm_g.shape[0]
    seq = x.shape[1]
    tm = min(512, seq)
    mod = _ada(c, w_ada, b_ada)
    rope_t = None
    for i in range(depth):
        j = i // 2
        mod_l, g_l = mod[i], norm_g[i]
        if i % 2 == 0:
            ssm_w = _ssm_weights(ssm_a_re[j], ssm_a_im[j], ssm_log_dt[j], ssm_b_re[j], ssm_b_im[j],
                                 ssm_c_re[j], ssm_c_im[j], ssm_d[j])
            ucat = _ssm_pre(x, mod_l, g_l, tm)
            zcat = _ssm(ucat, *ssm_w)
            x = _ssm_post(zcat, x, ssm_w_glu[j].astype(BF16), mod_l, g_l, tm)
            f = ffn_w_out.shape[1]
            x = _ffn(x, ffn_w_in[j].astype(BF16), ffn_w_out[j].astype(BF16), mod_l, g_l, tm, f // 2)
        else:
            if rope_t is None:
                rope_t = _rope_tables(positions)
            proj_w, wo_pad = _mla_weights(mla_w_in[j], mla_q_norm[j], mla_kv_norm[j],
                                          mla_w_uq[j], mla_w_ukv[j], mla_w_o[j])
            q, k, v = _mla_proj(x, rope_t, mod_l, g_l, proj_w, tm)
            a = _attention(q, k, v, tm)
            x = _attn_out(a, x, wo_pad, mod_l, g_l, tm)
            x = _moe_layer(x, moe_w_router[j], moe_b_router[j], moe_w_in[j], moe_w_out[j], mod_l, g_l)
    return x
```

```python
import functools
import math

import jax
import jax.numpy as jnp
from jax import lax
from jax.experimental import pallas as pl
from jax.experimental.pallas import tpu as pltpu

F32 = jnp.float32
BF16 = jnp.bfloat16

RMS_EPS = 1e-6
LANES = 128
SSM_GROUP = 16
SSM_STATE = 64
SSM_L = 16
SSM_LG = LANES // SSM_GROUP
STRIDE_PAD = 4
MLA_HEADS = 16
MLA_NOPE = 64
MLA_ROPE = 32
MLA_V = 64
MLA_QK = MLA_NOPE + MLA_ROPE
V_ROWS = 80
MLA_Q_RANK = 384
MLA_KV_RANK = 256
ROPE_THETA = 10000.0
N_EXPERTS = 8
MOE_TB = 1024
MOE_SUB = 144
VMEM_LIMIT = 56 * 1024 * 1024


def _cparams(sem):
    return pltpu.CompilerParams(dimension_semantics=sem, vmem_limit_bytes=VMEM_LIMIT)


def _rms(x, g):
    return x * lax.rsqrt(jnp.mean(x * x, axis=-1, keepdims=True) + RMS_EPS) * g


def _sigmoid(x):
    return 1.0 / (1.0 + jnp.exp(-x))


def _dot(a, b):
    return jnp.dot(a, b, preferred_element_type=F32)


def _ada_kernel(c_ref, w_ref, b_ref, o_ref):
    c = c_ref[...]
    c_act = c * _sigmoid(c)
    c_hi = c_act.astype(BF16)
    c_lo = (c_act - c_hi.astype(F32)).astype(BF16)
    w = w_ref[...]
    w_hi = w.astype(BF16)
    w_lo = (w - w_hi.astype(F32)).astype(BF16)
    o_ref[...] = _dot(c_hi, w_hi) + _dot(c_hi, w_lo) + _dot(c_lo, w_hi) + b_ref[...]


def _ada(c, w_ada, b_ada):
    depth, d, _ = w_ada.shape
    bsz = c.shape[0]
    rows = 16
    c_pad =jnp.pad(c, ((0, rows - bsz), (0, 0)))
    out = pl.pallas_call(
        _ada_kernel,
        grid=(depth, 6),
        in_specs=[
            pl.BlockSpec((rows, d), lambda i, j: (0, 0)),
            pl.BlockSpec((None, d, d), lambda i, j: (i, 0, j)),
            pl.BlockSpec((None, None, 1, d), lambda i, j: (i, j, 0, 0)),
        ],
        out_specs=pl.BlockSpec((None, None, rows, d), lambda i, j: (i, j, 0, 0)),
        out_shape=jax.ShapeDtypeStruct((depth, 6, rows, d), F32),
        compiler_params=_cparams(("arbitrary", "arbitrary")),
        name="ada_mod",
    )(c_pad, w_ada, b_ada.reshape(depth, 6, 1, d))
    return out[:, :, :bsz, :].transpose(0, 2, 1, 3)


def _ssm_pre_kernel(x_ref, mod_ref, g_ref, o_ref, scr):
    tm = x_ref.shape[0]
    n = tm // SSM_L
    u = _rms(x_ref[...], g_ref[0:1, :]) * (1.0 + mod_ref[1:2, :]) + mod_ref[0:1, :]
    n_lg = o_ref.shape[0]
    pitch = SSM_L + STRIDE_PAD
    for c in range(n):
        for lg in range(n_lg):
            scr[lg, c * pitch:c * pitch + SSM_L, :] = u[c * SSM_L:(c + 1) * SSM_L, lg * LANES:(lg + 1) * LANES]
    for s in range(SSM_L):
        for lg in range(n_lg):
            o_ref[lg, :, s * LANES:(s + 1) * LANES] = scr[lg, pl.ds(s, n, stride=pitch), :].astype(BF16)


def _ssm_pre(x, mod_l, g_l, tm):
    bsz, seq, d = x.shape
    n_lg = d // LANES
    return pl.pallas_call(
        _ssm_pre_kernel,
        grid=(bsz, seq // tm),
        in_specs=[
            pl.BlockSpec((None, tm, d), lambda b, i: (b, i, 0)),
            pl.BlockSpec((None, 6, d), lambda b, i: (b, 0, 0)),
            pl.BlockSpec((4, d), lambda b, i: (0, 0)),
        ],
        out_specs=pl.BlockSpec((None, n_lg, tm // SSM_L, SSM_L * LANES), lambda b, i: (b, 0, i, 0)),
        out_shape=jax.ShapeDtypeStruct((bsz, n_lg, seq // SSM_L, SSM_L * LANES), BF16),
        scratch_shapes=[pltpu.VMEM((n_lg, (tm // SSM_L) * (SSM_L + STRIDE_PAD), LANES), F32)],
        compiler_params=_cparams(("parallel", "parallel")),
        name="ssm_pre",
    )(x, mod_l, g_l)


def _gelu_tanh(y):
    return 0.5 * y * (1.0 + jnp.tanh(math.sqrt(2.0 / math.pi) * (y + 0.044715 * (y * y * y))))


def _same_group(shape, row_shift, col_shift):
    row_g = (lax.broadcasted_iota(jnp.int32, shape, 0) >> row_shift) & (SSM_LG - 1)
    col_g = (lax.broadcasted_iota(jnp.int32, shape, 1) >> col_shift) & (SSM_LG - 1)
    return row_g == col_g


def _ssm_expand_weights(xw_ref, yw_ref, cc_ref, t1_ref, t1t_ref, win_s, wout_s, m8_s, mx_s):
    hl = SSM_L // 2
    in_mask = _same_group((LANES, t1_ref.shape[1]), 4, 6)
    out_mask = _same_group((t1_ref.shape[1], LANES), 6, 4)
    k_mask = _same_group((LANES, LANES), 4, 4)
    m8_s[...] = jnp.zeros_like(m8_s)
    for s in range(SSM_L):
        rows = slice(s * LANES, (s + 1) * LANES)
        val = lax.dot_general(xw_ref[s], t1_ref[...], (((0,), (0,)), ((), ())), preferred_element_type=F32)
        win_s[rows, :] = jnp.where(in_mask, val, 0.0).astype(BF16)
        val = lax.dot_general(t1t_ref[...], yw_ref[s], (((1,), (1,)), ((), ())), preferred_element_type=F32)
        wout_s[:, rows] = jnp.where(out_mask, val, 0.0).astype(BF16)
    for tau in range(SSM_L):
        k_tau = lax.dot_general(xw_ref[SSM_L - 1 - tau], cc_ref[...], (((0,), (1,)), ((), ())),
                                preferred_element_type=F32)
        kbd = jnp.where(k_mask, k_tau, 0.0).astype(BF16)
        for a in range(hl):
            if a + tau < hl:
                m8_s[a * LANES:(a + 1) * LANES, (a + tau) * LANES:(a + tau + 1) * LANES] = kbd
            b = a + tau - hl
            if 0 <= b < hl:
                mx_s[a * LANES:(a + 1) * LANES, b * LANES:(b + 1) * LANES] = kbd


def _ssm_kernel(u_ref, xw_ref, yw_ref, cc_ref, t1_ref, t1t_ref, a_ref, d_ref, o_ref,
                win_s, wout_s, m8_s, mx_s, v_scr, h_scr):
    n_chunks = u_ref.shape[0]
    half = (SSM_L // 2) * LANES
    ns = a_ref.shape[1]

    @pl.when(pl.program_id(1) == 0)
    def _():
        _ssm_expand_weights(xw_ref, yw_ref, cc_ref, t1_ref, t1t_ref, win_s, wout_s, m8_s, mx_s)

    uc = u_ref[...]
    v_scr[...] = _dot(uc, win_s[...])
    ar = a_ref[0:1, :]
    ai = a_ref[1:2, :]

    def step(c, carry):
        hr, hi = carry
        h_scr[pl.ds(c, 1), pl.ds(0, ns)] = hr
        h_scr[pl.ds(c, 1), pl.ds(ns, ns)] = hi
        vr = v_scr[pl.ds(c, 1), pl.ds(0, ns)]
        vi = v_scr[pl.ds(c, 1), pl.ds(ns, ns)]
        return ar * hr - ai * hi + vr, ar * hi + ai * hr + vi

    zero = jnp.zeros((1, ns), F32)
    lax.fori_loop(0, n_chunks, step, (zero, zero))
    y = _dot(h_scr[...].astype(BF16), wout_s[...])
    u_lo = uc[:, :half]
    u_hi = uc[:, half:]
    m8 = m8_s[...]
    y_lo = y[:, :half] + _dot(u_lo, m8)
    y_hi = y[:, half:] + _dot(u_hi, m8) + _dot(u_lo, mx_s[...])
    dd = d_ref[...]
    o_ref[:, :half] = _gelu_tanh(y_lo + dd[:, :half] * u_lo.astype(F32)).astype(BF16)
    o_ref[:, half:] = _gelu_tanh(y_hi + dd[:, half:] * u_hi.astype(F32)).astype(BF16)


def _ssm(ucat, xw, yw, cc, t1, t1t, a16, dcat):
    bsz, n_lg, n_chunks, width = ucat.shape
    ns2 = t1.shape[-1]
    half = width // 2

    def per_lg(a):
        return pl.BlockSpec((None,) + a.shape[1:], lambda g, b: (g,) + (0,) * (a.ndim - 1))

    def const(a):
        return pl.BlockSpec(a.shape, lambda g, b: (0, 0))

    return pl.pallas_call(
        _ssm_kernel,
        grid=(n_lg, bsz),
        in_specs=[
            pl.BlockSpec((None, None, n_chunks, width), lambda g, b: (b, g, 0, 0)),
            per_lg(xw), per_lg(yw), per_lg(cc), const(t1), const(t1t), per_lg(a16), per_lg(dcat),
        ],
        out_specs=pl.BlockSpec((None, None, n_chunks, width), lambda g, b: (b, g, 0, 0)),
        out_shape=jax.ShapeDtypeStruct(ucat.shape, BF16),
        scratch_shapes=[
            pltpu.VMEM((width, ns2), BF16), pltpu.VMEM((ns2, width), BF16),
            pltpu.VMEM((half, half), BF16), pltpu.VMEM((half, half), BF16),
            pltpu.VMEM((n_chunks, ns2), F32), pltpu.VMEM((n_chunks, ns2), F32),
        ],
        compiler_params=_cparams(("arbitrary", "arbitrary")),
        name="ssm_scan",
    )(ucat, xw, yw, cc, t1, t1t, a16, dcat)


def _ssm_post_kernel(z_ref, x_ref, w_ref, mod_ref, g_ref, o_ref, zp, rp):
    n_lg, n, _ = z_ref.shape
    d = x_ref.shape[1]
    for s in range(SSM_L):
        for lg in range(n_lg):
            zp[s * n:(s + 1) * n, lg * LANES:(lg + 1) * LANES] = z_ref[lg, :, s * LANES:(s + 1) * LANES]
    gl = _dot(zp[...], w_ref[...])
    h = gl[:, :d] * _sigmoid(gl[:, d:])
    r = mod_ref[2:3, :] * _rms(h, g_ref[1:2, :])
    pitch = n + STRIDE_PAD
    for s in range(SSM_L):
        for lg in range(n_lg):
            rp[lg, s * pitch:s * pitch + n, :] = r[s * n:(s + 1) * n, lg * LANES:(lg + 1) * LANES]
    sub = 8
    for c in range(n):
        for k in range(SSM_L // sub):
            rows = pl.ds(c * SSM_L + k * sub, sub)
            for lg in range(n_lg):
                cols = slice(lg * LANES, (lg + 1) * LANES)
                o_ref[rows, cols] = x_ref[rows, cols] + rp[lg, pl.ds(k * sub * pitch + c, sub, stride=pitch), :]


def _ssm_post(zcat, x, w_glu, mod_l, g_l, tm):
    bsz, seq, d = x.shape
    n_lg = d // LANES
    return pl.pallas_call(
        _ssm_post_kernel,
        grid=(bsz, seq // tm),
        in_specs=[
            pl.BlockSpec((None, n_lg, tm // SSM_L, SSM_L * LANES), lambda b, i: (b, 0, i, 0)),
            pl.BlockSpec((None, tm, d), lambda b, i: (b, i, 0)),
            pl.BlockSpec((d, 2 * d), lambda b, i: (0, 0)),
            pl.BlockSpec((None, 6, d), lambda b, i: (b, 0, 0)),
            pl.BlockSpec((4, d), lambda b, i: (0, 0)),
        ],
        out_specs=pl.BlockSpec((None, tm, d), lambda b, i: (b, i, 0)),
        out_shape=jax.ShapeDtypeStruct(x.shape, F32),
        scratch_shapes=[pltpu.VMEM((tm, d), BF16),
                        pltpu.VMEM((n_lg, SSM_L * (tm // SSM_L + STRIDE_PAD), LANES), F32)],
        compiler_params=_cparams(("parallel", "parallel")),
        name="ssm_post",
    )(zcat, x, w_glu, mod_l, g_l)


def _ssm_weights(a_re, a_im, log_dt, b_re, b_im, c_re, c_im, d_skip):
    n_groups = a_re.shape[0]
    n_lg = n_groups // SSM_LG
    p, hh, ll = SSM_STATE, SSM_GROUP, SSM_L
    dt = jnp.exp(log_dt)[:, None]
    k = jnp.arange(ll + 1, dtype=F32)[:, None, None]
    mag = jnp.exp(k * (a_re * dt))
    pw_re = mag * jnp.cos(k * (a_im * dt))
    pw_im = mag * jnp.sin(k * (a_im * dt))
    den = a_re * a_re + a_im * a_im
    num_re, num_im = pw_re[1] - 1.0, pw_im[1]
    f_re = (num_re * a_re + num_im * a_im) / den
    f_im = (num_im * a_re - num_re * a_im) / den
    bb_re = f_re[..., None] * b_re - f_im[..., None] * b_im
    bb_im = f_re[..., None] * b_im + f_im[..., None] * b_re
    b2_re, b2_im = (x.reshape(n_lg, SSM_LG, p, hh).transpose(0, 2, 1, 3).reshape(n_lg, p, LANES)
                    for x in (bb_re, bb_im))

    def lanes_g(x):
        x = x.reshape(x.shape[0], n_lg, SSM_LG, p).transpose(0, 1, 3, 2)
        return jnp.repeat(x, hh, axis=-1)

    l_re, l_im = lanes_g(pw_re[:ll][::-1]), lanes_g(pw_im[:ll][::-1])
    xw = jnp.concatenate([l_re * b2_re - l_im * b2_im, l_re * b2_im + l_im * b2_re], axis=2)
    xw = xw.transpose(1, 0, 2, 3)
    cp_re = c_re[None] * pw_re[1:, :, None, :] - c_im[None] * pw_im[1:, :, None, :]
    cp_im = c_re[None] * pw_im[1:, :, None, :] + c_im[None] * pw_re[1:, :, None, :]
    yw = jnp.concatenate([cp_re, -cp_im], axis=-1).reshape(ll, n_lg, LANES, 2 * p).transpose(1, 0, 2, 3)
    cc = jnp.concatenate([c_re, -c_im], axis=-1).reshape(n_lg, LANES, 2 * p)
    j1 = jnp.arange(2 * SSM_LG * p)
    t1 = (jnp.arange(2 * p)[:, None] == ((j1 // (SSM_LG * p)) * p + j1 % p)[None, :]).astype(BF16)
    a16 = jnp.stack([pw_re[ll], pw_im[ll]], axis=0).reshape(2, n_lg, SSM_LG * p).transpose(1, 0, 2)
    dcat = jnp.tile(d_skip.reshape(n_lg, 1, LANES), (1, 1, ll))
    return xw.astype(BF16), yw.astype(BF16), cc.astype(BF16), t1, t1.T, a16, dcat


def _ffn_kernel(x_ref, wg_ref, wu_ref, wo_ref, mod_ref, g_ref, o_ref, h_scr, acc):
    k = pl.program_id(2)

    @pl.when(k == 0)
    def _():
        h = _rms(x_ref[...], g_ref[2:3, :]) * (1.0 + mod_ref[4:5, :]) + mod_ref[3:4, :]
        h_scr[...] = h.astype(BF16)
        acc[...] = jnp.zeros_like(acc)

    h = h_scr[...]
    g = _dot(h, wg_ref[...])
    u = _dot(h, wu_ref[...])
    act = (g * _sigmoid(g) * u).astype(BF16)
    acc[...] += _dot(act, wo_ref[...])

    @pl.when(k == pl.num_programs(2) - 1)
    def _():
        o_ref[...] = x_ref[...] + mod_ref[5:6, :] * _rms(acc[...], g_ref[3:4, :])


def _ffn(x, w_in, w_out, mod_l, g_l, tm, tf):
    bsz, seq, d = x.shape
    f = w_out.shape[0]
    nk = f // tf
    return pl.pallas_call(
        _ffn_kernel,
        grid=(bsz, seq // tm, nk),
        in_specs=[
            pl.BlockSpec((None, tm, d), lambda b, i, k: (b, i, 0)),
            pl.BlockSpec((d, tf), lambda b, i, k: (0, k)),
            pl.BlockSpec((d, tf), lambda b, i, k: (0, k + nk)),
            pl.BlockSpec((tf, d), lambda b, i, k: (k, 0)),
            pl.BlockSpec((None, 6, d), lambda b, i, k: (b, 0, 0)),
            pl.BlockSpec((4, d), lambda b, i, k: (0, 0)),
        ],
        out_specs=pl.BlockSpec((None, tm, d), lambda b, i, k: (b, i, 0)),
        out_shape=jax.ShapeDtypeStruct(x.shape, F32),
        scratch_shapes=[pltpu.VMEM((tm, d), BF16), pltpu.VMEM((tm, d), F32)],
        compiler_params=_cparams(("parallel", "parallel", "arbitrary")),
        name="ffn",
    )(x, w_in, w_in, w_out, mod_l, g_l)


def _rope_kernel(pos_ref, f_ref, cos_ref, sin_ref):
    ang = pos_ref[...].astype(F32) * f_ref[...]
    cos_ref[...] = jnp.cos(ang)
    sin_ref[...] = jnp.sin(ang)


def _rope_tables(positions):
    bsz, seq = positions.shape
    half = MLA_ROPE // 2
    inv_freq = ROPE_THETA ** (-jnp.arange(half, dtype=F32) / half)
    per_row = LANES // half
    rows = bsz * seq // per_row
    pos_rep = jnp.broadcast_to(positions[..., None], (bsz, seq, half)).reshape(rows, LANES)
    f_rep = jnp.tile(inv_freq, per_row).reshape(1, LANES)
    tr = min(rows, 512)
    cos, sin = pl.pallas_call(
        _rope_kernel,
        grid=(rows // tr,),
        in_specs=[pl.BlockSpec((tr, LANES), lambda i: (i, 0)), pl.BlockSpec((1, LANES), lambda i: (0, 0))],
        out_specs=[pl.BlockSpec((tr, LANES), lambda i: (i, 0))] * 2,
        out_shape=[jax.ShapeDtypeStruct((rows, LANES), F32)] * 2,
        compiler_params=_cparams(("parallel",)),
        name="rope_tables",
    )(pos_rep, f_rep)
    cos = cos.reshape(bsz, seq, half)
    sin = sin.reshape(bsz, seq, half)
    ones = jnp.ones((bsz, seq, MLA_NOPE), F32)
    zpad = jnp.zeros((bsz, seq, LANES - MLA_QK), F32)
    z16 = jnp.zeros((bsz, seq, half), F32)
    cos_t = jnp.concatenate([ones, cos, cos, zpad], axis=-1)
    sin_lo = jnp.concatenate([0.0 * ones, -sin, z16, zpad], axis=-1)
    sin_hi = jnp.concatenate([0.0 * ones, z16, sin, zpad], axis=-1)
    return cos_t, sin_lo, sin_hi


def _mla_proj_kernel(x_ref, cos_ref, sin_lo_ref, sin_hi_ref, mod_ref, g_ref, wcq_ref, wckv_ref, wkr_ref,
                     qn_ref, kvn_ref, wq_ref, wk_ref, wv_ref, exp_ref, q_ref, k_ref, v_ref):
    h = (_rms(x_ref[...], g_ref[0:1, :]) * (1.0 + mod_ref[1:2, :]) + mod_ref[0:1, :]).astype(BF16)
    cq = _rms(_dot(h, wcq_ref[...]), qn_ref[...]).astype(BF16)
    ckv = _rms(_dot(h, wckv_ref[...]), kvn_ref[...]).astype(BF16)
    kr = _dot(h, wkr_ref[...])
    cos = cos_ref[...]
    sin_lo = sin_lo_ref[...]
    sin_hi = sin_hi_ref[...]
    half = MLA_ROPE // 2

    def rope(t):
        width = t.shape[1]
        up = pltpu.roll(t, half, axis=1)
        down = pltpu.roll(t, width - half, axis=1)
        return [t[:, s:s + LANES] * cos + down[:, s:s + LANES] * sin_lo + up[:, s:s + LANES] * sin_hi
                for s in range(0, width, LANES)]

    q = _dot(cq, wq_ref[...])
    for hd, q_hd in enumerate(rope(q)):
        q_ref[:, hd * LANES:(hd + 1) * LANES] = q_hd.astype(BF16)
    kr_rot = rope(kr)[0].astype(BF16)
    k_ref[...] = (_dot(ckv, wk_ref[...]) + _dot(kr_rot, exp_ref[...])).astype(BF16)
    vt = lax.dot_general(wv_ref[...], ckv, (((1,), (1,)), ((), ())), preferred_element_type=F32)
    row = lax.broadcasted_iota(jnp.int32, vt.shape, 0)
    v_ref[...] = jnp.where(row % V_ROWS == MLA_V, 1.0, vt).astype(BF16)


def _mla_proj(x, rope_t, mod_l, g_l, wts, tm):
    bsz, seq, d = x.shape
    hw = MLA_HEADS * LANES

    def full(a):
        return pl.BlockSpec(a.shape, lambda b, i: (0,) * a.ndim)

    out_sds = jax.ShapeDtypeStruct((bsz, seq, hw), BF16)
    vt_rows = MLA_HEADS * V_ROWS
    vt_sds = jax.ShapeDtypeStruct((bsz, seq // tm, vt_rows, tm), BF16)
    return pl.pallas_call(
        _mla_proj_kernel,
        grid=(bsz, seq // tm),
        in_specs=[
            pl.BlockSpec((None, tm, d), lambda b, i: (b, i, 0)),
            pl.BlockSpec((None, tm, LANES), lambda b, i: (b, i, 0)),
            pl.BlockSpec((None, tm, LANES), lambda b, i: (b, i, 0)),
            pl.BlockSpec((None, tm, LANES), lambda b, i: (b, i, 0)),
            pl.BlockSpec((None, 6, d), lambda b, i: (b, 0, 0)),
            pl.BlockSpec((4, d), lambda b, i: (0, 0)),
        ] + [full(w) for w in wts],
        out_specs=[pl.BlockSpec((None, tm, hw), lambda b, i: (b, i, 0))] * 2
        + [pl.BlockSpec((None, None, vt_rows, tm), lambda b, i: (b, i, 0, 0))],
        out_shape=[out_sds, out_sds, vt_sds],
        compiler_params=_cparams(("parallel", "parallel")),
        name="mla_proj",
    )(x, *rope_t, mod_l, g_l, *wts)


def _mla_weights(w_in, q_norm, kv_norm, w_uq, w_ukv, w_o):
    pad = LANES - MLA_QK
    w_cq = w_in[:, :MLA_Q_RANK]
    w_ckv = w_in[:, MLA_Q_RANK:MLA_Q_RANK + MLA_KV_RANK]
    w_kr = jnp.pad(w_in[:, MLA_Q_RANK + MLA_KV_RANK:], ((0, 0), (MLA_NOPE, pad)))
    scale = MLA_QK ** -0.5 * math.log2(math.e)
    wq = w_uq.reshape(MLA_Q_RANK, MLA_HEADS, MLA_QK) * scale
    wq_pad = jnp.pad(wq, ((0, 0), (0, 0), (0, pad))).reshape(MLA_Q_RANK, MLA_HEADS * LANES)
    wkv =w_ukv.reshape(MLA_KV_RANK, MLA_HEADS, MLA_NOPE + MLA_V)
    zk = jnp.zeros((MLA_KV_RANK, MLA_HEADS, LANES - MLA_NOPE), F32)
    wk_pad = jnp.concatenate([wkv[..., :MLA_NOPE], zk], axis=-1).reshape(MLA_KV_RANK, MLA_HEADS * LANES)
    wv_t = jnp.pad(wkv[..., MLA_NOPE:].transpose(1, 2, 0), ((0, 0), (0, V_ROWS - MLA_V), (0, 0)))
    wv_t = wv_t.reshape(MLA_HEADS * V_ROWS, MLA_KV_RANK)
    lane = jnp.arange(LANES)
    is_rope = (lane >= MLA_NOPE) & (lane < MLA_QK)
    expand = jnp.tile((jnp.eye(LANES, dtype=F32) * is_rope[None, :].astype(F32)), (1, MLA_HEADS))
    proj_w = (w_cq.astype(BF16), w_ckv.astype(BF16), w_kr.astype(BF16),
              q_norm.reshape(1, -1), kv_norm.reshape(1, -1),
              wq_pad.astype(BF16), wk_pad.astype(BF16), wv_t.astype(BF16), expand.astype(BF16))
    return proj_w, w_o.astype(BF16)


def _attn_kernel(q_ref, k_ref, vt_ref, o_ref, m_scr, mt_scr, acc, s0_scr, s1_scr, qt_scr):
    tk = vt_ref.shape[2]
    qi = pl.program_id(2)
    s_scr = (s0_scr, s1_scr)
    qt_scr[...] = q_ref[...].astype(F32).T.astype(BF16)
    m_scr[...] = jnp.full_like(m_scr, -jnp.inf)
    acc[...] = jnp.zeros_like(acc)

    def scores(hd, j):
        lanes = slice(hd * LANES, (hd + 1) * LANES)
        k = k_ref[pl.ds(pl.multiple_of(j * tk, tk), tk), lanes]
        st = _dot(k, qt_scr[lanes, :])
        s_scr[hd][...] = st
        mt_scr[hd] = jnp.max(st, axis=0, keepdims=True)

    def accumulate(hd, j, masked):
        st = s_scr[hd][...]
        if masked:
            key = lax.broadcasted_iota(jnp.int32, st.shape, 0)
            qry = lax.broadcasted_iota(jnp.int32, st.shape, 1)
            st = jnp.where(key <= qry, st, -jnp.inf)
            mt = jnp.max(st, axis=0, keepdims=True)
        else:
            mt = mt_scr[hd]
        m_prev = m_scr[hd]
        m_new = jnp.maximum(m_prev, mt)
        alpha = jnp.exp2(m_prev - m_new)
        pt = jnp.exp2(st - m_new).astype(BF16)
        vt = vt_ref[j, hd * V_ROWS:(hd + 1) * V_ROWS, :]
        acc[hd] = alpha * acc[hd] + _dot(vt, pt)
        m_scr[hd] = m_new

    scores(0, 0)

    def body(j, carry):
        scores(1, j)
        accumulate(0, j, False)
        scores(0, j + 1)
        accumulate(1, j, False)
        return carry

    def body_pair(jj, carry):
        body(2 * jj, carry)
        return body(2 * jj + 1, carry)

    lax.fori_loop(0, qi // 2, body_pair, 0)

    @pl.when(qi % 2 == 1)
    def _():
        body(qi - 1, 0)

    scores(1, qi)
    accumulate(0, qi, True)
    accumulate(1, qi, True)
    outs = []
    for hd in range(2):
        a = acc[hd]
        outs.append(a[:MLA_V, :] / a[MLA_V:MLA_V + 1, :])
    o_ref[...] = jnp.concatenate(outs, axis=0).T.astype(BF16)


def _attention(q, k, vt, tq):
    bsz, seq, hw = q.shape
    n_pairs = hw // (2 * LANES)
    n_kt, _, tk = vt.shape[1:]
    assert tq == tk, "the causal diagonal tile assumes square tiles"
    return pl.pallas_call(
        _attn_kernel,
        grid=(bsz, n_pairs, seq // tq),
        in_specs=[
            pl.BlockSpec((None, tq, 2 * LANES), lambda b, h, i: (b, i, h)),
            pl.BlockSpec((None, seq, 2 * LANES), lambda b, h, i: (b, 0, h)),
            pl.BlockSpec((None, n_kt, 2 * V_ROWS, tk), lambda b, h, i: (b, 0, h, 0)),
        ],
        out_specs=pl.BlockSpec((None, tq, 2 * MLA_V), lambda b, h, i: (b, i, h)),
        out_shape=jax.ShapeDtypeStruct((bsz, seq, n_pairs * 2 * MLA_V), BF16),
        scratch_shapes=[pltpu.VMEM((2, 1, tq), F32), pltpu.VMEM((2, 1, tq), F32), pltpu.VMEM((2, V_ROWS, tq), F32),
                        pltpu.VMEM((tk, tq), F32), pltpu.VMEM((tk, tq), F32), pltpu.VMEM((2 * LANES, tq), BF16)],
        compiler_params=_cparams(("parallel", "parallel", "arbitrary")),
        name="mla_attention",
    )(q, k, vt)


def _attn_out_kernel(a_ref, x_ref, w_ref, mod_ref, g_ref, o_ref):
    h = _dot(a_ref[...], w_ref[...])
    o_ref[...] = x_ref[...] + mod_ref[2:3, :] * _rms(h, g_ref[1:2, :])


def _attn_out(a, x, wo_pad, mod_l, g_l, tm):
    bsz, seq, d = x.shape
    hw = a.shape[-1]
    return pl.pallas_call(
        _attn_out_kernel,
        grid=(bsz, seq // tm),
        in_specs=[
            pl.BlockSpec((None, tm, hw), lambda b, i: (b, i, 0)),
            pl.BlockSpec((None, tm, d), lambda b, i: (b, i, 0)),
            pl.BlockSpec((hw, d), lambda b, i: (0, 0)),
            pl.BlockSpec((None, 6, d), lambda b, i: (b, 0, 0)),
            pl.BlockSpec((4, d), lambda b, i: (0, 0)),
        ],
        out_specs=pl.BlockSpec((None, tm, d), lambda b, i: (b, i, 0)),
        out_shape=jax.ShapeDtypeStruct(x.shape, F32),
        compiler_params=_cparams(("parallel", "parallel")),
        name="mla_out",
    )(a, x, wo_pad, mod_l, g_l)


def _route(logits):
    row = lax.broadcasted_iota(jnp.int32, logits.shape, 0)
    n = logits.shape[0]
    neg = -jnp.inf
    m1 = jnp.max(logits, axis=0, keepdims=True)
    i1 = jnp.min(jnp.where(logits == m1, row, n), axis=0, keepdims=True)
    rest = jnp.where(row == i1, neg, logits)
    m2 = jnp.max(rest, axis=0, keepdims=True)
    i2 = jnp.min(jnp.where(rest == m2, row, n), axis=0, keepdims=True)
    e = jnp.exp(m2 - m1)
    w1 = 1.0 / (1.0 + e)
    w2 = e / (1.0 + e)
    first = row == i1
    second = row == i2
    comb = jnp.where(first, w1, 0.0) + jnp.where(second, w2, 0.0)
    sel = jnp.where(first, 1.0, 0.0) + jnp.where(second, 1.0, 0.0)
    return comb, sel


def _moe_kernel(x_ref, wr_ref, br_ref, tri_ref, wg_ref, wu_ref, wo_ref, mod_ref, g_ref, o_ref,
                h_scr, sel_scr, rank_scr, w_scr, acc):
    e = pl.program_id(1)
    tb = x_ref.shape[0]

    @pl.when(e == 0)
    def _():
        h = _rms(x_ref[...], g_ref[2:3, :]) * (1.0 + mod_ref[4:5, :]) + mod_ref[3:4, :]
        h_hi = h.astype(BF16)
        h_lo = (h - h_hi.astype(F32)).astype(BF16)
        nt = (((1,), (1,)), ((), ()))
        logits = (lax.dot_general(wr_ref[0], h_hi, nt, preferred_element_type=F32)
                  + lax.dot_general(wr_ref[1], h_hi, nt, preferred_element_type=F32)
                  + lax.dot_general(wr_ref[0], h_lo, nt, preferred_element_type=F32))
        n_e = sel_scr.shape[0]
        comb, sel = _route(logits[:n_e, :] + br_ref[:n_e, 0:1])
        sel_scr[...] = sel
        w_scr[...] = comb
        sel_pad = jnp.concatenate([sel, jnp.zeros_like(sel)], axis=0).astype(BF16)
        rank_scr[...] = _dot(sel_pad, tri_ref[...])[:n_e, :]
        h_scr[...] = h_hi
        acc[...] = jnp.zeros_like(acc)

    sel_row = sel_scr[pl.ds(e, 1), :]
    rank_row = rank_scr[pl.ds(e, 1), :]
    w_row = w_scr[pl.ds(e, 1), :]
    n_rows = jnp.sum(sel_row).astype(jnp.int32)

    def step(r, carry):
        slot = (lax.broadcasted_iota(jnp.int32, (MOE_SUB, tb), 0) + r * MOE_SUB).astype(F32)
        hit = rank_row == slot
        onehot = jnp.where(hit, sel_row, 0.0).astype(BF16)
        xs = _dot(onehot, h_scr[...]).astype(BF16)
        g = _dot(xs, wg_ref[...])
        u = _dot(xs, wu_ref[...])
        act = (g * _sigmoid(g) * u).astype(BF16)
        y = _dot(act, wo_ref[...])
        w_col = jnp.sum(jnp.where(hit, w_row, 0.0), axis=1, keepdims=True)
        yw = (y * w_col).astype(BF16)
        acc[...] += lax.dot_general(onehot, yw, (((0,), (0,)), ((), ())), preferred_element_type=F32)
        return carry

    lax.fori_loop(0, (n_rows + MOE_SUB - 1) // MOE_SUB, step, 0)

    @pl.when(e == pl.num_programs(1) - 1)
    def _():
        o_ref[...] = x_ref[...] + mod_ref[5:6, :] * _rms(acc[...], g_ref[3:4, :])


def _moe_layer(x, w_router, b_router, w_in, w_out, mod_l, g_l):
    bsz, seq, d = x.shape
    n_e, f, _ = w_out.shape
    tb = min(MOE_TB, seq)
    blocks_per_batch = seq // tb
    wr = jnp.pad(w_router.T, ((0, LANES - n_e), (0, 0)))
    wr_hi = wr.astype(BF16)
    wr = jnp.stack([wr_hi, (wr - wr_hi.astype(F32)).astype(BF16)])
    br = jnp.broadcast_to(jnp.pad(b_router, (0, LANES - n_e))[:, None], (LANES, LANES))
    pos = jnp.arange(tb)
    tri = (pos[:, None] < pos[None, :]).astype(BF16)
    w_in = w_in.astype(BF16)
    out = pl.pallas_call(
        _moe_kernel,
        grid=(bsz * blocks_per_batch, n_e),
        in_specs=[
            pl.BlockSpec((tb, d), lambda i, e: (i, 0)),
            pl.BlockSpec((2, LANES, d), lambda i, e: (0, 0, 0)),
            pl.BlockSpec((LANES, LANES), lambda i, e: (0, 0)),
            pl.BlockSpec((tb, tb), lambda i, e: (0, 0)),
            pl.BlockSpec((None, d, f), lambda i, e: (e, 0, 0)),
            pl.BlockSpec((None, d, f), lambda i, e: (e, 0, 1)),
            pl.BlockSpec((None, f, d), lambda i, e: (e, 0, 0)),
            pl.BlockSpec((None, 6, d), lambda i, e: (i // blocks_per_batch, 0, 0)),
            pl.BlockSpec((4, d), lambda i, e: (0, 0)),
        ],
        out_specs=pl.BlockSpec((tb, d), lambda i, e: (i, 0)),
        out_shape=jax.ShapeDtypeStruct((bsz * seq, d), F32),
        scratch_shapes=[pltpu.VMEM((tb, d), BF16), pltpu.VMEM((n_e, tb), F32), pltpu.VMEM((n_e, tb), F32),
                        pltpu.VMEM((n_e, tb), F32), pltpu.VMEM((tb, d), F32)],
        compiler_params=_cparams(("parallel", "arbitrary")),
        name="moe",
    )(x.reshape(bsz * seq, d), wr, br, tri, w_in, w_in, w_out.astype(BF16), mod_l, g_l)
    return out.reshape(bsz, seq, d)


def kernel(x, c, positions, norm_g, w_ada, b_ada, ssm_a_re, ssm_a_im, ssm_log_dt, ssm_b_re, ssm_b_im, ssm_c_re, ssm_c_im, ssm_d, ssm_w_glu, ffn_w_in, ffn_w_out, mla_w_in, mla_q_norm, mla_kv_norm, mla_w_uq, mla_w_ukv, mla_w_o, moe_w_router, moe_b_router, moe_w_in, moe_w_out):
    depth = norm_g.shape[0]
    seq = x.shape[1]
    tm = min(512, seq)
    mod = _ada(c, w_ada, b_ada)
    rope_t = None
    for i in range(depth):
        j = i // 2
        mod_l, g_l = mod[i], norm_g[i]
        if i % 2 == 0:
            ssm_w = _ssm_weights(ssm_a_re[j], ssm_a_im[j], ssm_log_dt[j], ssm_b_re[j], ssm_b_im[j],
                                 ssm_c_re[j], ssm_c_im[j], ssm_d[j])
            ucat = _ssm_pre(x, mod_l, g_l, tm)
            zcat = _ssm(ucat, *ssm_w)
            x = _ssm_post(zcat, x, ssm_w_glu[j].astype(BF16), mod_l, g_l, tm)
            f = ffn_w_out.shape[1]
            x = _ffn(x, ffn_w_in[j].astype(BF16), ffn_w_out[j].astype(BF16), mod_l, g_l, tm, f // 2)
        else:
            if rope_t is None:
                rope_t = _rope_tables(positions)
            proj_w, wo_pad = _mla_weights(mla_w_in[j], mla_q_norm[j], mla_kv_norm[j],
                                          mla_w_uq[j], mla_w_ukv[j], mla_w_o[j])
            q, k, v = _mla_proj(x, rope_t, mod_l, g_l, proj_w, tm)
            a = _attention(q, k, v, tm)
            x = _attn_out(a, x, wo_pad, mod_l, g_l, tm)
            x = _moe_layer(x, moe_w_router[j], moe_b_router[j], moe_w_in[j], moe_w_out[j], mod_l, g_l)
    return x
```

```python
import functools
import math

import jax
import jax.numpy as jnp
from jax import lax
from jax.experimental import pallas as pl
from jax.experimental.pallas import tpu as pltpu

F32 = jnp.float32
BF16 = jnp.bfloat16

RMS_EPS = 1e-6
LANES = 128
SSM_GROUP = 16
SSM_STATE = 64
SSM_L = 16
SSM_LG = LANES // SSM_GROUP
STRIDE_PAD = 4
MLA_HEADS = 16
MLA_NOPE = 64
MLA_ROPE = 32
MLA_V = 64
MLA_QK = MLA_NOPE + MLA_ROPE
V_ROWS = 80
MLA_Q_RANK = 384
MLA_KV_RANK = 256
ROPE_THETA = 10000.0
N_EXPERTS = 8
MOE_TB = 1024
MOE_SUB = 144
VMEM_LIMIT = 56 * 1024 * 1024


def _cparams(sem):
    return pltpu.CompilerParams(dimension_semantics=sem, vmem_limit_bytes=VMEM_LIMIT)


def _rms(x, g):
    return x * lax.rsqrt(jnp.mean(x * x, axis=-1, keepdims=True) + RMS_EPS) * g


def _sigmoid(x):
    return 1.0 / (1.0 + jnp.exp(-x))


def _dot(a, b):
    return jnp.dot(a, b, preferred_element_type=F32)


def _ada_kernel(c_ref, w_ref, b_ref, o_ref):
    c = c_ref[...]
    c_act = c * _sigmoid(c)
    c_hi = c_act.astype(BF16)
    c_lo = (c_act - c_hi.astype(F32)).astype(BF16)
    w = w_ref[...]
    w_hi = w.astype(BF16)
    w_lo = (w - w_hi.astype(F32)).astype(BF16)
    o_ref[...] = _dot(c_hi, w_hi) + _dot(c_hi, w_lo) + _dot(c_lo, w_hi) + b_ref[...]


def _ada(c, w_ada, b_ada):
    depth, d, _ = w_ada.shape
    bsz = c.shape[0]
    rows = 16
    c_pad =jnp.pad(c, ((0, rows - bsz), (0, 0)))
    out = pl.pallas_call(
        _ada_kernel,
        grid=(depth, 6),
        in_specs=[
            pl.BlockSpec((rows, d), lambda i, j: (0, 0)),
            pl.BlockSpec((None, d, d), lambda i, j: (i, 0, j)),
            pl.BlockSpec((None, None, 1, d), lambda i, j: (i, j, 0, 0)),
        ],
        out_specs=pl.BlockSpec((None, None, rows, d), lambda i, j: (i, j, 0, 0)),
        out_shape=jax.ShapeDtypeStruct((depth, 6, rows, d), F32),
        compiler_params=_cparams(("arbitrary", "arbitrary")),
        name="ada_mod",
    )(c_pad, w_ada, b_ada.reshape(depth, 6, 1, d))
    return out[:, :, :bsz, :].transpose(0, 2, 1, 3)


def _ssm_pre_kernel(x_ref, mod_ref, g_ref, o_ref, scr):
    tm = x_ref.shape[0]
    n = tm // SSM_L
    u = _rms(x_ref[...], g_ref[0:1, :]) * (1.0 + mod_ref[1:2, :]) + mod_ref[0:1, :]
    n_lg = o_ref.shape[0]
    pitch = SSM_L + STRIDE_PAD
    for c in range(n):
        for lg in range(n_lg):
            scr[lg, c * pitch:c * pitch + SSM_L, :] = u[c * SSM_L:(c + 1) * SSM_L, lg * LANES:(lg + 1) * LANES]
    for s in range(SSM_L):
        for lg in range(n_lg):
            o_ref[lg, :, s * LANES:(s + 1) * LANES] = scr[lg, pl.ds(s, n, stride=pitch), :].astype(BF16)


def _ssm_pre(x, mod_l, g_l, tm):
    bsz, seq, d = x.shape
    n_lg = d // LANES
    return pl.pallas_call(
        _ssm_pre_kernel,
        grid=(bsz, seq // tm),
        in_specs=[
            pl.BlockSpec((None, tm, d), lambda b, i: (b, i, 0)),
            pl.BlockSpec((None, 6, d), lambda b, i: (b, 0, 0)),
            pl.BlockSpec((4, d), lambda b, i: (0, 0)),
        ],
        out_specs=pl.BlockSpec((None, n_lg, tm // SSM_L, SSM_L * LANES), lambda b, i: (b, 0, i, 0)),
        out_shape=jax.ShapeDtypeStruct((bsz, n_lg, seq // SSM_L, SSM_L * LANES), BF16),
        scratch_shapes=[pltpu.VMEM((n_lg, (tm // SSM_L) * (SSM_L + STRIDE_PAD), LANES), F32)],
        compiler_params=_cparams(("parallel", "parallel")),
        name="ssm_pre",
    )(x, mod_l, g_l)


def _gelu_tanh(y):
    return 0.5 * y * (1.0 + jnp.tanh(math.sqrt(2.0 / math.pi) * (y + 0.044715 * (y * y * y))))


def _same_group(shape, row_shift, col_shift):
    row_g = (lax.broadcasted_iota(jnp.int32, shape, 0) >> row_shift) & (SSM_LG - 1)
    col_g = (lax.broadcasted_iota(jnp.int32, shape, 1) >> col_shift) & (SSM_LG - 1)
    return row_g == col_g


def _ssm_expand_weights(xw_ref, yw_ref, cc_ref, t1_ref, t1t_ref, win_s, wout_s, m8_s, mx_s):
    hl = SSM_L // 2
    in_mask = _same_group((LANES, t1_ref.shape[1]), 4, 6)
    out_mask = _same_group((t1_ref.shape[1], LANES), 6, 4)
    k_mask = _same_group((LANES, LANES), 4, 4)
    m8_s[...] = jnp.zeros_like(m8_s)
    for s in range(SSM_L):
        rows = slice(s * LANES, (s + 1) * LANES)
        val = lax.dot_general(xw_ref[s], t1_ref[...], (((0,), (0,)), ((), ())), preferred_element_type=F32)
        win_s[rows, :] = jnp.where(in_mask, val, 0.0).astype(BF16)
        val = lax.dot_general(t1t_ref[...], yw_ref[s], (((1,), (1,)), ((), ())), preferred_element_type=F32)
        wout_s[:, rows] = jnp.where(out_mask, val, 0.0).astype(BF16)
    for tau in range(SSM_L):
        k_tau = lax.dot_general(xw_ref[SSM_L - 1 - tau], cc_ref[...], (((0,), (1,)), ((), ())),
                                preferred_element_type=F32)
        kbd = jnp.where(k_mask, k_tau, 0.0).astype(BF16)
        for a in range(hl):
            if a + tau < hl:
                m8_s[a * LANES:(a + 1) * LANES, (a + tau) * LANES:(a + tau + 1) * LANES] = kbd
            b = a + tau - hl
            if 0 <= b < hl:
                mx_s[a * LANES:(a + 1) * LANES, b * LANES:(b + 1) * LANES] = kbd


def _ssm_kernel(u_ref, xw_ref, yw_ref, cc_ref, t1_ref, t1t_ref, a_ref, d_ref, o_ref,
                win_s, wout_s, m8_s, mx_s, v_scr, h_scr):
    n_chunks = u_ref.shape[0]
    half = (SSM_L // 2) * LANES
    ns = a_ref.shape[1]

    @pl.when(pl.program_id(1) == 0)
    def _():
        _ssm_expand_weights(xw_ref, yw_ref, cc_ref, t1_ref, t1t_ref, win_s, wout_s, m8_s, mx_s)

    uc = u_ref[...]
    v_scr[...] = _dot(uc, win_s[...])
    ar = a_ref[0:1, :]
    ai = a_ref[1:2, :]

    def step(c, carry):
        hr, hi = carry
        h_scr[pl.ds(c, 1), pl.ds(0, ns)] = hr
        h_scr[pl.ds(c, 1), pl.ds(ns, ns)] = hi
        vr = v_scr[pl.ds(c, 1), pl.ds(0, ns)]
        vi = v_scr[pl.ds(c, 1), pl.ds(ns, ns)]
        return ar * hr - ai * hi + vr, ar * hi + ai * hr + vi

    zero = jnp.zeros((1, ns), F32)
    lax.fori_loop(0, n_chunks, step, (zero, zero))
    y = _dot(h_scr[...].astype(BF16), wout_s[...])
    u_lo = uc[:, :half]
    u_hi = uc[:, half:]
    m8 = m8_s[...]
    y_lo = y[:, :half] + _dot(u_lo, m8)
    y_hi = y[:, half:] + _dot(u_hi, m8) + _dot(u_lo, mx_s[...])
    dd = d_ref[...]
    o_ref[:, :half] = _gelu_tanh(y_lo + dd[:, :half] * u_lo.astype(F32)).astype(BF16)
    o_ref[:, half:] = _gelu_tanh(y_hi + dd[:, half:] * u_hi.astype(F32)).astype(BF16)


def _ssm(ucat, xw, yw, cc, t1, t1t, a16, dcat):
    bsz, n_lg, n_chunks, width = ucat.shape
    ns2 = t1.shape[-1]
    half = width // 2

    def per_lg(a):
        return pl.BlockSpec((None,) + a.shape[1:], lambda g, b: (g,) + (0,) * (a.ndim - 1))

    def const(a):
        return pl.BlockSpec(a.shape, lambda g, b: (0, 0))

    return pl.pallas_call(
        _ssm_kernel,
        grid=(n_lg, bsz),
        in_specs=[
            pl.BlockSpec((None, None, n_chunks, width), lambda g, b: (b, g, 0, 0)),
            per_lg(xw), per_lg(yw), per_lg(cc), const(t1), const(t1t), per_lg(a16), per_lg(dcat),
        ],
        out_specs=pl.BlockSpec((None, None, n_chunks, width), lambda g, b: (b, g, 0, 0)),
        out_shape=jax.ShapeDtypeStruct(ucat.shape, BF16),
        scratch_shapes=[
            pltpu.VMEM((width, ns2), BF16), pltpu.VMEM((ns2, width), BF16),
            pltpu.VMEM((half, half), BF16), pltpu.VMEM((half, half), BF16),
            pltpu.VMEM((n_chunks, ns2), F32), pltpu.VMEM((n_chunks, ns2), F32),
        ],
        compiler_params=_cparams(("arbitrary", "arbitrary")),
        name="ssm_scan",
    )(ucat, xw, yw, cc, t1, t1t, a16, dcat)


def _ssm_post_kernel(z_ref, x_ref, w_ref, mod_ref, g_ref, o_ref, zp, rp):
    n_lg, n, _ = z_ref.shape
    d = x_ref.shape[1]
    for s in range(SSM_L):
        for lg in range(n_lg):
            zp[s * n:(s + 1) * n, lg * LANES:(lg + 1) * LANES] = z_ref[lg, :, s * LANES:(s + 1) * LANES]
    gl = _dot(zp[...], w_ref[...])
    h = gl[:, :d] * _sigmoid(gl[:, d:])
    r = mod_ref[2:3, :] * _rms(h, g_ref[1:2, :])
    pitch = n + STRIDE_PAD
    for s in range(SSM_L):
        for lg in range(n_lg):
            rp[lg, s * pitch:s * pitch + n, :] = r[s * n:(s + 1) * n, lg * LANES:(lg + 1) * LANES]
    sub = 8
    for c in range(n):
        for k in range(SSM_L // sub):
            rows = pl.ds(c * SSM_L + k * sub, sub)
            for lg in range(n_lg):
                cols = slice(lg * LANES, (lg + 1) * LANES)
                o_ref[rows, cols] = x_ref[rows, cols] + rp[lg, pl.ds(k * sub * pitch + c, sub, stride=pitch), :]


def _ssm_post(zcat, x, w_glu, mod_l, g_l, tm):
    bsz, seq, d = x.shape
    n_lg = d // LANES
    return pl.pallas_call(
        _ssm_post_kernel,
        grid=(bsz, seq // tm),
        in_specs=[
            pl.BlockSpec((None, n_lg, tm // SSM_L, SSM_L * LANES), lambda b, i: (b, 0, i, 0)),
            pl.BlockSpec((None, tm, d), lambda b, i: (b, i, 0)),
            pl.BlockSpec((d, 2 * d), lambda b, i: (0, 0)),
            pl.BlockSpec((None, 6, d), lambda b, i: (b, 0, 0)),
            pl.BlockSpec((4, d), lambda b, i: (0, 0)),
        ],
        out_specs=pl.BlockSpec((None, tm, d), lambda b, i: (b, i, 0)),
        out_shape=jax.ShapeDtypeStruct(x.shape, F32),
        scratch_shapes=[pltpu.VMEM((tm, d), BF16),
                        pltpu.VMEM((n_lg, SSM_L * (tm // SSM_L + STRIDE_PAD), LANES), F32)],
        compiler_params=_cparams(("parallel", "parallel")),
        name="ssm_post",
    )(zcat, x, w_glu, mod_l, g_l)


def _ssm_weights(a_re, a_im, log_dt, b_re, b_im, c_re, c_im, d_skip):
    n_groups = a_re.shape[0]
    n_lg = n_groups // SSM_LG
    p, hh, ll = SSM_STATE, SSM_GROUP, SSM_L
    dt = jnp.exp(log_dt)[:, None]
    k = jnp.arange(ll + 1, dtype=F32)[:, None, None]
    mag = jnp.exp(k * (a_re * dt))
    pw_re = mag * jnp.cos(k * (a_im * dt))
    pw_im = mag * jnp.sin(k * (a_im * dt))
    den = a_re * a_re + a_im * a_im
    num_re, num_im = pw_re[1] - 1.0, pw_im[1]
    f_re = (num_re * a_re + num_im * a_im) / den
    f_im = (num_im * a_re - num_re * a_im) / den
    bb_re = f_re[..., None] * b_re - f_im[..., None] * b_im
    bb_im = f_re[..., None] * b_im + f_im[..., None] * b_re
    b2_re, b2_im = (x.reshape(n_lg, SSM_LG, p, hh).transpose(0, 2, 1, 3).reshape(n_lg, p, LANES)
                    for x in (bb_re, bb_im))

    def lanes_g(x):
        x = x.reshape(x.shape[0], n_lg, SSM_LG, p).transpose(0, 1, 3, 2)
        return jnp.repeat(x, hh, axis=-1)

    l_re, l_im = lanes_g(pw_re[:ll][::-1]), lanes_g(pw_im[:ll][::-1])
    xw = jnp.concatenate([l_re * b2_re - l_im * b2_im, l_re * b2_im + l_im * b2_re], axis=2)
    xw = xw.transpose(1, 0, 2, 3)
    cp_re = c_re[None] * pw_re[1:, :, None, :] - c_im[None] * pw_im[1:, :, None, :]
    cp_im = c_re[None] * pw_im[1:, :, None, :] + c_im[None] * pw_re[1:, :, None, :]
    yw = jnp.concatenate([cp_re, -cp_im], axis=-1).reshape(ll, n_lg, LANES, 2 * p).transpose(1, 0, 2, 3)
    cc = jnp.concatenate([c_re, -c_im], axis=-1).reshape(n_lg, LANES, 2 * p)
    j1 = jnp.arange(2 * SSM_LG * p)
    t1 = (jnp.arange(2 * p)[:, None] == ((j1 // (SSM_LG * p)) * p + j1 % p)[None, :]).astype(BF16)
    a16 = jnp.stack([pw_re[ll], pw_im[ll]], axis=0).reshape(2, n_lg, SSM_LG * p).transpose(1, 0, 2)
    dcat = jnp.tile(d_skip.reshape(n_lg, 1, LANES), (1, 1, ll))
    return xw.astype(BF16), yw.astype(BF16), cc.astype(BF16), t1, t1.T, a16, dcat


def _ffn_kernel(x_ref, wg_ref, wu_ref, wo_ref, mod_ref, g_ref, o_ref, h_scr, acc):
    k = pl.program_id(2)

    @pl.when(k == 0)
    def _():
        h = _rms(x_ref[...], g_ref[2:3, :]) * (1.0 + mod_ref[4:5, :]) + mod_ref[3:4, :]
        h_scr[...] = h.astype(BF16)
        acc[...] = jnp.zeros_like(acc)

    h = h_scr[...]
    g = _dot(h, wg_ref[...])
    u = _dot(h, wu_ref[...])
    act = (g * _sigmoid(g) * u).astype(BF16)
    acc[...] += _dot(act, wo_ref[...])

    @pl.when(k == pl.num_programs(2) - 1)
    def _():
        o_ref[...] = x_ref[...] + mod_ref[5:6, :] * _rms(acc[...], g_ref[3:4, :])


def _ffn(x, w_in, w_out, mod_l, g_l, tm, tf):
    bsz, seq, d = x.shape
    f = w_out.shape[0]
    nk = f // tf
    return pl.pallas_call(
        _ffn_kernel,
        grid=(bsz, seq // tm, nk),
        in_specs=[
            pl.BlockSpec((None, tm, d), lambda b, i, k: (b, i, 0)),
            pl.BlockSpec((d, tf), lambda b, i, k: (0, k)),
            pl.BlockSpec((d, tf), lambda b, i, k: (0, k + nk)),
            pl.BlockSpec((tf, d), lambda b, i, k: (k, 0)),
            pl.BlockSpec((None, 6, d), lambda b, i, k: (b, 0, 0)),
            pl.BlockSpec((4, d), lambda b, i, k: (0, 0)),
        ],
        out_specs=pl.BlockSpec((None, tm, d), lambda b, i, k: (b, i, 0)),
        out_shape=jax.ShapeDtypeStruct(x.shape, F32),
        scratch_shapes=[pltpu.VMEM((tm, d), BF16), pltpu.VMEM((tm, d), F32)],
        compiler_params=_cparams(("parallel", "parallel", "arbitrary")),
        name="ffn",
    )(x, w_in, w_in, w_out, mod_l, g_l)


def _rope_inputs(positions):
    bsz, seq = positions.shape
    half = MLA_ROPE // 2
    per_row = LANES // half
    inv_freq = ROPE_THETA ** (-jnp.arange(half, dtype=F32) / half)
    pos_rep = jnp.broadcast_to(positions[..., None], (bsz, seq, half)).reshape(bsz, seq // per_row, LANES)
    return pos_rep, jnp.tile(inv_freq, per_row).reshape(1, LANES)


def _rope_tables(pos_ref, f_ref, cos_scr, sin_lo_scr, sin_hi_scr):
    half = MLA_ROPE // 2
    per_row = LANES // half
    ang = pos_ref[...].astype(F32) * f_ref[...]
    cos_d = jnp.cos(ang)
    sin_d = jnp.sin(ang)
    n = ang.shape[0]
    lane = lax.broadcasted_iota(jnp.int32, ang.shape, 1)

    def onto(t, first_lane, q):
        shift = (first_lane - half * q) % LANES
        return t if shift == 0 else pltpu.roll(t, shift, axis=1)

    for q in range(per_row):
        rows = pl.ds(q, n, stride=per_row)
        x1, x2 = MLA_NOPE, MLA_NOPE + half
        cos_scr[rows, :] = jnp.where(lane < x1, 1.0, jnp.where(lane < x2, onto(cos_d, x1, q),
                                     jnp.where(lane < MLA_QK, onto(cos_d, x2, q), 0.0)))
        sin_lo_scr[rows, :] = jnp.where(lane < x1, 0.0, jnp.where(lane < x2, -onto(sin_d, x1, q), 0.0))
        sin_hi_scr[rows, :] = jnp.where(lane < x2, 0.0, jnp.where(lane < MLA_QK, onto(sin_d, x2, q), 0.0))


def _mla_proj_kernel(x_ref, pos_ref, f_ref, mod_ref, g_ref, wcq_ref, wckv_ref, wkr_ref,
                     qn_ref, kvn_ref, wq_ref, wk_ref, wv_ref, exp_ref, q_ref, k_ref, v_ref,
                     cos_scr, sin_lo_scr, sin_hi_scr):
    h = (_rms(x_ref[...], g_ref[0:1, :]) * (1.0 + mod_ref[1:2, :]) + mod_ref[0:1, :]).astype(BF16)
    cq = _rms(_dot(h, wcq_ref[...]), qn_ref[...]).astype(BF16)
    ckv = _rms(_dot(h, wckv_ref[...]), kvn_ref[...]).astype(BF16)
    kr = _dot(h, wkr_ref[...])
    _rope_tables(pos_ref, f_ref, cos_scr, sin_lo_scr, sin_hi_scr)
    cos = cos_scr[...]
    sin_lo = sin_lo_scr[...]
    sin_hi = sin_hi_scr[...]
    half = MLA_ROPE // 2

    def rope(t):
        width = t.shape[1]
        up = pltpu.roll(t, half, axis=1)
        down = pltpu.roll(t, width - half, axis=1)
        return [t[:, s:s + LANES] * cos + down[:, s:s + LANES] * sin_lo + up[:, s:s + LANES] * sin_hi
                for s in range(0, width, LANES)]

    q = _dot(cq, wq_ref[...])
    for hd, q_hd in enumerate(rope(q)):
        q_ref[:, hd * LANES:(hd + 1) * LANES] = q_hd.astype(BF16)
    kr_rot = rope(kr)[0].astype(BF16)
    k_ref[...] = (_dot(ckv, wk_ref[...]) + _dot(kr_rot, exp_ref[...])).astype(BF16)
    vt = lax.dot_general(wv_ref[...], ckv, (((1,), (1,)), ((), ())), preferred_element_type=F32)
    row = lax.broadcasted_iota(jnp.int32, vt.shape, 0)
    v_ref[...] = jnp.where(row % V_ROWS == MLA_V, 1.0, vt).astype(BF16)


def _mla_proj(x, rope_t, mod_l, g_l, wts, tm):
    bsz, seq, d = x.shape
    hw = MLA_HEADS * LANES

    def full(a):
        return pl.BlockSpec(a.shape, lambda b, i: (0,) * a.ndim)

    out_sds = jax.ShapeDtypeStruct((bsz, seq, hw), BF16)
    vt_rows = MLA_HEADS * V_ROWS
    vt_sds = jax.ShapeDtypeStruct((bsz, seq // tm, vt_rows, tm), BF16)
    return pl.pallas_call(
        _mla_proj_kernel,
        grid=(bsz, seq // tm),
        in_specs=[
            pl.BlockSpec((None, tm, d), lambda b, i: (b, i, 0)),
            pl.BlockSpec((None, tm // (LANES // (MLA_ROPE // 2)), LANES), lambda b, i: (b, i, 0)),
            pl.BlockSpec((1, LANES), lambda b, i: (0, 0)),
            pl.BlockSpec((None, 6, d), lambda b, i: (b, 0, 0)),
            pl.BlockSpec((4, d), lambda b, i: (0, 0)),
        ] + [full(w) for w in wts],
        out_specs=[pl.BlockSpec((None, tm, hw), lambda b, i: (b, i, 0))] * 2
        + [pl.BlockSpec((None, None, vt_rows, tm), lambda b, i: (b, i, 0, 0))],
        out_shape=[out_sds, out_sds, vt_sds],
        scratch_shapes=[pltpu.VMEM((tm, LANES), F32)] * 3,
        compiler_params=_cparams(("parallel", "parallel")),
        name="mla_proj",
    )(x, *rope_t, mod_l, g_l, *wts)


def _mla_weights(w_in, q_norm, kv_norm, w_uq, w_ukv, w_o):
    pad = LANES - MLA_QK
    w_cq = w_in[:, :MLA_Q_RANK]
    w_ckv = w_in[:, MLA_Q_RANK:MLA_Q_RANK + MLA_KV_RANK]
    w_kr = jnp.pad(w_in[:, MLA_Q_RANK + MLA_KV_RANK:], ((0, 0), (MLA_NOPE, pad)))
    scale = MLA_QK ** -0.5 * math.log2(math.e)
    wq = w_uq.reshape(MLA_Q_RANK, MLA_HEADS, MLA_QK) * scale
    wq_pad = jnp.pad(wq, ((0, 0), (0, 0), (0, pad))).reshape(MLA_Q_RANK, MLA_HEADS * LANES)
    wkv =w_ukv.reshape(MLA_KV_RANK, MLA_HEADS, MLA_NOPE + MLA_V)
    zk = jnp.zeros((MLA_KV_RANK, MLA_HEADS, LANES - MLA_NOPE), F32)
    wk_pad = jnp.concatenate([wkv[..., :MLA_NOPE], zk], axis=-1).reshape(MLA_KV_RANK, MLA_HEADS * LANES)
    wv_t = jnp.pad(wkv[..., MLA_NOPE:].transpose(1, 2, 0), ((0, 0), (0, V_ROWS - MLA_V), (0, 0)))
    wv_t = wv_t.reshape(MLA_HEADS * V_ROWS, MLA_KV_RANK)
    lane = jnp.arange(LANES)
    is_rope = (lane >= MLA_NOPE) & (lane < MLA_QK)
    expand = jnp.tile((jnp.eye(LANES, dtype=F32) * is_rope[None, :].astype(F32)), (1, MLA_HEADS))
    proj_w = (w_cq.astype(BF16), w_ckv.astype(BF16), w_kr.astype(BF16),
              q_norm.reshape(1, -1), kv_norm.reshape(1, -1),
              wq_pad.astype(BF16), wk_pad.astype(BF16), wv_t.astype(BF16), expand.astype(BF16))
    return proj_w, w_o.astype(BF16)


def _attn_kernel(q_ref, k_ref, vt_ref, o_ref, m_scr, mt_scr, acc, s0_scr, s1_scr, qt_scr):
    tk = vt_ref.shape[2]
    qi = pl.program_id(2)
    s_scr = (s0_scr, s1_scr)
    qt_scr[...] = q_ref[...].astype(F32).T.astype(BF16)
    m_scr[...] = jnp.full_like(m_scr, -jnp.inf)
    acc[...] = jnp.zeros_like(acc)

    def scores(hd, j):
        lanes = slice(hd * LANES, (hd + 1) * LANES)
        k = k_ref[pl.ds(pl.multiple_of(j * tk, tk), tk), lanes]
        st = _dot(k, qt_scr[lanes, :])
        s_scr[hd][...] = st
        mt_scr[hd] = jnp.max(st, axis=0, keepdims=True)

    def accumulate(hd, j, diag=None):
        st = s_scr[hd][...]
        if diag is not None:
            key = lax.broadcasted_iota(jnp.int32, st.shape, 0) + diag * tk
            qry = lax.broadcasted_iota(jnp.int32, st.shape, 1)
            st = jnp.where(key <= qry, st, -jnp.inf)
            mt = jnp.max(st, axis=0, keepdims=True)
        else:
            mt = mt_scr[hd]
        m_prev = m_scr[hd]
        m_new = jnp.maximum(m_prev, mt)
        alpha = jnp.exp2(m_prev - m_new)
        pt = jnp.exp2(st - m_new).astype(BF16)
        vt = vt_ref[j, hd * V_ROWS:(hd + 1) * V_ROWS, :]
        acc[hd] = alpha * acc[hd] + _dot(vt, pt)
        m_scr[hd] = m_new

    scores(0, 0)

    def body(j):
        scores(1, j)
        accumulate(0, j)
        scores(0, j + 1)
        accumulate(1, j)

    def body_pair(jj, carry):
        body(2 * jj)
        body(2 * jj + 1)
        return carry

    lax.fori_loop(0, qi, body_pair, 0)
    d0 = 2 * qi
    scores(1, d0)
    accumulate(0, d0, diag=0)
    scores(0, d0 + 1)
    accumulate(1, d0, diag=0)
    scores(1, d0 + 1)
    accumulate(0, d0 + 1, diag=1)
    accumulate(1, d0 + 1, diag=1)
    outs = []
    for hd in range(2):
        a = acc[hd]
        outs.append(a[:MLA_V, :] / a[MLA_V:MLA_V + 1, :])
    o_ref[...] = jnp.concatenate(outs, axis=0).T.astype(BF16)


def _attention(q, k, vt, tq):
    bsz, seq, hw = q.shape
    n_pairs = hw // (2 * LANES)
    n_kt, _, tk = vt.shape[1:]
    assert tq == 2 * tk, "the kernel walks the causal diagonal as two kv tiles per query tile"
    return pl.pallas_call(
        _attn_kernel,
        grid=(bsz, n_pairs, seq // tq),
        in_specs=[
            pl.BlockSpec((None, tq, 2 * LANES), lambda b, h, i: (b, i, h)),
            pl.BlockSpec((None, seq, 2 * LANES), lambda b, h, i: (b, 0, h)),
            pl.BlockSpec((None, n_kt, 2 * V_ROWS, tk), lambda b, h, i: (b, 0, h, 0)),
        ],
        out_specs=pl.BlockSpec((None, tq, 2 * MLA_V), lambda b, h, i: (b, i, h)),
        out_shape=jax.ShapeDtypeStruct((bsz, seq, n_pairs * 2 * MLA_V), BF16),
        scratch_shapes=[pltpu.VMEM((2, 1, tq), F32), pltpu.VMEM((2, 1, tq), F32), pltpu.VMEM((2, V_ROWS, tq), F32),
                        pltpu.VMEM((tk, tq), F32), pltpu.VMEM((tk, tq), F32), pltpu.VMEM((2 * LANES, tq), BF16)],
        compiler_params=_cparams(("parallel", "parallel", "arbitrary")),
        name="mla_attention",
    )(q, k, vt)


def _attn_out_kernel(a_ref, x_ref, w_ref, mod_ref, g_ref, o_ref):
    h = _dot(a_ref[...], w_ref[...])
    o_ref[...] = x_ref[...] + mod_ref[2:3, :] * _rms(h, g_ref[1:2, :])


def _attn_out(a, x, wo_pad, mod_l, g_l, tm):
    bsz, seq, d = x.shape
    hw = a.shape[-1]
    return pl.pallas_call(
        _attn_out_kernel,
        grid=(bsz, seq // tm),
        in_specs=[
            pl.BlockSpec((None, tm, hw), lambda b, i: (b, i, 0)),
            pl.BlockSpec((None, tm, d), lambda b, i: (b, i, 0)),
            pl.BlockSpec((hw, d), lambda b, i: (0, 0)),
            pl.BlockSpec((None, 6, d), lambda b, i: (b, 0, 0)),
            pl.BlockSpec((4, d), lambda b, i: (0, 0)),
        ],
        out_specs=pl.BlockSpec((None, tm, d), lambda b, i: (b, i, 0)),
        out_shape=jax.ShapeDtypeStruct(x.shape, F32),
        compiler_params=_cparams(("parallel", "parallel")),
        name="mla_out",
    )(a, x, wo_pad, mod_l, g_l)


def _route(logits):
    row = lax.broadcasted_iota(jnp.int32, logits.shape, 0)
    n = logits.shape[0]
    neg = -jnp.inf
    m1 = jnp.max(logits, axis=0, keepdims=True)
    i1 = jnp.min(jnp.where(logits == m1, row, n), axis=0, keepdims=True)
    rest = jnp.where(row == i1, neg, logits)
    m2 = jnp.max(rest, axis=0, keepdims=True)
    i2 = jnp.min(jnp.where(rest == m2, row, n), axis=0, keepdims=True)
    e = jnp.exp(m2 - m1)
    w1 = 1.0 / (1.0 + e)
    w2 = e / (1.0 + e)
    first = row == i1
    second = row == i2
    comb = jnp.where(first, w1, 0.0) + jnp.where(second, w2, 0.0)
    sel = jnp.where(first, 1.0, 0.0) + jnp.where(second, 1.0, 0.0)
    return comb, sel


def _moe_kernel(x_ref, wr_ref, br_ref, tri_ref, wg_ref, wu_ref, wo_ref, mod_ref, g_ref, o_ref,
                h_scr, sel_scr, rank_scr, w_scr, acc):
    e = pl.program_id(1)
    tb = x_ref.shape[0]

    @pl.when(e == 0)
    def _():
        h = _rms(x_ref[...], g_ref[2:3, :]) * (1.0 + mod_ref[4:5, :]) + mod_ref[3:4, :]
        h_hi = h.astype(BF16)
        h_lo = (h - h_hi.astype(F32)).astype(BF16)
        nt = (((1,), (1,)), ((), ()))
        logits = (lax.dot_general(wr_ref[0], h_hi, nt, preferred_element_type=F32)
                  + lax.dot_general(wr_ref[1], h_hi, nt, preferred_element_type=F32)
                  + lax.dot_general(wr_ref[0], h_lo, nt, preferred_element_type=F32))
        n_e = sel_scr.shape[0]
        comb, sel = _route(logits[:n_e, :] + br_ref[:n_e, 0:1])
        sel_scr[...] = sel
        w_scr[...] = comb
        sel_pad = jnp.concatenate([sel, jnp.zeros_like(sel)], axis=0).astype(BF16)
        rank_scr[...] = _dot(sel_pad, tri_ref[...])[:n_e, :]
        h_scr[...] = h_hi
        acc[...] = jnp.zeros_like(acc)

    sel_row = sel_scr[pl.ds(e, 1), :]
    rank_row = rank_scr[pl.ds(e, 1), :]
    w_row = w_scr[pl.ds(e, 1), :]
    n_rows = jnp.sum(sel_row).astype(jnp.int32)

    def step(r, carry):
        slot = (lax.broadcasted_iota(jnp.int32, (MOE_SUB, tb), 0) + r * MOE_SUB).astype(F32)
        hit = rank_row == slot
        onehot = jnp.where(hit, sel_row, 0.0).astype(BF16)
        xs = _dot(onehot, h_scr[...]).astype(BF16)
        g = _dot(xs, wg_ref[...])
        u = _dot(xs, wu_ref[...])
        act = (g * _sigmoid(g) * u).astype(BF16)
        y = _dot(act, wo_ref[...])
        w_col = jnp.sum(jnp.where(hit, w_row, 0.0), axis=1, keepdims=True)
        yw = (y * w_col).astype(BF16)
        acc[...] += lax.dot_general(onehot, yw, (((0,), (0,)), ((), ())), preferred_element_type=F32)
        return carry

    lax.fori_loop(0, (n_rows + MOE_SUB - 1) // MOE_SUB, step, 0)

    @pl.when(e == pl.num_programs(1) - 1)
    def _():
        o_ref[...] = x_ref[...] + mod_ref[5:6, :] * _rms(acc[...], g_ref[3:4, :])


def _moe_layer(x, w_router, b_router, w_in, w_out, mod_l, g_l):
    bsz, seq, d = x.shape
    n_e, f, _ = w_out.shape
    tb = min(MOE_TB, seq)
    blocks_per_batch = seq // tb
    wr = jnp.pad(w_router.T, ((0, LANES - n_e), (0, 0)))
    wr_hi = wr.astype(BF16)
    wr = jnp.stack([wr_hi, (wr - wr_hi.astype(F32)).astype(BF16)])
    br = jnp.broadcast_to(jnp.pad(b_router, (0, LANES - n_e))[:, None], (LANES, LANES))
    pos = jnp.arange(tb)
    tri = (pos[:, None] < pos[None, :]).astype(BF16)
    w_in = w_in.astype(BF16)
    out = pl.pallas_call(
        _moe_kernel,
        grid=(bsz * blocks_per_batch, n_e),
        in_specs=[
            pl.BlockSpec((tb, d), lambda i, e: (i, 0)),
            pl.BlockSpec((2, LANES, d), lambda i, e: (0, 0, 0)),
            pl.BlockSpec((LANES, LANES), lambda i, e: (0, 0)),
            pl.BlockSpec((tb, tb), lambda i, e: (0, 0)),
            pl.BlockSpec((None, d, f), lambda i, e: (e, 0, 0)),
            pl.BlockSpec((None, d, f), lambda i, e: (e, 0, 1)),
            pl.BlockSpec((None, f, d), lambda i, e: (e, 0, 0)),
            pl.BlockSpec((None, 6, d), lambda i, e: (i // blocks_per_batch, 0, 0)),
            pl.BlockSpec((4, d), lambda i, e: (0, 0)),
        ],
        out_specs=pl.BlockSpec((tb, d), lambda i, e: (i, 0)),
        out_shape=jax.ShapeDtypeStruct((bsz * seq, d), F32),
        scratch_shapes=[pltpu.VMEM((tb, d), BF16), pltpu.VMEM((n_e, tb), F32), pltpu.VMEM((n_e, tb), F32),
                        pltpu.VMEM((n_e, tb), F32), pltpu.VMEM((tb, d), F32)],
        compiler_params=_cparams(("parallel", "arbitrary")),
        name="moe",
    )(x.reshape(bsz * seq, d), wr, br, tri, w_in, w_in, w_out.astype(BF16), mod_l, g_l)
    return out.reshape(bsz, seq, d)


def kernel(x, c, positions, norm_g, w_ada, b_ada, ssm_a_re, ssm_a_im, ssm_log_dt, ssm_b_re, ssm_b_im, ssm_c_re, ssm_c_im, ssm_d, ssm_w_glu, ffn_w_in, ffn_w_out, mla_w_in, mla_q_norm, mla_kv_norm, mla_w_uq, mla_w_ukv, mla_w_o, moe_w_router, moe_b_router, moe_w_in, moe_w_out):
    depth = norm_g.shape[0]
    seq = x.shape[1]
    tm = min(512, seq)
    mod = _ada(c, w_ada, b_ada)
    rope_t = None
    for i in range(depth):
        j = i // 2
        mod_l, g_l = mod[i], norm_g[i]
        if i % 2 == 0:
            ssm_w = _ssm_weights(ssm_a_re[j], ssm_a_im[j], ssm_log_dt[j], ssm_b_re[j], ssm_b_im[j],
                                 ssm_c_re[j], ssm_c_im[j], ssm_d[j])
            ucat = _ssm_pre(x, mod_l, g_l, tm)
            zcat = _ssm(ucat, *ssm_w)
            x = _ssm_post(zcat, x, ssm_w_glu[j].astype(BF16), mod_l, g_l, tm)
            f = ffn_w_out.shape[1]
            x = _ffn(x, ffn_w_in[j].astype(BF16), ffn_w_out[j].astype(BF16), mod_l, g_l, tm, f // 2)
        else:
            if rope_t is None:
                rope_t = _rope_inputs(positions)
            proj_w, wo_pad = _mla_weights(mla_w_in[j], mla_q_norm[j], mla_kv_norm[j],
                                          mla_w_uq[j], mla_w_ukv[j], mla_w_o[j])
            q, k, v = _mla_proj(x, rope_t, mod_l, g_l, proj_w, tm)
            a = _attention(q, k, v, 2 * tm)
            x = _attn_out(a, x, wo_pad, mod_l, g_l, tm)
            x = _moe_layer(x, moe_w_router[j], moe_b_router[j], moe_w_in[j], moe_w_out[j], mod_l, g_l)
    return x
```

```python
import functools
import math

import jax
import jax.numpy as jnp
from jax import lax
from jax.experimental import pallas as pl
from jax.experimental.pallas import tpu as pltpu

F32 = jnp.float32
BF16 = jnp.bfloat16

RMS_EPS = 1e-6
LANES = 128
SSM_GROUP = 16
SSM_STATE = 64
SSM_L = 16
SSM_LG = LANES // SSM_GROUP
STRIDE_PAD = 4
MLA_HEADS = 16
MLA_NOPE = 64
MLA_ROPE = 32
MLA_V = 64
MLA_QK = MLA_NOPE + MLA_ROPE
ATTN_TQ = 1024
V_ROWS = 80
MLA_Q_RANK = 384
MLA_KV_RANK = 256
ROPE_THETA = 10000.0
N_EXPERTS = 8
MOE_TB = 1024
MOE_SUB = 144
VMEM_LIMIT = 56 * 1024 * 1024


def _cparams(sem):
    return pltpu.CompilerParams(dimension_semantics=sem, vmem_limit_bytes=VMEM_LIMIT)


def _rms(x, g):
    return x * lax.rsqrt(jnp.mean(x * x, axis=-1, keepdims=True) + RMS_EPS) * g


def _sigmoid(x):
    return 0.5 * jnp.tanh(0.5 * x) + 0.5


def _dot(a, b):
    return jnp.dot(a, b, preferred_element_type=F32)


def _ada_kernel(c_ref, w_ref, b_ref, o_ref):
    c = c_ref[...]
    c_act = c * _sigmoid(c)
    c_hi = c_act.astype(BF16)
    c_lo = (c_act - c_hi.astype(F32)).astype(BF16)
    w = w_ref[...]
    w_hi = w.astype(BF16)
    w_lo = (w - w_hi.astype(F32)).astype(BF16)
    o_ref[...] = _dot(c_hi, w_hi) + _dot(c_hi, w_lo) + _dot(c_lo, w_hi) + b_ref[...]


def _ada(c, w_ada, b_ada):
    depth, d, _ = w_ada.shape
    bsz = c.shape[0]
    rows = 16
    c_pad =jnp.pad(c, ((0, rows - bsz), (0, 0)))
    out = pl.pallas_call(
        _ada_kernel,
        grid=(depth, 6),
        in_specs=[
            pl.BlockSpec((rows, d), lambda i, j: (0, 0)),
            pl.BlockSpec((None, d, d), lambda i, j: (i, 0, j)),
            pl.BlockSpec((None, None, 1, d), lambda i, j: (i, j, 0, 0)),
        ],
        out_specs=pl.BlockSpec((None, None, rows, d), lambda i, j: (i, j, 0, 0)),
        out_shape=jax.ShapeDtypeStruct((depth, 6, rows, d), F32),
        compiler_params=_cparams(("arbitrary", "arbitrary")),
        name="ada_mod",
    )(c_pad, w_ada, b_ada.reshape(depth, 6, 1, d))
    return out[:, :, :bsz, :].transpose(0, 2, 1, 3)


def _ssm_pre_kernel(x_ref, mod_ref, g_ref, o_ref, scr):
    tm = x_ref.shape[0]
    n = tm // SSM_L
    u = _rms(x_ref[...], g_ref[0:1, :]) * (1.0 + mod_ref[1:2, :]) + mod_ref[0:1, :]
    n_lg = o_ref.shape[0]
    pitch = SSM_L + STRIDE_PAD
    for c in range(n):
        for lg in range(n_lg):
            scr[lg, c * pitch:c * pitch + SSM_L, :] = u[c * SSM_L:(c + 1) * SSM_L, lg * LANES:(lg + 1) * LANES]
    for s in range(SSM_L):
        for lg in range(n_lg):
            o_ref[lg, :, s * LANES:(s + 1) * LANES] = scr[lg, pl.ds(s, n, stride=pitch), :].astype(BF16)


def _ssm_pre(x, mod_l, g_l, tm):
    bsz, seq, d = x.shape
    n_lg = d // LANES
    return pl.pallas_call(
        _ssm_pre_kernel,
        grid=(bsz, seq // tm),
        in_specs=[
            pl.BlockSpec((None, tm, d), lambda b, i: (b, i, 0)),
            pl.BlockSpec((None, 6, d), lambda b, i: (b, 0, 0)),
            pl.BlockSpec((4, d), lambda b, i: (0, 0)),
        ],
        out_specs=pl.BlockSpec((None, n_lg, tm // SSM_L, SSM_L * LANES), lambda b, i: (b, 0, i, 0)),
        out_shape=jax.ShapeDtypeStruct((bsz, n_lg, seq // SSM_L, SSM_L * LANES), BF16),
        scratch_shapes=[pltpu.VMEM((n_lg, (tm // SSM_L) * (SSM_L + STRIDE_PAD), LANES), F32)],
        compiler_params=_cparams(("parallel", "parallel")),
        name="ssm_pre",
    )(x, mod_l, g_l)


def _gelu_tanh(y):
    return 0.5 * y * (1.0 + jnp.tanh(math.sqrt(2.0 / math.pi) * (y + 0.044715 * (y * y * y))))


def _same_group(shape, row_shift, col_shift):
    row_g = (lax.broadcasted_iota(jnp.int32, shape, 0) >> row_shift) & (SSM_LG - 1)
    col_g = (lax.broadcasted_iota(jnp.int32, shape, 1) >> col_shift) & (SSM_LG - 1)
    return row_g == col_g


def _ssm_expand_weights(xw_ref, yw_ref, cc_ref, t1_ref, t1t_ref, win_s, wout_s, m8_s, mx_s):
    hl = SSM_L // 2
    in_mask = _same_group((LANES, t1_ref.shape[1]), 4, 6)
    out_mask = _same_group((t1_ref.shape[1], LANES), 6, 4)
    k_mask = _same_group((LANES, LANES), 4, 4)
    m8_s[...] = jnp.zeros_like(m8_s)
    for s in range(SSM_L):
        rows = slice(s * LANES, (s + 1) * LANES)
        val = lax.dot_general(xw_ref[s], t1_ref[...], (((0,), (0,)), ((), ())), preferred_element_type=F32)
        win_s[rows, :] = jnp.where(in_mask, val, 0.0).astype(BF16)
        val = lax.dot_general(t1t_ref[...], yw_ref[s], (((1,), (1,)), ((), ())), preferred_element_type=F32)
        wout_s[:, rows] = jnp.where(out_mask, val, 0.0).astype(BF16)
    for tau in range(SSM_L):
        k_tau = lax.dot_general(xw_ref[SSM_L - 1 - tau], cc_ref[...], (((0,), (1,)), ((), ())),
                                preferred_element_type=F32)
        kbd = jnp.where(k_mask, k_tau, 0.0).astype(BF16)
        for a in range(hl):
            if a + tau < hl:
                m8_s[a * LANES:(a + 1) * LANES, (a + tau) * LANES:(a + tau + 1) * LANES] = kbd
            b = a + tau - hl
            if 0 <= b < hl:
                mx_s[a * LANES:(a + 1) * LANES, b * LANES:(b + 1) * LANES] = kbd


def _ssm_kernel(u_ref, xw_ref, yw_ref, cc_ref, t1_ref, t1t_ref, a_ref, d_ref, o_ref,
                win_s, wout_s, m8_s, mx_s, v_scr, h_scr):
    n_chunks = u_ref.shape[0]
    half = (SSM_L // 2) * LANES
    ns = a_ref.shape[1]

    @pl.when(pl.program_id(1) == 0)
    def _():
        _ssm_expand_weights(xw_ref, yw_ref, cc_ref, t1_ref, t1t_ref, win_s, wout_s, m8_s, mx_s)

    uc = u_ref[...]
    v_scr[...] = _dot(uc, win_s[...])
    ar = a_ref[0:1, :]
    ai = a_ref[1:2, :]

    def step(c, carry):
        hr, hi = carry
        h_scr[pl.ds(c, 1), pl.ds(0, ns)] = hr
        h_scr[pl.ds(c, 1), pl.ds(ns, ns)] = hi
        vr = v_scr[pl.ds(c, 1), pl.ds(0, ns)]
        vi = v_scr[pl.ds(c, 1), pl.ds(ns, ns)]
        return ar * hr - ai * hi + vr, ar * hi + ai * hr + vi

    zero = jnp.zeros((1, ns), F32)
    lax.fori_loop(0, n_chunks, step, (zero, zero))
    y = _dot(h_scr[...].astype(BF16), wout_s[...])
    u_lo = uc[:, :half]
    u_hi = uc[:, half:]
    m8 = m8_s[...]
    y_lo = y[:, :half] + _dot(u_lo, m8)
    y_hi = y[:, half:] + _dot(u_hi, m8) + _dot(u_lo, mx_s[...])
    dd = d_ref[...]
    o_ref[:, :half] = _gelu_tanh(y_lo + dd[:, :half] * u_lo.astype(F32)).astype(BF16)
    o_ref[:, half:] = _gelu_tanh(y_hi + dd[:, half:] * u_hi.astype(F32)).astype(BF16)


def _ssm(ucat, xw, yw, cc, t1, t1t, a16, dcat):
    bsz, n_lg, n_chunks, width = ucat.shape
    ns2 = t1.shape[-1]
    half = width // 2

    def per_lg(a):
        return pl.BlockSpec((None,) + a.shape[1:], lambda g, b: (g,) + (0,) * (a.ndim - 1))

    def const(a):
        return pl.BlockSpec(a.shape, lambda g, b: (0, 0))

    return pl.pallas_call(
        _ssm_kernel,
        grid=(n_lg, bsz),
        in_specs=[
            pl.BlockSpec((None, None, n_chunks, width), lambda g, b: (b, g, 0, 0)),
            per_lg(xw), per_lg(yw), per_lg(cc), const(t1), const(t1t), per_lg(a16), per_lg(dcat),
        ],
        out_specs=pl.BlockSpec((None, None, n_chunks, width), lambda g, b: (b, g, 0, 0)),
        out_shape=jax.ShapeDtypeStruct(ucat.shape, BF16),
        scratch_shapes=[
            pltpu.VMEM((width, ns2), BF16), pltpu.VMEM((ns2, width), BF16),
            pltpu.VMEM((half, half), BF16), pltpu.VMEM((half, half), BF16),
            pltpu.VMEM((n_chunks, ns2), F32), pltpu.VMEM((n_chunks, ns2), F32),
        ],
        compiler_params=_cparams(("arbitrary", "arbitrary")),
        name="ssm_scan",
    )(ucat, xw, yw, cc, t1, t1t, a16, dcat)


def _ssm_post_kernel(z_ref, x_ref, w_ref, mod_ref, g_ref, o_ref, zp, rp):
    n_lg, n, _ = z_ref.shape
    d = x_ref.shape[1]
    for s in range(SSM_L):
        for lg in range(n_lg):
            zp[s * n:(s + 1) * n, lg * LANES:(lg + 1) * LANES] = z_ref[lg, :, s * LANES:(s + 1) * LANES]
    gl = _dot(zp[...], w_ref[...])
    h = gl[:, :d] * _sigmoid(gl[:, d:])
    r = mod_ref[2:3, :] * _rms(h, g_ref[1:2, :])
    pitch = n + STRIDE_PAD
    for s in range(SSM_L):
        for lg in range(n_lg):
            rp[lg, s * pitch:s * pitch + n, :] = r[s * n:(s + 1) * n, lg * LANES:(lg + 1) * LANES]
    sub = 8
    for c in range(n):
        for k in range(SSM_L // sub):
            rows = pl.ds(c * SSM_L + k * sub, sub)
            for lg in range(n_lg):
                cols = slice(lg * LANES, (lg + 1) * LANES)
                o_ref[rows, cols] = x_ref[rows, cols] + rp[lg, pl.ds(k * sub * pitch + c, sub, stride=pitch), :]


def _ssm_post(zcat, x, w_glu, mod_l, g_l, tm):
    bsz, seq, d = x.shape
    n_lg = d // LANES
    return pl.pallas_call(
        _ssm_post_kernel,
        grid=(bsz, seq // tm),
        in_specs=[
            pl.BlockSpec((None, n_lg, tm // SSM_L, SSM_L * LANES), lambda b, i: (b, 0, i, 0)),
            pl.BlockSpec((None, tm, d), lambda b, i: (b, i, 0)),
            pl.BlockSpec((d, 2 * d), lambda b, i: (0, 0)),
            pl.BlockSpec((None, 6, d), lambda b, i: (b, 0, 0)),
            pl.BlockSpec((4, d), lambda b, i: (0, 0)),
        ],
        out_specs=pl.BlockSpec((None, tm, d), lambda b, i: (b, i, 0)),
        out_shape=jax.ShapeDtypeStruct(x.shape, F32),
        scratch_shapes=[pltpu.VMEM((tm, d), BF16),
                        pltpu.VMEM((n_lg, SSM_L * (tm // SSM_L + STRIDE_PAD), LANES), F32)],
        compiler_params=_cparams(("parallel", "parallel")),
        name="ssm_post",
    )(zcat, x, w_glu, mod_l, g_l)


def _ssm_weights(a_re, a_im, log_dt, b_re, b_im, c_re, c_im, d_skip):
    n_groups = a_re.shape[0]
    n_lg = n_groups // SSM_LG
    p, hh, ll = SSM_STATE, SSM_GROUP, SSM_L
    dt = jnp.exp(log_dt)[:, None]
    k = jnp.arange(ll + 1, dtype=F32)[:, None, None]
    mag = jnp.exp(k * (a_re * dt))
    pw_re = mag * jnp.cos(k * (a_im * dt))
    pw_im = mag * jnp.sin(k * (a_im * dt))
    den = a_re * a_re + a_im * a_im
    num_re, num_im = pw_re[1] - 1.0, pw_im[1]
    f_re = (num_re * a_re + num_im * a_im) / den
    f_im = (num_im * a_re - num_re * a_im) / den
    bb_re = f_re[..., None] * b_re - f_im[..., None] * b_im
    bb_im = f_re[..., None] * b_im + f_im[..., None] * b_re
    b2_re, b2_im = (x.reshape(n_lg, SSM_LG, p, hh).transpose(0, 2, 1, 3).reshape(n_lg, p, LANES)
                    for x in (bb_re, bb_im))

    def lanes_g(x):
        x = x.reshape(x.shape[0], n_lg, SSM_LG, p).transpose(0, 1, 3, 2)
        return jnp.repeat(x, hh, axis=-1)

    l_re, l_im = lanes_g(pw_re[:ll][::-1]), lanes_g(pw_im[:ll][::-1])
    xw = jnp.concatenate([l_re * b2_re - l_im * b2_im, l_re * b2_im + l_im * b2_re], axis=2)
    xw = xw.transpose(1, 0, 2, 3)
    cp_re = c_re[None] * pw_re[1:, :, None, :] - c_im[None] * pw_im[1:, :, None, :]
    cp_im = c_re[None] * pw_im[1:, :, None, :] + c_im[None] * pw_re[1:, :, None, :]
    yw = jnp.concatenate([cp_re, -cp_im], axis=-1).reshape(ll, n_lg, LANES, 2 * p).transpose(1, 0, 2, 3)
    cc = jnp.concatenate([c_re, -c_im], axis=-1).reshape(n_lg, LANES, 2 * p)
    j1 = jnp.arange(2 * SSM_LG * p)
    t1 = (jnp.arange(2 * p)[:, None] == ((j1 // (SSM_LG * p)) * p + j1 % p)[None, :]).astype(BF16)
    a16 = jnp.stack([pw_re[ll], pw_im[ll]], axis=0).reshape(2, n_lg, SSM_LG * p).transpose(1, 0, 2)
    dcat = jnp.tile(d_skip.reshape(n_lg, 1, LANES), (1, 1, ll))
    return xw.astype(BF16), yw.astype(BF16), cc.astype(BF16), t1, t1.T, a16, dcat


def _ffn_kernel(x_ref, wg_ref, wu_ref, wo_ref, mod_ref, g_ref, o_ref, h_scr, acc):
    k = pl.program_id(2)

    @pl.when(k == 0)
    def _():
        h = _rms(x_ref[...], g_ref[2:3, :]) * (1.0 + mod_ref[4:5, :]) + mod_ref[3:4, :]
        h_scr[...] = h.astype(BF16)
        acc[...] = jnp.zeros_like(acc)

    h = h_scr[...]
    g = _dot(h, wg_ref[...])
    u = _dot(h, wu_ref[...])
    act = (g * _sigmoid(g) * u).astype(BF16)
    acc[...] += _dot(act, wo_ref[...])

    @pl.when(k == pl.num_programs(2) - 1)
    def _():
        o_ref[...] = x_ref[...] + mod_ref[5:6, :] * _rms(acc[...], g_ref[3:4, :])


def _ffn(x, w_in, w_out, mod_l, g_l, tm, tf):
    bsz, seq, d = x.shape
    f = w_out.shape[0]
    nk = f // tf
    return pl.pallas_call(
        _ffn_kernel,
        grid=(bsz, seq // tm, nk),
        in_specs=[
            pl.BlockSpec((None, tm, d), lambda b, i, k: (b, i, 0)),
            pl.BlockSpec((d, tf), lambda b, i, k: (0, k)),
            pl.BlockSpec((d, tf), lambda b, i, k: (0, k + nk)),
            pl.BlockSpec((tf, d), lambda b, i, k: (k, 0)),
            pl.BlockSpec((None, 6, d), lambda b, i, k: (b, 0, 0)),
            pl.BlockSpec((4, d), lambda b, i, k: (0, 0)),
        ],
        out_specs=pl.BlockSpec((None, tm, d), lambda b, i, k: (b, i, 0)),
        out_shape=jax.ShapeDtypeStruct(x.shape, F32),
        scratch_shapes=[pltpu.VMEM((tm, d), BF16), pltpu.VMEM((tm, d), F32)],
        compiler_params=_cparams(("parallel", "parallel", "arbitrary")),
        name="ffn",
    )(x, w_in, w_in, w_out, mod_l, g_l)


def _rope_inputs(positions):
    bsz, seq = positions.shape
    half = MLA_ROPE // 2
    per_row = LANES // half
    inv_freq = ROPE_THETA ** (-jnp.arange(half, dtype=F32) / half)
    pos_rep = jnp.broadcast_to(positions[..., None], (bsz, seq, half)).reshape(bsz, seq // per_row, LANES)
    return pos_rep, jnp.tile(inv_freq, per_row).reshape(1, LANES)


def _rope_tables(pos_ref, f_ref, cos_scr, sin_lo_scr, sin_hi_scr):
    half = MLA_ROPE // 2
    per_row = LANES // half
    ang = pos_ref[...].astype(F32) * f_ref[...]
    cos_d = jnp.cos(ang)
    sin_d = jnp.sin(ang)
    n = ang.shape[0]
    lane = lax.broadcasted_iota(jnp.int32, ang.shape, 1)

    def onto(t, first_lane, q):
        shift = (first_lane - half * q) % LANES
        return t if shift == 0 else pltpu.roll(t, shift, axis=1)

    for q in range(per_row):
        rows = pl.ds(q, n, stride=per_row)
        x1, x2 = MLA_NOPE, MLA_NOPE + half
        cos_scr[rows, :] = jnp.where(lane < x1, 1.0, jnp.where(lane < x2, onto(cos_d, x1, q),
                                     jnp.where(lane < MLA_QK, onto(cos_d, x2, q), 0.0)))
        sin_lo_scr[rows, :] = jnp.where(lane < x1, 0.0, jnp.where(lane < x2, -onto(sin_d, x1, q), 0.0))
        sin_hi_scr[rows, :] = jnp.where(lane < x2, 0.0, jnp.where(lane < MLA_QK, onto(sin_d, x2, q), 0.0))


def _mla_proj_kernel(x_ref, pos_ref, f_ref, mod_ref, g_ref, wcq_ref, wckv_ref, wkr_ref,
                     qn_ref, kvn_ref, wq_ref, wk_ref, wv_ref, exp_ref, q_ref, k_ref, v_ref,
                     cos_scr, sin_lo_scr, sin_hi_scr):
    h = (_rms(x_ref[...], g_ref[0:1, :]) * (1.0 + mod_ref[1:2, :]) + mod_ref[0:1, :]).astype(BF16)
    cq = _rms(_dot(h, wcq_ref[...]), qn_ref[...]).astype(BF16)
    ckv = _rms(_dot(h, wckv_ref[...]), kvn_ref[...]).astype(BF16)
    kr = _dot(h, wkr_ref[...])
    _rope_tables(pos_ref, f_ref, cos_scr, sin_lo_scr, sin_hi_scr)
    cos = cos_scr[...]
    sin_lo = sin_lo_scr[...]
    sin_hi = sin_hi_scr[...]
    half = MLA_ROPE // 2

    def rope(t):
        width = t.shape[1]
        up = pltpu.roll(t, half, axis=1)
        down = pltpu.roll(t, width - half, axis=1)
        return [t[:, s:s + LANES] * cos + down[:, s:s + LANES] * sin_lo + up[:, s:s + LANES] * sin_hi
                for s in range(0, width, LANES)]

    q = _dot(cq, wq_ref[...])
    for hd, q_hd in enumerate(rope(q)):
        q_ref[:, hd * LANES:(hd + 1) * LANES] = q_hd.astype(BF16)
    kr_rot = rope(kr)[0].astype(BF16)
    k_ref[...] = (_dot(ckv, wk_ref[...]) + _dot(kr_rot, exp_ref[...])).astype(BF16)
    vt = lax.dot_general(wv_ref[...], ckv, (((1,), (1,)), ((), ())), preferred_element_type=F32)
    row = lax.broadcasted_iota(jnp.int32, vt.shape, 0)
    v_ref[...] = jnp.where(row % V_ROWS == MLA_V, 1.0, vt).astype(BF16)


def _mla_proj(x, rope_t, mod_l, g_l, wts, tm):
    bsz, seq, d = x.shape
    hw = MLA_HEADS * LANES

    def full(a):
        return pl.BlockSpec(a.shape, lambda b, i: (0,) * a.ndim)

    out_sds = jax.ShapeDtypeStruct((bsz, seq, hw), BF16)
    vt_rows = MLA_HEADS * V_ROWS
    vt_sds = jax.ShapeDtypeStruct((bsz, seq // tm, vt_rows, tm), BF16)
    return pl.pallas_call(
        _mla_proj_kernel,
        grid=(bsz, seq // tm),
        in_specs=[
            pl.BlockSpec((None, tm, d), lambda b, i: (b, i, 0)),
            pl.BlockSpec((None, tm // (LANES // (MLA_ROPE // 2)), LANES), lambda b, i: (b, i, 0)),
            pl.BlockSpec((1, LANES), lambda b, i: (0, 0)),
            pl.BlockSpec((None, 6, d), lambda b, i: (b, 0, 0)),
            pl.BlockSpec((4, d), lambda b, i: (0, 0)),
        ] + [full(w) for w in wts],
        out_specs=[pl.BlockSpec((None, tm, hw), lambda b, i: (b, i, 0))] * 2
        + [pl.BlockSpec((None, None, vt_rows, tm), lambda b, i: (b, i, 0, 0))],
        out_shape=[out_sds, out_sds, vt_sds],
        scratch_shapes=[pltpu.VMEM((tm, LANES), F32)] * 3,
        compiler_params=_cparams(("parallel", "parallel")),
        name="mla_proj",
    )(x, *rope_t, mod_l, g_l, *wts)


def _mla_weights(w_in, q_norm, kv_norm, w_uq, w_ukv, w_o):
    pad = LANES - MLA_QK
    w_cq = w_in[:, :MLA_Q_RANK]
    w_ckv = w_in[:, MLA_Q_RANK:MLA_Q_RANK + MLA_KV_RANK]
    w_kr = jnp.pad(w_in[:, MLA_Q_RANK + MLA_KV_RANK:], ((0, 0), (MLA_NOPE, pad)))
    scale = MLA_QK ** -0.5 * math.log2(math.e)
    wq = w_uq.reshape(MLA_Q_RANK, MLA_HEADS, MLA_QK) * scale
    wq_pad = jnp.pad(wq, ((0, 0), (0, 0), (0, pad))).reshape(MLA_Q_RANK, MLA_HEADS * LANES)
    wkv =w_ukv.reshape(MLA_KV_RANK, MLA_HEADS, MLA_NOPE + MLA_V)
    zk = jnp.zeros((MLA_KV_RANK, MLA_HEADS, LANES - MLA_NOPE), F32)
    wk_pad = jnp.concatenate([wkv[..., :MLA_NOPE], zk], axis=-1).reshape(MLA_KV_RANK, MLA_HEADS * LANES)
    wv_t = jnp.pad(wkv[..., MLA_NOPE:].transpose(1, 2, 0), ((0, 0), (0, V_ROWS - MLA_V), (0, 0)))
    wv_t = wv_t.reshape(MLA_HEADS * V_ROWS, MLA_KV_RANK)
    lane = jnp.arange(LANES)
    is_rope = (lane >= MLA_NOPE) & (lane < MLA_QK)
    expand = jnp.tile((jnp.eye(LANES, dtype=F32) * is_rope[None, :].astype(F32)), (1, MLA_HEADS))
    proj_w = (w_cq.astype(BF16), w_ckv.astype(BF16), w_kr.astype(BF16),
              q_norm.reshape(1, -1), kv_norm.reshape(1, -1),
              wq_pad.astype(BF16), wk_pad.astype(BF16), wv_t.astype(BF16), expand.astype(BF16))
    return proj_w, w_o.astype(BF16)


def _attn_kernel(q_ref, k_ref, vt_ref, o_ref, m_scr, mt_scr, acc, s0_scr, s1_scr, qt_scr):
    tk = vt_ref.shape[2]
    n_sub = q_ref.shape[0] // tk
    qi = pl.program_id(2)
    s_scr = (s0_scr, s1_scr)
    qt_scr[...] = q_ref[...].astype(F32).T.astype(BF16)
    m_scr[...] = jnp.full_like(m_scr, -jnp.inf)
    acc[...] = jnp.zeros_like(acc)

    def scores(hd, j, q0=0):
        lanes = slice(hd * LANES, (hd + 1) * LANES)
        k = k_ref[pl.ds(pl.multiple_of(j * tk, tk), tk), lanes]
        st = _dot(k, qt_scr[lanes, q0:])
        s_scr[hd][:, q0:] = st
        mt_scr[hd, :, q0:] = jnp.max(st, axis=0, keepdims=True)

    def accumulate(hd, j, diag=None):
        q0 = 0 if diag is None else diag * tk
        st = s_scr[hd][:, q0:]
        if diag is not None:
            key = lax.broadcasted_iota(jnp.int32, st.shape, 0)
            qry = lax.broadcasted_iota(jnp.int32, st.shape, 1)
            st = jnp.where(key <= qry, st, -jnp.inf)
            mt = jnp.max(st, axis=0, keepdims=True)
        else:
            mt = mt_scr[hd]
        m_prev = m_scr[hd, :, q0:]
        m_new = jnp.maximum(m_prev, mt)
        alpha = jnp.exp2(m_prev - m_new)
        pt = jnp.exp2(st - m_new).astype(BF16)
        vt = vt_ref[j, hd * V_ROWS:(hd + 1) * V_ROWS, :]
        acc[hd, :, q0:] = alpha * acc[hd, :, q0:] + _dot(vt, pt)
        m_scr[hd, :, q0:] = m_new

    scores(0, 0)

    def body(j):
        scores(1, j)
        accumulate(0, j)
        scores(0, j + 1)
        accumulate(1, j)

    def body_pair(jj, carry):
        body(2 * jj)
        body(2 * jj + 1)
        return carry

    lax.fori_loop(0, (n_sub // 2) * qi, body_pair, 0)
    d0 = n_sub * qi
    for d in range(n_sub):
        scores(1, d0 + d, d * tk)
        accumulate(0, d0 + d, diag=d)
        if d + 1 < n_sub:
            scores(0, d0 + d + 1, (d + 1) * tk)
        accumulate(1, d0 + d, diag=d)
    outs = []
    for hd in range(2):
        a = acc[hd]
        outs.append(a[:MLA_V, :] / a[MLA_V:MLA_V + 1, :])
    o_ref[...] = jnp.concatenate(outs, axis=0).T.astype(BF16)


def _attention(q, k, vt, tq):
    bsz, seq, hw = q.shape
    n_pairs = hw // (2 * LANES)
    n_kt, _, tk = vt.shape[1:]
    assert tq % (2 * tk) == 0, "the main loop consumes kv tiles in pairs"
    return pl.pallas_call(
        _attn_kernel,
        grid=(bsz, n_pairs, seq // tq),
        in_specs=[
            pl.BlockSpec((None, tq, 2 * LANES), lambda b, h, i: (b, i, h)),
            pl.BlockSpec((None, seq, 2 * LANES), lambda b, h, i: (b, 0, h)),
            pl.BlockSpec((None, n_kt, 2 * V_ROWS, tk), lambda b, h, i: (b, 0, h, 0)),
        ],
        out_specs=pl.BlockSpec((None, tq, 2 * MLA_V), lambda b, h, i: (b, i, h)),
        out_shape=jax.ShapeDtypeStruct((bsz, seq, n_pairs * 2 * MLA_V), BF16),
        scratch_shapes=[pltpu.VMEM((2, 1, tq), F32), pltpu.VMEM((2, 1, tq), F32), pltpu.VMEM((2, V_ROWS, tq), F32),
                        pltpu.VMEM((tk, tq), F32), pltpu.VMEM((tk, tq), F32), pltpu.VMEM((2 * LANES, tq), BF16)],
        compiler_params=_cparams(("parallel", "parallel", "arbitrary")),
        name="mla_attention",
    )(q, k, vt)


def _attn_out_kernel(a_ref, x_ref, w_ref, mod_ref, g_ref, o_ref):
    h = _dot(a_ref[...], w_ref[...])
    o_ref[...] = x_ref[...] + mod_ref[2:3, :] * _rms(h, g_ref[1:2, :])


def _attn_out(a, x, wo_pad, mod_l, g_l, tm):
    bsz, seq, d = x.shape
    hw = a.shape[-1]
    return pl.pallas_call(
        _attn_out_kernel,
        grid=(bsz, seq // tm),
        in_specs=[
            pl.BlockSpec((None, tm, hw), lambda b, i: (b, i, 0)),
            pl.BlockSpec((None, tm, d), lambda b, i: (b, i, 0)),
            pl.BlockSpec((hw, d), lambda b, i: (0, 0)),
            pl.BlockSpec((None, 6, d), lambda b, i: (b, 0, 0)),
            pl.BlockSpec((4, d), lambda b, i: (0, 0)),
        ],
        out_specs=pl.BlockSpec((None, tm, d), lambda b, i: (b, i, 0)),
        out_shape=jax.ShapeDtypeStruct(x.shape, F32),
        compiler_params=_cparams(("parallel", "parallel")),
        name="mla_out",
    )(a, x, wo_pad, mod_l, g_l)


def _route(logits):
    row = lax.broadcasted_iota(jnp.int32, logits.shape, 0)
    n = logits.shape[0]
    neg = -jnp.inf
    m1 = jnp.max(logits, axis=0, keepdims=True)
    i1 = jnp.min(jnp.where(logits == m1, row, n), axis=0, keepdims=True)
    rest = jnp.where(row == i1, neg, logits)
    m2 = jnp.max(rest, axis=0, keepdims=True)
    i2 = jnp.min(jnp.where(rest == m2, row, n), axis=0, keepdims=True)
    e = jnp.exp(m2 - m1)
    w1 = 1.0 / (1.0 + e)
    w2 = e / (1.0 + e)
    first = row == i1
    second = row == i2
    comb = jnp.where(first, w1, 0.0) + jnp.where(second, w2, 0.0)
    sel = jnp.where(first, 1.0, 0.0) + jnp.where(second, 1.0, 0.0)
    return comb, sel


def _moe_kernel(x_ref, wr_ref, br_ref, tri_ref, wg_ref, wu_ref, wo_ref, mod_ref, g_ref, o_ref,
                h_scr, sel_scr, rank_scr, w_scr, acc):
    e = pl.program_id(1)
    tb = x_ref.shape[0]

    @pl.when(e == 0)
    def _():
        h = _rms(x_ref[...], g_ref[2:3, :]) * (1.0 + mod_ref[4:5, :]) + mod_ref[3:4, :]
        h_hi = h.astype(BF16)
        h_lo = (h - h_hi.astype(F32)).astype(BF16)
        nt = (((1,), (1,)), ((), ()))
        logits = (lax.dot_general(wr_ref[0], h_hi, nt, preferred_element_type=F32)
                  + lax.dot_general(wr_ref[1], h_hi, nt, preferred_element_type=F32)
                  + lax.dot_general(wr_ref[0], h_lo, nt, preferred_element_type=F32))
        n_e = sel_scr.shape[0]
        comb, sel = _route(logits[:n_e, :] + br_ref[:n_e, 0:1])
        sel_scr[...] = sel
        w_scr[...] = comb
        sel_pad = jnp.concatenate([sel, jnp.zeros_like(sel)], axis=0).astype(BF16)
        rank_scr[...] = _dot(sel_pad, tri_ref[...])[:n_e, :]
        h_scr[...] = h_hi
        acc[...] = jnp.zeros_like(acc)

    sel_row = sel_scr[pl.ds(e, 1), :]
    rank_row = rank_scr[pl.ds(e, 1), :]
    w_row = w_scr[pl.ds(e, 1), :]
    n_rows = jnp.sum(sel_row).astype(jnp.int32)

    def step(r, carry):
        slot = (lax.broadcasted_iota(jnp.int32, (MOE_SUB, tb), 0) + r * MOE_SUB).astype(F32)
        hit = rank_row == slot
        onehot = jnp.where(hit, sel_row, 0.0).astype(BF16)
        xs = _dot(onehot, h_scr[...]).astype(BF16)
        g = _dot(xs, wg_ref[...])
        u = _dot(xs, wu_ref[...])
        act = (g * _sigmoid(g) * u).astype(BF16)
        y = _dot(act, wo_ref[...])
        w_col = jnp.sum(jnp.where(hit, w_row, 0.0), axis=1, keepdims=True)
        yw = (y * w_col).astype(BF16)
        acc[...] += lax.dot_general(onehot, yw, (((0,), (0,)), ((), ())), preferred_element_type=F32)
        return carry

    lax.fori_loop(0, (n_rows + MOE_SUB - 1) // MOE_SUB, step, 0)

    @pl.when(e == pl.num_programs(1) - 1)
    def _():
        o_ref[...] = x_ref[...] + mod_ref[5:6, :] * _rms(acc[...], g_ref[3:4, :])


def _moe_layer(x, w_router, b_router, w_in, w_out, mod_l, g_l):
    bsz, seq, d = x.shape
    n_e, f, _ = w_out.shape
    tb = min(MOE_TB, seq)
    blocks_per_batch = seq // tb
    wr = jnp.pad(w_router.T, ((0, LANES - n_e), (0, 0)))
    wr_hi = wr.astype(BF16)
    wr = jnp.stack([wr_hi, (wr - wr_hi.astype(F32)).astype(BF16)])
    br = jnp.broadcast_to(jnp.pad(b_router, (0, LANES - n_e))[:, None], (LANES, LANES))
    pos = jnp.arange(tb)
    tri = (pos[:, None] < pos[None, :]).astype(BF16)
    w_in = w_in.astype(BF16)
    out = pl.pallas_call(
        _moe_kernel,
        grid=(bsz * blocks_per_batch, n_e),
        in_specs=[
            pl.BlockSpec((tb, d), lambda i, e: (i, 0)),
            pl.BlockSpec((2, LANES, d), lambda i, e: (0, 0, 0)),
            pl.BlockSpec((LANES, LANES), lambda i, e: (0, 0)),
            pl.BlockSpec((tb, tb), lambda i, e: (0, 0)),
            pl.BlockSpec((None, d, f), lambda i, e: (e, 0, 0)),
            pl.BlockSpec((None, d, f), lambda i, e: (e, 0, 1)),
            pl.BlockSpec((None, f, d), lambda i, e: (e, 0, 0)),
            pl.BlockSpec((None, 6, d), lambda i, e: (i // blocks_per_batch, 0, 0)),
            pl.BlockSpec((4, d), lambda i, e: (0, 0)),
        ],
        out_specs=pl.BlockSpec((tb, d), lambda i, e: (i, 0)),
        out_shape=jax.ShapeDtypeStruct((bsz * seq, d), F32),
        scratch_shapes=[pltpu.VMEM((tb, d), BF16), pltpu.VMEM((n_e, tb), F32), pltpu.VMEM((n_e, tb), F32),
                        pltpu.VMEM((n_e, tb), F32), pltpu.VMEM((tb, d), F32)],
        compiler_params=_cparams(("parallel", "arbitrary")),
        name="moe",
    )(x.reshape(bsz * seq, d), wr, br, tri, w_in, w_in, w_out.astype(BF16), mod_l, g_l)
    return out.reshape(bsz, seq, d)


def kernel(x, c, positions, norm_g, w_ada, b_ada, ssm_a_re, ssm_a_im, ssm_log_dt, ssm_b_re, ssm_b_im, ssm_c_re, ssm_c_im, ssm_d, ssm_w_glu, ffn_w_in, ffn_w_out, mla_w_in, mla_q_norm, mla_kv_norm, mla_w_uq, mla_w_ukv, mla_w_o, moe_w_router, moe_b_router, moe_w_in, moe_w_out):
    depth = norm_g.shape[0]
    seq = x.shape[1]
    tm = min(512, seq)
    mod = _ada(c, w_ada, b_ada)
    rope_t = None
    for i in range(depth):
        j = i // 2
        mod_l, g_l = mod[i], norm_g[i]
        if i % 2 == 0:
            ssm_w = _ssm_weights(ssm_a_re[j], ssm_a_im[j], ssm_log_dt[j], ssm_b_re[j], ssm_b_im[j],
                                 ssm_c_re[j], ssm_c_im[j], ssm_d[j])
            ucat = _ssm_pre(x, mod_l, g_l, tm)
            zcat = _ssm(ucat, *ssm_w)
            x = _ssm_post(zcat, x, ssm_w_glu[j].astype(BF16), mod_l, g_l, tm)
            f = ffn_w_out.shape[1]
            x = _ffn(x, ffn_w_in[j].astype(BF16), ffn_w_out[j].astype(BF16), mod_l, g_l, tm, f // 2)
        else:
            if rope_t is None:
                rope_t = _rope_inputs(positions)
            proj_w, wo_pad = _mla_weights(mla_w_in[j], mla_q_norm[j], mla_kv_norm[j],
                                          mla_w_uq[j], mla_w_ukv[j], mla_w_o[j])
            q, k, v = _mla_proj(x, rope_t, mod_l, g_l, proj_w, tm)
            a = _attention(q, k, v, min(ATTN_TQ, seq))
            x = _attn_out(a, x, wo_pad, mod_l, g_l, tm)
            x = _moe_layer(x, moe_w_router[j], moe_b_router[j], moe_w_in[j], moe_w_out[j], mod_l, g_l)
    return x
```

```python
import functools
import math

import jax
import jax.numpy as jnp
from jax import lax
from jax.experimental import pallas as pl
from jax.experimental.pallas import tpu as pltpu

F32 = jnp.float32
BF16 = jnp.bfloat16

RMS_EPS = 1e-6
LANES = 128
SSM_GROUP = 16
SSM_STATE = 64
SSM_L = 16
SSM_LG = LANES // SSM_GROUP
STRIDE_PAD = 4
MLA_HEADS = 16
MLA_NOPE = 64
MLA_ROPE = 32
MLA_V = 64
MLA_QK = MLA_NOPE + MLA_ROPE
ATTN_TQ = 1024
V_ROWS = 80
MLA_Q_RANK = 384
MLA_KV_RANK = 256
ROPE_THETA = 10000.0
N_EXPERTS = 8
FFN_TM = 256
MOE_TB = 1024
MOE_SUB = 144
VMEM_LIMIT = 56 * 1024 * 1024


def _cparams(sem):
    return pltpu.CompilerParams(dimension_semantics=sem, vmem_limit_bytes=VMEM_LIMIT)


def _rms(x, g):
    return x * lax.rsqrt(jnp.mean(x * x, axis=-1, keepdims=True) + RMS_EPS) * g


def _sigmoid(x):
    return 0.5 * jnp.tanh(0.5 * x) + 0.5


def _dot(a, b):
    return jnp.dot(a, b, preferred_element_type=F32)


def _ada_kernel(c_ref, w_ref, b_ref, o_ref):
    c = c_ref[...]
    c_act = c * _sigmoid(c)
    c_hi = c_act.astype(BF16)
    c_lo = (c_act - c_hi.astype(F32)).astype(BF16)
    w = w_ref[...]
    w_hi = w.astype(BF16)
    w_lo = (w - w_hi.astype(F32)).astype(BF16)
    o_ref[...] = _dot(c_hi, w_hi) + _dot(c_hi, w_lo) + _dot(c_lo, w_hi) + b_ref[...]


def _ada(c, w_ada, b_ada):
    depth, d, _ = w_ada.shape
    bsz = c.shape[0]
    rows = 16
    c_pad =jnp.pad(c, ((0, rows - bsz), (0, 0)))
    out = pl.pallas_call(
        _ada_kernel,
        grid=(depth, 6),
        in_specs=[
            pl.BlockSpec((rows, d), lambda i, j: (0, 0)),
            pl.BlockSpec((None, d, d), lambda i, j: (i, 0, j)),
            pl.BlockSpec((None, None, 1, d), lambda i, j: (i, j, 0, 0)),
        ],
        out_specs=pl.BlockSpec((None, None, rows, d), lambda i, j: (i, j, 0, 0)),
        out_shape=jax.ShapeDtypeStruct((depth, 6, rows, d), F32),
        compiler_params=_cparams(("arbitrary", "arbitrary")),
        name="ada_mod",
    )(c_pad, w_ada, b_ada.reshape(depth, 6, 1, d))
    return out[:, :, :bsz, :].transpose(0, 2, 1, 3)


def _ssm_pre_kernel(x_ref, mod_ref, g_ref, o_ref, scr):
    tm = x_ref.shape[0]
    n = tm // SSM_L
    u = _rms(x_ref[...], g_ref[0:1, :]) * (1.0 + mod_ref[1:2, :]) + mod_ref[0:1, :]
    n_lg = o_ref.shape[0]
    pitch = SSM_L + STRIDE_PAD
    for c in range(n):
        for lg in range(n_lg):
            scr[lg, c * pitch:c * pitch + SSM_L, :] = u[c * SSM_L:(c + 1) * SSM_L, lg * LANES:(lg + 1) * LANES]
    for s in range(SSM_L):
        for lg in range(n_lg):
            o_ref[lg, :, s * LANES:(s + 1) * LANES] = scr[lg, pl.ds(s, n, stride=pitch), :].astype(BF16)


def _ssm_pre(x, mod_l, g_l, tm):
    bsz, seq, d = x.shape
    n_lg = d // LANES
    return pl.pallas_call(
        _ssm_pre_kernel,
        grid=(bsz, seq // tm),
        in_specs=[
            pl.BlockSpec((None, tm, d), lambda b, i: (b, i, 0)),
            pl.BlockSpec((None, 6, d), lambda b, i: (b, 0, 0)),
            pl.BlockSpec((4, d), lambda b, i: (0, 0)),
        ],
        out_specs=pl.BlockSpec((None, n_lg, tm // SSM_L, SSM_L * LANES), lambda b, i: (b, 0, i, 0)),
        out_shape=jax.ShapeDtypeStruct((bsz, n_lg, seq // SSM_L, SSM_L * LANES), BF16),
        scratch_shapes=[pltpu.VMEM((n_lg, (tm // SSM_L) * (SSM_L + STRIDE_PAD), LANES), F32)],
        compiler_params=_cparams(("parallel", "parallel")),
        name="ssm_pre",
    )(x, mod_l, g_l)


def _gelu_tanh(y):
    return 0.5 * y * (1.0 + jnp.tanh(math.sqrt(2.0 / math.pi) * (y + 0.044715 * (y * y * y))))


def _same_group(shape, row_shift, col_shift):
    row_g = (lax.broadcasted_iota(jnp.int32, shape, 0) >> row_shift) & (SSM_LG - 1)
    col_g = (lax.broadcasted_iota(jnp.int32, shape, 1) >> col_shift) & (SSM_LG - 1)
    return row_g == col_g


def _ssm_expand_weights(xw_ref, yw_ref, cc_ref, t1_ref, t1t_ref, win_s, wout_s, m8_s, mx_s):
    hl = SSM_L // 2
    in_mask = _same_group((LANES, t1_ref.shape[1]), 4, 6)
    out_mask = _same_group((t1_ref.shape[1], LANES), 6, 4)
    k_mask = _same_group((LANES, LANES), 4, 4)
    m8_s[...] = jnp.zeros_like(m8_s)
    for s in range(SSM_L):
        rows = slice(s * LANES, (s + 1) * LANES)
        val = lax.dot_general(xw_ref[s], t1_ref[...], (((0,), (0,)), ((), ())), preferred_element_type=F32)
        win_s[rows, :] = jnp.where(in_mask, val, 0.0).astype(BF16)
        val = lax.dot_general(t1t_ref[...], yw_ref[s], (((1,), (1,)), ((), ())), preferred_element_type=F32)
        wout_s[:, rows] = jnp.where(out_mask, val, 0.0).astype(BF16)
    for tau in range(SSM_L):
        k_tau = lax.dot_general(xw_ref[SSM_L - 1 - tau], cc_ref[...], (((0,), (1,)), ((), ())),
                                preferred_element_type=F32)
        kbd = jnp.where(k_mask, k_tau, 0.0).astype(BF16)
        for a in range(hl):
            if a + tau < hl:
                m8_s[a * LANES:(a + 1) * LANES, (a + tau) * LANES:(a + tau + 1) * LANES] = kbd
            b = a + tau - hl
            if 0 <= b < hl:
                mx_s[a * LANES:(a + 1) * LANES, b * LANES:(b + 1) * LANES] = kbd


def _ssm_kernel(u_ref, xw_ref, yw_ref, cc_ref, t1_ref, t1t_ref, a_ref, d_ref, o_ref,
                win_s, wout_s, m8_s, mx_s, v_scr, h_scr):
    n_chunks = u_ref.shape[0]
    half = (SSM_L // 2) * LANES
    ns = a_ref.shape[1]

    @pl.when(pl.program_id(1) == 0)
    def _():
        _ssm_expand_weights(xw_ref, yw_ref, cc_ref, t1_ref, t1t_ref, win_s, wout_s, m8_s, mx_s)

    uc = u_ref[...]
    v_scr[...] = _dot(uc, win_s[...])
    ar = a_ref[0:1, :]
    ai = a_ref[1:2, :]

    def step(c, carry):
        hr, hi = carry
        h_scr[pl.ds(c, 1), pl.ds(0, ns)] = hr
        h_scr[pl.ds(c, 1), pl.ds(ns, ns)] = hi
        vr = v_scr[pl.ds(c, 1), pl.ds(0, ns)]
        vi = v_scr[pl.ds(c, 1), pl.ds(ns, ns)]
        return ar * hr - ai * hi + vr, ar * hi + ai * hr + vi

    zero = jnp.zeros((1, ns), F32)
    lax.fori_loop(0, n_chunks, step, (zero, zero))
    y = _dot(h_scr[...].astype(BF16), wout_s[...])
    u_lo = uc[:, :half]
    u_hi = uc[:, half:]
    m8 = m8_s[...]
    y_lo = y[:, :half] + _dot(u_lo, m8)
    y_hi = y[:, half:] + _dot(u_hi, m8) + _dot(u_lo, mx_s[...])
    dd = d_ref[...]
    o_ref[:, :half] = _gelu_tanh(y_lo + dd[:, :half] * u_lo.astype(F32)).astype(BF16)
    o_ref[:, half:] = _gelu_tanh(y_hi + dd[:, half:] * u_hi.astype(F32)).astype(BF16)


def _ssm(ucat, xw, yw, cc, t1, t1t, a16, dcat):
    bsz, n_lg, n_chunks, width = ucat.shape
    ns2 = t1.shape[-1]
    half = width // 2

    def per_lg(a):
        return pl.BlockSpec((None,) + a.shape[1:], lambda g, b: (g,) + (0,) * (a.ndim - 1))

    def const(a):
        return pl.BlockSpec(a.shape, lambda g, b: (0, 0))

    return pl.pallas_call(
        _ssm_kernel,
        grid=(n_lg, bsz),
        in_specs=[
            pl.BlockSpec((None, None, n_chunks, width), lambda g, b: (b, g, 0, 0)),
            per_lg(xw), per_lg(yw), per_lg(cc), const(t1), const(t1t), per_lg(a16), per_lg(dcat),
        ],
        out_specs=pl.BlockSpec((None, None, n_chunks, width), lambda g, b: (b, g, 0, 0)),
        out_shape=jax.ShapeDtypeStruct(ucat.shape, BF16),
        scratch_shapes=[
            pltpu.VMEM((width, ns2), BF16), pltpu.VMEM((ns2, width), BF16),
            pltpu.VMEM((half, half), BF16), pltpu.VMEM((half, half), BF16),
            pltpu.VMEM((n_chunks, ns2), F32), pltpu.VMEM((n_chunks, ns2), F32),
        ],
        compiler_params=_cparams(("arbitrary", "arbitrary")),
        name="ssm_scan",
    )(ucat, xw, yw, cc, t1, t1t, a16, dcat)


def _ssm_post_kernel(z_ref, x_ref, w_ref, mod_ref, g_ref, o_ref, zp, rp):
    n_lg, n, _ = z_ref.shape
    d = x_ref.shape[1]
    for s in range(SSM_L):
        for lg in range(n_lg):
            zp[s * n:(s + 1) * n, lg * LANES:(lg + 1) * LANES] = z_ref[lg, :, s * LANES:(s + 1) * LANES]
    gl = _dot(zp[...], w_ref[...])
    h = gl[:, :d] * _sigmoid(gl[:, d:])
    r = mod_ref[2:3, :] * _rms(h, g_ref[1:2, :])
    pitch = n + STRIDE_PAD
    for s in range(SSM_L):
        for lg in range(n_lg):
            rp[lg, s * pitch:s * pitch + n, :] = r[s * n:(s + 1) * n, lg * LANES:(lg + 1) * LANES]
    sub = 8
    for c in range(n):
        for k in range(SSM_L // sub):
            rows = pl.ds(c * SSM_L + k * sub, sub)
            for lg in range(n_lg):
                cols = slice(lg * LANES, (lg + 1) * LANES)
                o_ref[rows, cols] = x_ref[rows, cols] + rp[lg, pl.ds(k * sub * pitch + c, sub, stride=pitch), :]


def _ssm_post(zcat, x, w_glu, mod_l, g_l, tm):
    bsz, seq, d = x.shape
    n_lg = d // LANES
    return pl.pallas_call(
        _ssm_post_kernel,
        grid=(bsz, seq // tm),
        in_specs=[
            pl.BlockSpec((None, n_lg, tm // SSM_L, SSM_L * LANES), lambda b, i: (b, 0, i, 0)),
            pl.BlockSpec((None, tm, d), lambda b, i: (b, i, 0)),
            pl.BlockSpec((d, 2 * d), lambda b, i: (0, 0)),
            pl.BlockSpec((None, 6, d), lambda b, i: (b, 0, 0)),
            pl.BlockSpec((4, d), lambda b, i: (0, 0)),
        ],
        out_specs=pl.BlockSpec((None, tm, d), lambda b, i: (b, i, 0)),
        out_shape=jax.ShapeDtypeStruct(x.shape, F32),
        scratch_shapes=[pltpu.VMEM((tm, d), BF16),
                        pltpu.VMEM((n_lg, SSM_L * (tm // SSM_L + STRIDE_PAD), LANES), F32)],
        compiler_params=_cparams(("parallel", "parallel")),
        name="ssm_post",
    )(zcat, x, w_glu, mod_l, g_l)


def _ssm_weights(a_re, a_im, log_dt, b_re, b_im, c_re, c_im, d_skip):
    n_groups = a_re.shape[0]
    n_lg = n_groups // SSM_LG
    p, hh, ll = SSM_STATE, SSM_GROUP, SSM_L
    dt = jnp.exp(log_dt)[:, None]
    k = jnp.arange(ll + 1, dtype=F32)[:, None, None]
    mag = jnp.exp(k * (a_re * dt))
    pw_re = mag * jnp.cos(k * (a_im * dt))
    pw_im = mag * jnp.sin(k * (a_im * dt))
    den = a_re * a_re + a_im * a_im
    num_re, num_im = pw_re[1] - 1.0, pw_im[1]
    f_re = (num_re * a_re + num_im * a_im) / den
    f_im = (num_im * a_re - num_re * a_im) / den
    bb_re = f_re[..., None] * b_re - f_im[..., None] * b_im
    bb_im = f_re[..., None] * b_im + f_im[..., None] * b_re
    b2_re, b2_im = (x.reshape(n_lg, SSM_LG, p, hh).transpose(0, 2, 1, 3).reshape(n_lg, p, LANES)
                    for x in (bb_re, bb_im))

    def lanes_g(x):
        x = x.reshape(x.shape[0], n_lg, SSM_LG, p).transpose(0, 1, 3, 2)
        return jnp.repeat(x, hh, axis=-1)

    l_re, l_im = lanes_g(pw_re[:ll][::-1]), lanes_g(pw_im[:ll][::-1])
    xw = jnp.concatenate([l_re * b2_re - l_im * b2_im, l_re * b2_im + l_im * b2_re], axis=2)
    xw = xw.transpose(1, 0, 2, 3)
    cp_re = c_re[None] * pw_re[1:, :, None, :] - c_im[None] * pw_im[1:, :, None, :]
    cp_im = c_re[None] * pw_im[1:, :, None, :] + c_im[None] * pw_re[1:, :, None, :]
    yw = jnp.concatenate([cp_re, -cp_im], axis=-1).reshape(ll, n_lg, LANES, 2 * p).transpose(1, 0, 2, 3)
    cc = jnp.concatenate([c_re, -c_im], axis=-1).reshape(n_lg, LANES, 2 * p)
    j1 = jnp.arange(2 * SSM_LG * p)
    t1 = (jnp.arange(2 * p)[:, None] == ((j1 // (SSM_LG * p)) * p + j1 % p)[None, :]).astype(BF16)
    a16 = jnp.stack([pw_re[ll], pw_im[ll]], axis=0).reshape(2, n_lg, SSM_LG * p).transpose(1, 0, 2)
    dcat = jnp.tile(d_skip.reshape(n_lg, 1, LANES), (1, 1, ll))
    return xw.astype(BF16), yw.astype(BF16), cc.astype(BF16), t1, t1.T, a16, dcat


def _ffn_kernel(x_ref, wg_ref, wu_ref, wo_ref, mod_ref, g_ref, o_ref, h_scr, acc):
    k = pl.program_id(2)

    @pl.when(k == 0)
    def _():
        h = _rms(x_ref[...], g_ref[2:3, :]) * (1.0 + mod_ref[4:5, :]) + mod_ref[3:4, :]
        h_scr[...] = h.astype(BF16)
        acc[...] = jnp.zeros_like(acc)

    h = h_scr[...]
    g = _dot(h, wg_ref[...])
    u = _dot(h, wu_ref[...])
    act = (g * _sigmoid(g) * u).astype(BF16)
    acc[...] += _dot(act, wo_ref[...])

    @pl.when(k == pl.num_programs(2) - 1)
    def _():
        o_ref[...] = x_ref[...] + mod_ref[5:6, :] * _rms(acc[...], g_ref[3:4, :])


def _ffn(x, w_in, w_out, mod_l, g_l, tm, tf):
    bsz, seq, d = x.shape
    f = w_out.shape[0]
    nk = f // tf
    return pl.pallas_call(
        _ffn_kernel,
        grid=(bsz, seq // tm, nk),
        in_specs=[
            pl.BlockSpec((None, tm, d), lambda b, i, k: (b, i, 0)),
            pl.BlockSpec((d, tf), lambda b, i, k: (0, k)),
            pl.BlockSpec((d, tf), lambda b, i, k: (0, k + nk)),
            pl.BlockSpec((tf, d), lambda b, i, k: (k, 0)),
            pl.BlockSpec((None, 6, d), lambda b, i, k: (b, 0, 0)),
            pl.BlockSpec((4, d), lambda b, i, k: (0, 0)),
        ],
        out_specs=pl.BlockSpec((None, tm, d), lambda b, i, k: (b, i, 0)),
        out_shape=jax.ShapeDtypeStruct(x.shape, F32),
        scratch_shapes=[pltpu.VMEM((tm, d), BF16), pltpu.VMEM((tm, d), F32)],
        compiler_params=_cparams(("parallel", "parallel", "arbitrary")),
        name="ffn",
    )(x, w_in, w_in, w_out, mod_l, g_l)


def _rope_inputs(positions):
    bsz, seq = positions.shape
    half = MLA_ROPE // 2
    per_row = LANES // half
    inv_freq = ROPE_THETA ** (-jnp.arange(half, dtype=F32) / half)
    pos_rep = jnp.broadcast_to(positions[..., None], (bsz, seq, half)).reshape(bsz, seq // per_row, LANES)
    return pos_rep, jnp.tile(inv_freq, per_row).reshape(1, LANES)


def _rope_tables(pos_ref, f_ref, cos_scr, sin_lo_scr, sin_hi_scr):
    half = MLA_ROPE // 2
    per_row = LANES // half
    ang = pos_ref[...].astype(F32) * f_ref[...]
    cos_d = jnp.cos(ang)
    sin_d = jnp.sin(ang)
    n = ang.shape[0]
    lane = lax.broadcasted_iota(jnp.int32, ang.shape, 1)

    def onto(t, first_lane, q):
        shift = (first_lane - half * q) % LANES
        return t if shift == 0 else pltpu.roll(t, shift, axis=1)

    for q in range(per_row):
        rows = pl.ds(q, n, stride=per_row)
        x1, x2 = MLA_NOPE, MLA_NOPE + half
        cos_scr[rows, :] = jnp.where(lane < x1, 1.0, jnp.where(lane < x2, onto(cos_d, x1, q),
                                     jnp.where(lane < MLA_QK, onto(cos_d, x2, q), 0.0)))
        sin_lo_scr[rows, :] = jnp.where(lane < x1, 0.0, jnp.where(lane < x2, -onto(sin_d, x1, q), 0.0))
        sin_hi_scr[rows, :] = jnp.where(lane < x2, 0.0, jnp.where(lane < MLA_QK, onto(sin_d, x2, q), 0.0))


def _mla_proj_kernel(x_ref, pos_ref, f_ref, mod_ref, g_ref, wcq_ref, wckv_ref, wkr_ref,
                     qn_ref, kvn_ref, wq_ref, wk_ref, wv_ref, q_ref, k_ref, v_ref,
                     cos_scr, sin_lo_scr, sin_hi_scr):
    h = (_rms(x_ref[...], g_ref[0:1, :]) * (1.0 + mod_ref[1:2, :]) + mod_ref[0:1, :]).astype(BF16)
    cq = _rms(_dot(h, wcq_ref[...]), qn_ref[...]).astype(BF16)
    ckv = _rms(_dot(h, wckv_ref[...]), kvn_ref[...]).astype(BF16)
    kr = _dot(h, wkr_ref[...])
    _rope_tables(pos_ref, f_ref, cos_scr, sin_lo_scr, sin_hi_scr)
    cos = cos_scr[...]
    sin_lo = sin_lo_scr[...]
    sin_hi = sin_hi_scr[...]
    half = MLA_ROPE // 2

    def rope(t):
        width = t.shape[1]
        up = pltpu.roll(t, half, axis=1)
        down = pltpu.roll(t, width - half, axis=1)
        return [t[:, s:s + LANES] * cos + down[:, s:s + LANES] * sin_lo + up[:, s:s + LANES] * sin_hi
                for s in range(0, width, LANES)]

    q = _dot(cq, wq_ref[...])
    for hd, q_hd in enumerate(rope(q)):
        q_ref[:, hd * LANES:(hd + 1) * LANES] = q_hd.astype(BF16)
    kr_rot = rope(kr)[0]
    k = _dot(ckv, wk_ref[...])
    for hd in range(MLA_HEADS):
        sl = slice(hd * LANES, (hd + 1) * LANES)
        k_ref[:, sl] = (k[:, sl] + kr_rot).astype(BF16)
    vt = lax.dot_general(wv_ref[...], ckv, (((1,), (1,)), ((), ())), preferred_element_type=F32)
    row = lax.broadcasted_iota(jnp.int32, vt.shape, 0)
    v_ref[...] = jnp.where(row % V_ROWS == MLA_V, 1.0, vt).astype(BF16)


def _mla_proj(x, rope_t, mod_l, g_l, wts, tm):
    bsz, seq, d = x.shape
    hw = MLA_HEADS * LANES

    def full(a):
        return pl.BlockSpec(a.shape, lambda b, i: (0,) * a.ndim)

    out_sds = jax.ShapeDtypeStruct((bsz, seq, hw), BF16)
    vt_rows = MLA_HEADS * V_ROWS
    vt_sds = jax.ShapeDtypeStruct((bsz, seq // tm, vt_rows, tm), BF16)
    return pl.pallas_call(
        _mla_proj_kernel,
        grid=(bsz, seq // tm),
        in_specs=[
            pl.BlockSpec((None, tm, d), lambda b, i: (b, i, 0)),
            pl.BlockSpec((None, tm // (LANES // (MLA_ROPE // 2)), LANES), lambda b, i: (b, i, 0)),
            pl.BlockSpec((1, LANES), lambda b, i: (0, 0)),
            pl.BlockSpec((None, 6, d), lambda b, i: (b, 0, 0)),
            pl.BlockSpec((4, d), lambda b, i: (0, 0)),
        ] + [full(w) for w in wts],
        out_specs=[pl.BlockSpec((None, tm, hw), lambda b, i: (b, i, 0))] * 2
        + [pl.BlockSpec((None, None, vt_rows, tm), lambda b, i: (b, i, 0, 0))],
        out_shape=[out_sds, out_sds, vt_sds],
        scratch_shapes=[pltpu.VMEM((tm, LANES), F32)] * 3,
        compiler_params=_cparams(("parallel", "parallel")),
        name="mla_proj",
    )(x, *rope_t, mod_l, g_l, *wts)


def _mla_weights(w_in, q_norm, kv_norm, w_uq, w_ukv, w_o):
    pad = LANES - MLA_QK
    w_cq = w_in[:, :MLA_Q_RANK]
    w_ckv = w_in[:, MLA_Q_RANK:MLA_Q_RANK + MLA_KV_RANK]
    w_kr = jnp.pad(w_in[:, MLA_Q_RANK + MLA_KV_RANK:], ((0, 0), (MLA_NOPE, pad)))
    scale = MLA_QK ** -0.5 * math.log2(math.e)
    wq = w_uq.reshape(MLA_Q_RANK, MLA_HEADS, MLA_QK) * scale
    wq_pad = jnp.pad(wq, ((0, 0), (0, 0), (0, pad))).reshape(MLA_Q_RANK, MLA_HEADS * LANES)
    wkv =w_ukv.reshape(MLA_KV_RANK, MLA_HEADS, MLA_NOPE + MLA_V)
    zk = jnp.zeros((MLA_KV_RANK, MLA_HEADS, LANES - MLA_NOPE), F32)
    wk_pad = jnp.concatenate([wkv[..., :MLA_NOPE], zk], axis=-1).reshape(MLA_KV_RANK, MLA_HEADS * LANES)
    wv_t = jnp.pad(wkv[..., MLA_NOPE:].transpose(1, 2, 0), ((0, 0), (0, V_ROWS - MLA_V), (0, 0)))
    wv_t = wv_t.reshape(MLA_HEADS * V_ROWS, MLA_KV_RANK)
    proj_w = (w_cq.astype(BF16), w_ckv.astype(BF16), w_kr.astype(BF16),
              q_norm.reshape(1, -1), kv_norm.reshape(1, -1),
              wq_pad.astype(BF16), wk_pad.astype(BF16), wv_t.astype(BF16))
    return proj_w, w_o.astype(BF16)


def _attn_kernel(q_ref, k_ref, vt_ref, o_ref, m_scr, mt_scr, acc, s0_scr, s1_scr, qt_scr):
    tk = vt_ref.shape[2]
    n_sub = q_ref.shape[0] // tk
    qi = pl.program_id(2)
    s_scr = (s0_scr, s1_scr)
    qt_scr[...] = q_ref[...].astype(F32).T.astype(BF16)
    m_scr[...] = jnp.full_like(m_scr, -jnp.inf)
    acc[...] = jnp.zeros_like(acc)

    def scores(hd, j, q0=0):
        lanes = slice(hd * LANES, (hd + 1) * LANES)
        k = k_ref[pl.ds(pl.multiple_of(j * tk, tk), tk), lanes]
        st = _dot(k, qt_scr[lanes, q0:])
        s_scr[hd][:, q0:] = st
        mt_scr[hd, :, q0:] = jnp.max(st, axis=0, keepdims=True)

    def accumulate(hd, j, diag=None):
        q0 = 0 if diag is None else diag * tk
        st = s_scr[hd][:, q0:]
        if diag is not None:
            key = lax.broadcasted_iota(jnp.int32, st.shape, 0)
            qry = lax.broadcasted_iota(jnp.int32, st.shape, 1)
            st = jnp.where(key <= qry, st, -jnp.inf)
            mt = jnp.max(st, axis=0, keepdims=True)
        else:
            mt = mt_scr[hd]
        m_prev = m_scr[hd, :, q0:]
        m_new = jnp.maximum(m_prev, mt)
        alpha = jnp.exp2(m_prev - m_new)
        pt = jnp.exp2(st - m_new).astype(BF16)
        vt = vt_ref[j, hd * V_ROWS:(hd + 1) * V_ROWS, :]
        acc[hd, :, q0:] = alpha * acc[hd, :, q0:] + _dot(vt, pt)
        m_scr[hd, :, q0:] = m_new

    scores(0, 0)

    def body(j):
        scores(1, j)
        accumulate(0, j)
        scores(0, j + 1)
        accumulate(1, j)

    def body_pair(jj, carry):
        body(2 * jj)
        body(2 * jj + 1)
        return carry

    lax.fori_loop(0, (n_sub // 2) * qi, body_pair, 0)
    d0 = n_sub * qi
    for d in range(n_sub):
        scores(1, d0 + d, d * tk)
        accumulate(0, d0 + d, diag=d)
        if d + 1 < n_sub:
            scores(0, d0 + d + 1, (d + 1) * tk)
        accumulate(1, d0 + d, diag=d)
    outs = []
    for hd in range(2):
        a = acc[hd]
        outs.append(a[:MLA_V, :] / a[MLA_V:MLA_V + 1, :])
    o_ref[...] = jnp.concatenate(outs, axis=0).T.astype(BF16)


def _attention(q, k, vt, tq):
    bsz, seq, hw = q.shape
    n_pairs = hw // (2 * LANES)
    n_kt, _, tk = vt.shape[1:]
    assert tq % (2 * tk) == 0, "the main loop consumes kv tiles in pairs"
    return pl.pallas_call(
        _attn_kernel,
        grid=(bsz, n_pairs, seq // tq),
        in_specs=[
            pl.BlockSpec((None, tq, 2 * LANES), lambda b, h, i: (b, i, h)),
            pl.BlockSpec((None, seq, 2 * LANES), lambda b, h, i: (b, 0, h)),
            pl.BlockSpec((None, n_kt, 2 * V_ROWS, tk), lambda b, h, i: (b, 0, h, 0)),
        ],
        out_specs=pl.BlockSpec((None, tq, 2 * MLA_V), lambda b, h, i: (b, i, h)),
        out_shape=jax.ShapeDtypeStruct((bsz, seq, n_pairs * 2 * MLA_V), BF16),
        scratch_shapes=[pltpu.VMEM((2, 1, tq), F32), pltpu.VMEM((2, 1, tq), F32), pltpu.VMEM((2, V_ROWS, tq), F32),
                        pltpu.VMEM((tk, tq), F32), pltpu.VMEM((tk, tq), F32), pltpu.VMEM((2 * LANES, tq), BF16)],
        compiler_params=_cparams(("parallel", "parallel", "arbitrary")),
        name="mla_attention",
    )(q, k, vt)


def _attn_out_kernel(a_ref, x_ref, w_ref, mod_ref, g_ref, o_ref):
    h = _dot(a_ref[...], w_ref[...])
    o_ref[...] = x_ref[...] + mod_ref[2:3, :] * _rms(h, g_ref[1:2, :])


def _attn_out(a, x, wo_pad, mod_l, g_l, tm):
    bsz, seq, d = x.shape
    hw = a.shape[-1]
    return pl.pallas_call(
        _attn_out_kernel,
        grid=(bsz, seq // tm),
        in_specs=[
            pl.BlockSpec((None, tm, hw), lambda b, i: (b, i, 0)),
            pl.BlockSpec((None, tm, d), lambda b, i: (b, i, 0)),
            pl.BlockSpec((hw, d), lambda b, i: (0, 0)),
            pl.BlockSpec((None, 6, d), lambda b, i: (b, 0, 0)),
            pl.BlockSpec((4, d), lambda b, i: (0, 0)),
        ],
        out_specs=pl.BlockSpec((None, tm, d), lambda b, i: (b, i, 0)),
        out_shape=jax.ShapeDtypeStruct(x.shape, F32),
        compiler_params=_cparams(("parallel", "parallel")),
        name="mla_out",
    )(a, x, wo_pad, mod_l, g_l)


def _route(logits):
    row = lax.broadcasted_iota(jnp.int32, logits.shape, 0)
    n = logits.shape[0]
    neg = -jnp.inf
    m1 = jnp.max(logits, axis=0, keepdims=True)
    i1 = jnp.min(jnp.where(logits == m1, row, n), axis=0, keepdims=True)
    rest = jnp.where(row == i1, neg, logits)
    m2 = jnp.max(rest, axis=0, keepdims=True)
    i2 = jnp.min(jnp.where(rest == m2, row, n), axis=0, keepdims=True)
    e = jnp.exp(m2 - m1)
    w1 = 1.0 / (1.0 + e)
    w2 = e / (1.0 + e)
    first = row == i1
    second = row == i2
    comb = jnp.where(first, w1, 0.0) + jnp.where(second, w2, 0.0)
    sel = jnp.where(first, 1.0, 0.0) + jnp.where(second, 1.0, 0.0)
    return comb, sel


def _moe_kernel(x_ref, wr_ref, br_ref, tri_ref, wg_ref, wu_ref, wo_ref, mod_ref, g_ref, o_ref,
                h_scr, sel_scr, rank_scr, w_scr, acc):
    e = pl.program_id(1)
    tb = x_ref.shape[0]

    @pl.when(e == 0)
    def _():
        h = _rms(x_ref[...], g_ref[2:3, :]) * (1.0 + mod_ref[4:5, :]) + mod_ref[3:4, :]
        h_hi = h.astype(BF16)
        h_lo = (h - h_hi.astype(F32)).astype(BF16)
        nt = (((1,), (1,)), ((), ()))
        logits = (lax.dot_general(wr_ref[0], h_hi, nt, preferred_element_type=F32)
                  + lax.dot_general(wr_ref[1], h_hi, nt, preferred_element_type=F32)
                  + lax.dot_general(wr_ref[0], h_lo, nt, preferred_element_type=F32))
        n_e = sel_scr.shape[0]
        comb, sel = _route(logits[:n_e, :] + br_ref[:n_e, 0:1])
        sel_scr[...] = sel
        w_scr[...] = comb
        sel_pad = jnp.concatenate([sel, jnp.zeros_like(sel)], axis=0).astype(BF16)
        rank_scr[...] = _dot(sel_pad, tri_ref[...])[:n_e, :]
        h_scr[...] = h_hi
        acc[...] = jnp.zeros_like(acc)

    sel_row = sel_scr[pl.ds(e, 1), :]
    rank_row = rank_scr[pl.ds(e, 1), :]
    w_row = w_scr[pl.ds(e, 1), :]
    n_rows = jnp.sum(sel_row).astype(jnp.int32)

    def step(r, carry):
        slot = (lax.broadcasted_iota(jnp.int32, (MOE_SUB, tb), 0) + r * MOE_SUB).astype(F32)
        hit = rank_row == slot
        onehot = jnp.where(hit, sel_row, 0.0).astype(BF16)
        xs = _dot(onehot, h_scr[...]).astype(BF16)
        g = _dot(xs, wg_ref[...])
        u = _dot(xs, wu_ref[...])
        act = (g * _sigmoid(g) * u).astype(BF16)
        y = _dot(act, wo_ref[...])
        w_col = jnp.sum(jnp.where(hit, w_row, 0.0), axis=1, keepdims=True)
        yw = (y * w_col).astype(BF16)
        acc[...] += lax.dot_general(onehot, yw, (((0,), (0,)), ((), ())), preferred_element_type=F32)
        return carry

    lax.fori_loop(0, (n_rows + MOE_SUB - 1) // MOE_SUB, step, 0)

    @pl.when(e == pl.num_programs(1) - 1)
    def _():
        o_ref[...] = x_ref[...] + mod_ref[5:6, :] * _rms(acc[...], g_ref[3:4, :])


def _moe_layer(x, w_router, b_router, w_in, w_out, mod_l, g_l):
    bsz, seq, d = x.shape
    n_e, f, _ = w_out.shape
    tb = min(MOE_TB, seq)
    blocks_per_batch = seq // tb
    wr = jnp.pad(w_router.T, ((0, LANES - n_e), (0, 0)))
    wr_hi = wr.astype(BF16)
    wr = jnp.stack([wr_hi, (wr - wr_hi.astype(F32)).astype(BF16)])
    br = jnp.broadcast_to(jnp.pad(b_router, (0, LANES - n_e))[:, None], (LANES, LANES))
    pos = jnp.arange(tb)
    tri = (pos[:, None] < pos[None, :]).astype(BF16)
    w_in = w_in.astype(BF16)
    out = pl.pallas_call(
        _moe_kernel,
        grid=(bsz * blocks_per_batch, n_e),
        in_specs=[
            pl.BlockSpec((tb, d), lambda i, e: (i, 0)),
            pl.BlockSpec((2, LANES, d), lambda i, e: (0, 0, 0)),
            pl.BlockSpec((LANES, LANES), lambda i, e: (0, 0)),
            pl.BlockSpec((tb, tb), lambda i, e: (0, 0)),
            pl.BlockSpec((None, d, f), lambda i, e: (e, 0, 0)),
            pl.BlockSpec((None, d, f), lambda i, e: (e, 0, 1)),
            pl.BlockSpec((None, f, d), lambda i, e: (e, 0, 0)),
            pl.BlockSpec((None, 6, d), lambda i, e: (i // blocks_per_batch, 0, 0)),
            pl.BlockSpec((4, d), lambda i, e: (0, 0)),
        ],
        out_specs=pl.BlockSpec((tb, d), lambda i, e: (i, 0)),
        out_shape=jax.ShapeDtypeStruct((bsz * seq, d), F32),
        scratch_shapes=[pltpu.VMEM((tb, d), BF16), pltpu.VMEM((n_e, tb), F32), pltpu.VMEM((n_e, tb), F32),
                        pltpu.VMEM((n_e, tb), F32), pltpu.VMEM((tb, d), F32)],
        compiler_params=_cparams(("parallel", "arbitrary")),
        name="moe",
    )(x.reshape(bsz * seq, d), wr, br, tri, w_in, w_in, w_out.astype(BF16), mod_l, g_l)
    return out.reshape(bsz, seq, d)


def kernel(x, c, positions, norm_g, w_ada, b_ada, ssm_a_re, ssm_a_im, ssm_log_dt, ssm_b_re, ssm_b_im, ssm_c_re, ssm_c_im, ssm_d, ssm_w_glu, ffn_w_in, ffn_w_out, mla_w_in, mla_q_norm, mla_kv_norm, mla_w_uq, mla_w_ukv, mla_w_o, moe_w_router, moe_b_router, moe_w_in, moe_w_out):
    depth = norm_g.shape[0]
    seq = x.shape[1]
    tm = min(512, seq)
    mod = _ada(c, w_ada, b_ada)
    rope_t = None
    for i in range(depth):
        j = i // 2
        mod_l, g_l = mod[i], norm_g[i]
        if i % 2 == 0:
            ssm_w = _ssm_weights(ssm_a_re[j], ssm_a_im[j], ssm_log_dt[j], ssm_b_re[j], ssm_b_im[j],
                                 ssm_c_re[j], ssm_c_im[j], ssm_d[j])
            ucat = _ssm_pre(x, mod_l, g_l, tm)
            zcat = _ssm(ucat, *ssm_w)
            x = _ssm_post(zcat, x, ssm_w_glu[j].astype(BF16), mod_l, g_l, tm)
            f = ffn_w_out.shape[1]
            x = _ffn(x, ffn_w_in[j].astype(BF16), ffn_w_out[j].astype(BF16), mod_l, g_l, min(FFN_TM, seq), f)
        else:
            if rope_t is None:
                rope_t = _rope_inputs(positions)
            proj_w, wo_pad = _mla_weights(mla_w_in[j], mla_q_norm[j], mla_kv_norm[j],
                                          mla_w_uq[j], mla_w_ukv[j], mla_w_o[j])
            q, k, v = _mla_proj(x, rope_t, mod_l, g_l, proj_w, tm)
            a = _attention(q, k, v, min(ATTN_TQ, seq))
            x = _attn_out(a, x, wo_pad, mod_l, g_l, tm)
            x = _moe_layer(x, moe_w_router[j], moe_b_router[j], moe_w_in[j], moe_w_out[j], mod_l, g_l)
    return x
```

```python
import functools
import math

import jax
import jax.numpy as jnp
from jax import lax
from jax.experimental import pallas as pl
from jax.experimental.pallas import tpu as pltpu

F32 = jnp.float32
BF16 = jnp.bfloat16

RMS_EPS = 1e-6
LANES = 128
SSM_GROUP = 16
SSM_STATE = 64
SSM_L = 16
SSM_LG = LANES // SSM_GROUP
STRIDE_PAD = 4
MLA_HEADS = 16
MLA_NOPE = 64
MLA_ROPE = 32
MLA_V = 64
MLA_QK = MLA_NOPE + MLA_ROPE
ATTN_TQ = 1024
V_ROWS = 80
MLA_Q_RANK = 384
MLA_KV_RANK = 256
ROPE_THETA = 10000.0
N_EXPERTS = 8
FFN_TM = 256
MOE_TB = 1024
MOE_SUB = 144
VMEM_LIMIT = 56 * 1024 * 1024


def _cparams(sem):
    return pltpu.CompilerParams(dimension_semantics=sem, vmem_limit_bytes=VMEM_LIMIT)


def _rms(x, g):
    return x * lax.rsqrt(jnp.mean(x * x, axis=-1, keepdims=True) + RMS_EPS) * g


def _sigmoid(x):
    return 0.5 * jnp.tanh(0.5 * x) + 0.5


def _dot(a, b):
    return jnp.dot(a, b, preferred_element_type=F32)


def _ada_kernel(c_ref, w_ref, b_ref, o_ref):
    c = c_ref[...]
    c_act = c * _sigmoid(c)
    c_hi = c_act.astype(BF16)
    c_lo = (c_act - c_hi.astype(F32)).astype(BF16)
    w = w_ref[...]
    w_hi = w.astype(BF16)
    w_lo = (w - w_hi.astype(F32)).astype(BF16)
    o_ref[...] = _dot(c_hi, w_hi) + _dot(c_hi, w_lo) + _dot(c_lo, w_hi) + b_ref[...]


def _ada(c, w_ada, b_ada):
    depth, d, _ = w_ada.shape
    bsz = c.shape[0]
    rows = 16
    c_pad =jnp.pad(c, ((0, rows - bsz), (0, 0)))
    out = pl.pallas_call(
        _ada_kernel,
        grid=(depth, 6),
        in_specs=[
            pl.BlockSpec((rows, d), lambda i, j: (0, 0)),
            pl.BlockSpec((None, d, d), lambda i, j: (i, 0, j)),
            pl.BlockSpec((None, None, 1, d), lambda i, j: (i, j, 0, 0)),
        ],
        out_specs=pl.BlockSpec((None, None, rows, d), lambda i, j: (i, j, 0, 0)),
        out_shape=jax.ShapeDtypeStruct((depth, 6, rows, d), F32),
        compiler_params=_cparams(("arbitrary", "arbitrary")),
        name="ada_mod",
    )(c_pad, w_ada, b_ada.reshape(depth, 6, 1, d))
    return out[:, :, :bsz, :].transpose(0, 2, 1, 3)


def _ssm_pre_kernel(x_ref, mod_ref, g_ref, o_ref, scr):
    tm = x_ref.shape[0]
    n = tm // SSM_L
    u = _rms(x_ref[...], g_ref[0:1, :]) * (1.0 + mod_ref[1:2, :]) + mod_ref[0:1, :]
    n_lg = o_ref.shape[0]
    pitch = SSM_L + STRIDE_PAD
    for c in range(n):
        for lg in range(n_lg):
            scr[lg, c * pitch:c * pitch + SSM_L, :] = u[c * SSM_L:(c + 1) * SSM_L, lg * LANES:(lg + 1) * LANES]
    for s in range(SSM_L):
        for lg in range(n_lg):
            o_ref[lg, :, s * LANES:(s + 1) * LANES] = scr[lg, pl.ds(s, n, stride=pitch), :].astype(BF16)


def _ssm_pre(x, mod_l, g_l, tm):
    bsz, seq, d = x.shape
    n_lg = d // LANES
    return pl.pallas_call(
        _ssm_pre_kernel,
        grid=(bsz, seq // tm),
        in_specs=[
            pl.BlockSpec((None, tm, d), lambda b, i: (b, i, 0)),
            pl.BlockSpec((None, 6, d), lambda b, i: (b, 0, 0)),
            pl.BlockSpec((4, d), lambda b, i: (0, 0)),
        ],
        out_specs=pl.BlockSpec((None, n_lg, tm // SSM_L, SSM_L * LANES), lambda b, i: (b, 0, i, 0)),
        out_shape=jax.ShapeDtypeStruct((bsz, n_lg, seq // SSM_L, SSM_L * LANES), BF16),
        scratch_shapes=[pltpu.VMEM((n_lg, (tm // SSM_L) * (SSM_L + STRIDE_PAD), LANES), F32)],
        compiler_params=_cparams(("parallel", "parallel")),
        name="ssm_pre",
    )(x, mod_l, g_l)


def _gelu_tanh(y):
    return 0.5 * y * (1.0 + jnp.tanh(math.sqrt(2.0 / math.pi) * (y + 0.044715 * (y * y * y))))


def _same_group(shape, row_shift, col_shift):
    row_g = (lax.broadcasted_iota(jnp.int32, shape, 0) >> row_shift) & (SSM_LG - 1)
    col_g = (lax.broadcasted_iota(jnp.int32, shape, 1) >> col_shift) & (SSM_LG - 1)
    return row_g == col_g


def _ssm_expand_weights(xw_ref, yw_ref, cc_ref, t1_ref, t1t_ref, win_s, wout_s, m8_s, mx_s):
    hl = SSM_L // 2
    in_mask = _same_group((LANES, t1_ref.shape[1]), 4, 6)
    out_mask = _same_group((t1_ref.shape[1], LANES), 6, 4)
    k_mask = _same_group((LANES, LANES), 4, 4)
    m8_s[...] = jnp.zeros_like(m8_s)
    for s in range(SSM_L):
        rows = slice(s * LANES, (s + 1) * LANES)
        val = lax.dot_general(xw_ref[s], t1_ref[...], (((0,), (0,)), ((), ())), preferred_element_type=F32)
        win_s[rows, :] = jnp.where(in_mask, val, 0.0).astype(BF16)
        val = lax.dot_general(t1t_ref[...], yw_ref[s], (((1,), (1,)), ((), ())), preferred_element_type=F32)
        wout_s[:, rows] = jnp.where(out_mask, val, 0.0).astype(BF16)
    for tau in range(SSM_L):
        k_tau = lax.dot_general(xw_ref[SSM_L - 1 - tau], cc_ref[...], (((0,), (1,)), ((), ())),
                                preferred_element_type=F32)
        kbd = jnp.where(k_mask, k_tau, 0.0).astype(BF16)
        for a in range(hl):
            if a + tau < hl:
                m8_s[a * LANES:(a + 1) * LANES, (a + tau) * LANES:(a + tau + 1) * LANES] = kbd
            b = a + tau - hl
            if 0 <= b < hl:
                mx_s[a * LANES:(a + 1) * LANES, b * LANES:(b + 1) * LANES] = kbd


def _ssm_kernel(u_ref, xw_ref, yw_ref, cc_ref, t1_ref, t1t_ref, a_ref, d_ref, o_ref,
                win_s, wout_s, m8_s, mx_s, v_scr, h_scr):
    n_chunks = u_ref.shape[0]
    half = (SSM_L // 2) * LANES
    ns = a_ref.shape[1]

    @pl.when(pl.program_id(1) == 0)
    def _():
        _ssm_expand_weights(xw_ref, yw_ref, cc_ref, t1_ref, t1t_ref, win_s, wout_s, m8_s, mx_s)

    uc = u_ref[...]
    v_scr[...] = _dot(uc, win_s[...])
    ar = a_ref[0:1, :]
    ai = a_ref[1:2, :]

    def step(c, carry):
        hr, hi = carry
        h_scr[pl.ds(c, 1), pl.ds(0, ns)] = hr
        h_scr[pl.ds(c, 1), pl.ds(ns, ns)] = hi
        vr = v_scr[pl.ds(c, 1), pl.ds(0, ns)]
        vi = v_scr[pl.ds(c, 1), pl.ds(ns, ns)]
        return ar * hr - ai * hi + vr, ar * hi + ai * hr + vi

    zero = jnp.zeros((1, ns), F32)
    lax.fori_loop(0, n_chunks, step, (zero, zero))
    y = _dot(h_scr[...].astype(BF16), wout_s[...])
    u_lo = uc[:, :half]
    u_hi = uc[:, half:]
    m8 = m8_s[...]
    y_lo = y[:, :half] + _dot(u_lo, m8)
    y_hi = y[:, half:] + _dot(u_hi, m8) + _dot(u_lo, mx_s[...])
    dd = d_ref[...]
    o_ref[:, :half] = _gelu_tanh(y_lo + dd[:, :half] * u_lo.astype(F32)).astype(BF16)
    o_ref[:, half:] = _gelu_tanh(y_hi + dd[:, half:] * u_hi.astype(F32)).astype(BF16)


def _ssm(ucat, xw, yw, cc, t1, t1t, a16, dcat):
    bsz, n_lg, n_chunks, width = ucat.shape
    ns2 = t1.shape[-1]
    half = width // 2

    def per_lg(a):
        return pl.BlockSpec((None,) + a.shape[1:], lambda g, b: (g,) + (0,) * (a.ndim - 1))

    def per_lg_axis1(a):
        return pl.BlockSpec((a.shape[0], None) + a.shape[2:], lambda g, b: (0, g, 0, 0))

    def const(a):
        return pl.BlockSpec(a.shape, lambda g, b: (0, 0))

    return pl.pallas_call(
        _ssm_kernel,
        grid=(n_lg, bsz),
        in_specs=[
            pl.BlockSpec((None, None, n_chunks, width), lambda g, b: (b, g, 0, 0)),
            per_lg_axis1(xw), per_lg_axis1(yw), per_lg(cc), const(t1), const(t1t), per_lg(a16), per_lg(dcat),
        ],
        out_specs=pl.BlockSpec((None, None, n_chunks, width), lambda g, b: (b, g, 0, 0)),
        out_shape=jax.ShapeDtypeStruct(ucat.shape, BF16),
        scratch_shapes=[
            pltpu.VMEM((width, ns2), BF16), pltpu.VMEM((ns2, width), BF16),
            pltpu.VMEM((half, half), BF16), pltpu.VMEM((half, half), BF16),
            pltpu.VMEM((n_chunks, ns2), F32), pltpu.VMEM((n_chunks, ns2), F32),
        ],
        compiler_params=_cparams(("arbitrary", "arbitrary")),
        name="ssm_scan",
    )(ucat, xw, yw, cc, t1, t1t, a16, dcat)


def _ssm_post_kernel(z_ref, x_ref, w_ref, mod_ref, g_ref, o_ref, zp, rp):
    n_lg, n, _ = z_ref.shape
    d = x_ref.shape[1]
    for s in range(SSM_L):
        for lg in range(n_lg):
            zp[s * n:(s + 1) * n, lg * LANES:(lg + 1) * LANES] = z_ref[lg, :, s * LANES:(s + 1) * LANES]
    gl = _dot(zp[...], w_ref[...])
    h = gl[:, :d] * _sigmoid(gl[:, d:])
    r = mod_ref[2:3, :] * _rms(h, g_ref[1:2, :])
    pitch = n + STRIDE_PAD
    for s in range(SSM_L):
        for lg in range(n_lg):
            rp[lg, s * pitch:s * pitch + n, :] = r[s * n:(s + 1) * n, lg * LANES:(lg + 1) * LANES]
    sub = 8
    for c in range(n):
        for k in range(SSM_L // sub):
            rows = pl.ds(c * SSM_L + k * sub, sub)
            for lg in range(n_lg):
                cols = slice(lg * LANES, (lg + 1) * LANES)
                o_ref[rows, cols] = x_ref[rows, cols] + rp[lg, pl.ds(k * sub * pitch + c, sub, stride=pitch), :]


def _ssm_post(zcat, x, w_glu, mod_l, g_l, tm):
    bsz, seq, d = x.shape
    n_lg = d // LANES
    return pl.pallas_call(
        _ssm_post_kernel,
        grid=(bsz, seq // tm),
        in_specs=[
            pl.BlockSpec((None, n_lg, tm // SSM_L, SSM_L * LANES), lambda b, i: (b, 0, i, 0)),
            pl.BlockSpec((None, tm, d), lambda b, i: (b, i, 0)),
            pl.BlockSpec((d, 2 * d), lambda b, i: (0, 0)),
            pl.BlockSpec((None, 6, d), lambda b, i: (b, 0, 0)),
            pl.BlockSpec((4, d), lambda b, i: (0, 0)),
        ],
        out_specs=pl.BlockSpec((None, tm, d), lambda b, i: (b, i, 0)),
        out_shape=jax.ShapeDtypeStruct(x.shape, F32),
        scratch_shapes=[pltpu.VMEM((tm, d), BF16),
                        pltpu.VMEM((n_lg, SSM_L * (tm // SSM_L + STRIDE_PAD), LANES), F32)],
        compiler_params=_cparams(("parallel", "parallel")),
        name="ssm_post",
    )(zcat, x, w_glu, mod_l, g_l)


def _ssm_weights(a_re, a_im, log_dt, b_re, b_im, c_re, c_im, d_skip):
    n_groups = a_re.shape[0]
    n_lg = n_groups // SSM_LG
    p, hh, ll = SSM_STATE, SSM_GROUP, SSM_L
    dt = jnp.exp(log_dt)[:, None]
    k = jnp.arange(ll + 1, dtype=F32)[:, None, None]
    mag = jnp.exp(k * (a_re * dt))
    pw_re = mag * jnp.cos(k * (a_im * dt))
    pw_im = mag * jnp.sin(k * (a_im * dt))
    den = a_re * a_re + a_im * a_im
    num_re, num_im = pw_re[1] - 1.0, pw_im[1]
    f_re = (num_re * a_re + num_im * a_im) / den
    f_im = (num_im * a_re - num_re * a_im) / den
    bb_re = f_re[..., None] * b_re - f_im[..., None] * b_im
    bb_im = f_re[..., None] * b_im + f_im[..., None] * b_re
    b2_re, b2_im = (x.reshape(n_lg, SSM_LG, p, hh).transpose(0, 2, 1, 3).reshape(n_lg, p, LANES)
                    for x in (bb_re, bb_im))

    def lanes_g(x):
        x = x.reshape(x.shape[0], n_lg, SSM_LG, p).transpose(0, 1, 3, 2)
        return jnp.repeat(x, hh, axis=-1)

    l_re, l_im = lanes_g(pw_re[:ll][::-1]), lanes_g(pw_im[:ll][::-1])
    xw = jnp.concatenate([l_re * b2_re - l_im * b2_im, l_re * b2_im + l_im * b2_re], axis=2)
    c1 = jnp.concatenate([c_re, -c_im], axis=-1)
    c2 = jnp.concatenate([-c_im, -c_re], axis=-1)
    p1 = jnp.concatenate([pw_re[1:], pw_re[1:]], axis=-1)[:, :, None, :]
    p2 = jnp.concatenate([pw_im[1:], pw_im[1:]], axis=-1)[:, :, None, :]
    yw = (c1[None] * p1 + c2[None] * p2).reshape(ll, n_lg, LANES, 2 * p)
    cc = c1.reshape(n_lg, LANES, 2 * p)
    j1 = jnp.arange(2 * SSM_LG * p)
    t1 = (jnp.arange(2 * p)[:, None] == ((j1 // (SSM_LG * p)) * p + j1 % p)[None, :]).astype(BF16)
    a16 = jnp.stack([pw_re[ll], pw_im[ll]], axis=0).reshape(2, n_lg, SSM_LG * p).transpose(1, 0, 2)
    dcat = jnp.tile(d_skip.reshape(n_lg, 1, LANES), (1, 1, ll))
    return xw.astype(BF16), yw.astype(BF16), cc.astype(BF16), t1, t1.T, a16, dcat


def _ffn_kernel(x_ref, wg_ref, wu_ref, wo_ref, mod_ref, g_ref, o_ref, h_scr, acc):
    k = pl.program_id(2)

    @pl.when(k == 0)
    def _():
        h = _rms(x_ref[...], g_ref[2:3, :]) * (1.0 + mod_ref[4:5, :]) + mod_ref[3:4, :]
        h_scr[...] = h.astype(BF16)
        acc[...] = jnp.zeros_like(acc)

    h = h_scr[...]
    g = _dot(h, wg_ref[...])
    u = _dot(h, wu_ref[...])
    act = (g * _sigmoid(g) * u).astype(BF16)
    acc[...] += _dot(act, wo_ref[...])

    @pl.when(k == pl.num_programs(2) - 1)
    def _():
        o_ref[...] = x_ref[...] + mod_ref[5:6, :] * _rms(acc[...], g_ref[3:4, :])


def _ffn(x, w_in, w_out, mod_l, g_l, tm, tf):
    bsz, seq, d = x.shape
    f = w_out.shape[0]
    nk = f // tf
    return pl.pallas_call(
        _ffn_kernel,
        grid=(bsz, seq // tm, nk),
        in_specs=[
            pl.BlockSpec((None, tm, d), lambda b, i, k: (b, i, 0)),
            pl.BlockSpec((d, tf), lambda b, i, k: (0, k)),
            pl.BlockSpec((d, tf), lambda b, i, k: (0, k + nk)),
            pl.BlockSpec((tf, d), lambda b, i, k: (k, 0)),
            pl.BlockSpec((None, 6, d), lambda b, i, k: (b, 0, 0)),
            pl.BlockSpec((4, d), lambda b, i, k: (0, 0)),
        ],
        out_specs=pl.BlockSpec((None, tm, d), lambda b, i, k: (b, i, 0)),
        out_shape=jax.ShapeDtypeStruct(x.shape, F32),
        scratch_shapes=[pltpu.VMEM((tm, d), BF16), pltpu.VMEM((tm, d), F32)],
        compiler_params=_cparams(("parallel", "parallel", "arbitrary")),
        name="ffn",
    )(x, w_in, w_in, w_out, mod_l, g_l)


def _rope_inputs(positions):
    bsz, seq = positions.shape
    half = MLA_ROPE // 2
    per_row = LANES // half
    inv_freq = ROPE_THETA ** (-jnp.arange(half, dtype=F32) / half)
    pos_rep = jnp.broadcast_to(positions[..., None], (bsz, seq, half)).reshape(bsz, seq // per_row, LANES)
    return pos_rep, jnp.tile(inv_freq, per_row).reshape(1, LANES)


def _rope_tables(pos_ref, f_ref, cos_scr, sin_lo_scr, sin_hi_scr):
    half = MLA_ROPE // 2
    per_row = LANES // half
    ang = pos_ref[...].astype(F32) * f_ref[...]
    cos_d = jnp.cos(ang)
    sin_d = jnp.sin(ang)
    n = ang.shape[0]
    lane = lax.broadcasted_iota(jnp.int32, ang.shape, 1)

    def onto(t, first_lane, q):
        shift = (first_lane - half * q) % LANES
        return t if shift == 0 else pltpu.roll(t, shift, axis=1)

    for q in range(per_row):
        rows = pl.ds(q, n, stride=per_row)
        x1, x2 = MLA_NOPE, MLA_NOPE + half
        cos_scr[rows, :] = jnp.where(lane < x1, 1.0, jnp.where(lane < x2, onto(cos_d, x1, q),
                                     jnp.where(lane < MLA_QK, onto(cos_d, x2, q), 0.0)))
        sin_lo_scr[rows, :] = jnp.where(lane < x1, 0.0, jnp.where(lane < x2, -onto(sin_d, x1, q), 0.0))
        sin_hi_scr[rows, :] = jnp.where(lane < x2, 0.0, jnp.where(lane < MLA_QK, onto(sin_d, x2, q), 0.0))


def _mla_proj_kernel(x_ref, pos_ref, f_ref, mod_ref, g_ref, wcq_ref, wckv_ref, wkr_ref,
                     qn_ref, kvn_ref, wq_ref, wk_ref, wv_ref, q_ref, k_ref, v_ref,
                     cos_scr, sin_lo_scr, sin_hi_scr):
    h = (_rms(x_ref[...], g_ref[0:1, :]) * (1.0 + mod_ref[1:2, :]) + mod_ref[0:1, :]).astype(BF16)
    cq = _rms(_dot(h, wcq_ref[...]), qn_ref[...]).astype(BF16)
    ckv = _rms(_dot(h, wckv_ref[...]), kvn_ref[...]).astype(BF16)
    kr = _dot(h, wkr_ref[...])
    _rope_tables(pos_ref, f_ref, cos_scr, sin_lo_scr, sin_hi_scr)
    cos = cos_scr[...]
    sin_lo = sin_lo_scr[...]
    sin_hi = sin_hi_scr[...]
    half = MLA_ROPE // 2

    def rope(t):
        width = t.shape[1]
        up = pltpu.roll(t, half, axis=1)
        down = pltpu.roll(t, width - half, axis=1)
        return [t[:, s:s + LANES] * cos + down[:, s:s + LANES] * sin_lo + up[:, s:s + LANES] * sin_hi
                for s in range(0, width, LANES)]

    q = _dot(cq, wq_ref[...])
    for hd, q_hd in enumerate(rope(q)):
        q_ref[:, hd * LANES:(hd + 1) * LANES] = q_hd.astype(BF16)
    kr_rot = rope(kr)[0]
    k = _dot(ckv, wk_ref[...])
    for hd in range(MLA_HEADS):
        sl = slice(hd * LANES, (hd + 1) * LANES)
        k_ref[:, sl] = (k[:, sl] + kr_rot).astype(BF16)
    vt = lax.dot_general(wv_ref[...], ckv, (((1,), (1,)), ((), ())), preferred_element_type=F32)
    row = lax.broadcasted_iota(jnp.int32, vt.shape, 0)
    v_ref[...] = jnp.where(row % V_ROWS == MLA_V, 1.0, vt).astype(BF16)


def _mla_proj(x, rope_t, mod_l, g_l, wts, tm):
    bsz, seq, d = x.shape
    hw = MLA_HEADS * LANES

    def full(a):
        return pl.BlockSpec(a.shape, lambda b, i: (0,) * a.ndim)

    out_sds = jax.ShapeDtypeStruct((bsz, seq, hw), BF16)
    vt_rows = MLA_HEADS * V_ROWS
    vt_sds = jax.ShapeDtypeStruct((bsz, seq // tm, vt_rows, tm), BF16)
    return pl.pallas_call(
        _mla_proj_kernel,
        grid=(bsz, seq // tm),
        in_specs=[
            pl.BlockSpec((None, tm, d), lambda b, i: (b, i, 0)),
            pl.BlockSpec((None, tm // (LANES // (MLA_ROPE // 2)), LANES), lambda b, i: (b, i, 0)),
            pl.BlockSpec((1, LANES), lambda b, i: (0, 0)),
            pl.BlockSpec((None, 6, d), lambda b, i: (b, 0, 0)),
            pl.BlockSpec((4, d), lambda b, i: (0, 0)),
        ] + [full(w) for w in wts],
        out_specs=[pl.BlockSpec((None, tm, hw), lambda b, i: (b, i, 0))] * 2
        + [pl.BlockSpec((None, None, vt_rows, tm), lambda b, i: (b, i, 0, 0))],
        out_shape=[out_sds, out_sds, vt_sds],
        scratch_shapes=[pltpu.VMEM((tm, LANES), F32)] * 3,
        compiler_params=_cparams(("parallel", "parallel")),
        name="mla_proj",
    )(x, *rope_t, mod_l, g_l, *wts)


def _mla_weights(w_in, q_norm, kv_norm, w_uq, w_ukv, w_o):
    pad = LANES - MLA_QK
    w_cq = w_in[:, :MLA_Q_RANK]
    w_ckv = w_in[:, MLA_Q_RANK:MLA_Q_RANK + MLA_KV_RANK]
    w_kr = jnp.pad(w_in[:, MLA_Q_RANK + MLA_KV_RANK:], ((0, 0), (MLA_NOPE, pad)))
    scale = MLA_QK ** -0.5 * math.log2(math.e)
    wq = w_uq.reshape(MLA_Q_RANK, MLA_HEADS, MLA_QK) * scale
    wq_pad = jnp.pad(wq, ((0, 0), (0, 0), (0, pad))).reshape(MLA_Q_RANK, MLA_HEADS * LANES)
    wkv =w_ukv.reshape(MLA_KV_RANK, MLA_HEADS, MLA_NOPE + MLA_V)
    zk = jnp.zeros((MLA_KV_RANK, MLA_HEADS, LANES - MLA_NOPE), F32)
    wk_pad = jnp.concatenate([wkv[..., :MLA_NOPE], zk], axis=-1).reshape(MLA_KV_RANK, MLA_HEADS * LANES)
    wv_t = jnp.pad(wkv[..., MLA_NOPE:].transpose(1, 2, 0), ((0, 0), (0, V_ROWS - MLA_V), (0, 0)))
    wv_t = wv_t.reshape(MLA_HEADS * V_ROWS, MLA_KV_RANK)
    proj_w = (w_cq.astype(BF16), w_ckv.astype(BF16), w_kr.astype(BF16),
              q_norm.reshape(1, -1), kv_norm.reshape(1, -1),
              wq_pad.astype(BF16), wk_pad.astype(BF16), wv_t.astype(BF16))
    return proj_w, w_o.astype(BF16)


def _attn_kernel(q_ref, k_ref, vt_ref, o_ref, m_scr, mt_scr, acc, s0_scr, s1_scr, qt_scr):
    tk = vt_ref.shape[2]
    n_sub = q_ref.shape[0] // tk
    qi = pl.program_id(2)
    s_scr = (s0_scr, s1_scr)
    qt_scr[...] = q_ref[...].astype(F32).T.astype(BF16)
    m_scr[...] = jnp.full_like(m_scr, -jnp.inf)
    acc[...] = jnp.zeros_like(acc)

    def scores(hd, j, q0=0):
        lanes = slice(hd * LANES, (hd + 1) * LANES)
        k = k_ref[pl.ds(pl.multiple_of(j * tk, tk), tk), lanes]
        st = _dot(k, qt_scr[lanes, q0:])
        s_scr[hd][:, q0:] = st
        mt_scr[hd, :, q0:] = jnp.max(st, axis=0, keepdims=True)

    def accumulate(hd, j, diag=None):
        q0 = 0 if diag is None else diag * tk
        st = s_scr[hd][:, q0:]
        if diag is not None:
            key = lax.broadcasted_iota(jnp.int32, st.shape, 0)
            qry = lax.broadcasted_iota(jnp.int32, st.shape, 1)
            st = jnp.where(key <= qry, st, -jnp.inf)
            mt = jnp.max(st, axis=0, keepdims=True)
        else:
            mt = mt_scr[hd]
        m_prev = m_scr[hd, :, q0:]
        m_new = jnp.maximum(m_prev, mt)
        alpha = jnp.exp2(m_prev - m_new)
        pt = jnp.exp2(st - m_new).astype(BF16)
        vt = vt_ref[j, hd * V_ROWS:(hd + 1) * V_ROWS, :]
        acc[hd, :, q0:] = alpha * acc[hd, :, q0:] + _dot(vt, pt)
        m_scr[hd, :, q0:] = m_new

    scores(0, 0)

    def body(j):
        scores(1, j)
        accumulate(0, j)
        scores(0, j + 1)
        accumulate(1, j)

    def body_pair(jj, carry):
        body(2 * jj)
        body(2 * jj + 1)
        return carry

    lax.fori_loop(0, (n_sub // 2) * qi, body_pair, 0)
    d0 = n_sub * qi
    for d in range(n_sub):
        scores(1, d0 + d, d * tk)
        accumulate(0, d0 + d, diag=d)
        if d + 1 < n_sub:
            scores(0, d0 + d + 1, (d + 1) * tk)
        accumulate(1, d0 + d, diag=d)
    outs = []
    for hd in range(2):
        a = acc[hd]
        outs.append(a[:MLA_V, :] / a[MLA_V:MLA_V + 1, :])
    o_ref[...] = jnp.concatenate(outs, axis=0).T.astype(BF16)


def _attention(q, k, vt, tq):
    bsz, seq, hw = q.shape
    n_pairs = hw // (2 * LANES)
    n_kt, _, tk = vt.shape[1:]
    assert tq % (2 * tk) == 0, "the main loop consumes kv tiles in pairs"
    return pl.pallas_call(
        _attn_kernel,
        grid=(bsz, n_pairs, seq // tq),
        in_specs=[
            pl.BlockSpec((None, tq, 2 * LANES), lambda b, h, i: (b, i, h)),
            pl.BlockSpec((None, seq, 2 * LANES), lambda b, h, i: (b, 0, h)),
            pl.BlockSpec((None, n_kt, 2 * V_ROWS, tk), lambda b, h, i: (b, 0, h, 0)),
        ],
        out_specs=pl.BlockSpec((None, tq, 2 * MLA_V), lambda b, h, i: (b, i, h)),
        out_shape=jax.ShapeDtypeStruct((bsz, seq, n_pairs * 2 * MLA_V), BF16),
        scratch_shapes=[pltpu.VMEM((2, 1, tq), F32), pltpu.VMEM((2, 1, tq), F32), pltpu.VMEM((2, V_ROWS, tq), F32),
                        pltpu.VMEM((tk, tq), F32), pltpu.VMEM((tk, tq), F32), pltpu.VMEM((2 * LANES, tq), BF16)],
        compiler_params=_cparams(("parallel", "parallel", "arbitrary")),
        name="mla_attention",
    )(q, k, vt)


def _attn_out_kernel(a_ref, x_ref, w_ref, mod_ref, g_ref, o_ref):
    h = _dot(a_ref[...], w_ref[...])
    o_ref[...] = x_ref[...] + mod_ref[2:3, :] * _rms(h, g_ref[1:2, :])


def _attn_out(a, x, wo_pad, mod_l, g_l, tm):
    bsz, seq, d = x.shape
    hw = a.shape[-1]
    return pl.pallas_call(
        _attn_out_kernel,
        grid=(bsz, seq // tm),
        in_specs=[
            pl.BlockSpec((None, tm, hw), lambda b, i: (b, i, 0)),
            pl.BlockSpec((None, tm, d), lambda b, i: (b, i, 0)),
            pl.BlockSpec((hw, d), lambda b, i: (0, 0)),
            pl.BlockSpec((None, 6, d), lambda b, i: (b, 0, 0)),
            pl.BlockSpec((4, d), lambda b, i: (0, 0)),
        ],
        out_specs=pl.BlockSpec((None, tm, d), lambda b, i: (b, i, 0)),
        out_shape=jax.ShapeDtypeStruct(x.shape, F32),
        compiler_params=_cparams(("parallel", "parallel")),
        name="mla_out",
    )(a, x, wo_pad, mod_l, g_l)


def _route(logits):
    row = lax.broadcasted_iota(jnp.int32, logits.shape, 0)
    n = logits.shape[0]
    neg = -jnp.inf
    m1 = jnp.max(logits, axis=0, keepdims=True)
    i1 = jnp.min(jnp.where(logits == m1, row, n), axis=0, keepdims=True)
    rest = jnp.where(row == i1, neg, logits)
    m2 = jnp.max(rest, axis=0, keepdims=True)
    i2 = jnp.min(jnp.where(rest == m2, row, n), axis=0, keepdims=True)
    e = jnp.exp(m2 - m1)
    w1 = 1.0 / (1.0 + e)
    w2 = e / (1.0 + e)
    first = row == i1
    second = row == i2
    comb = jnp.where(first, w1, 0.0) + jnp.where(second, w2, 0.0)
    sel = jnp.where(first, 1.0, 0.0) + jnp.where(second, 1.0, 0.0)
    return comb, sel


def _moe_kernel(x_ref, wr_ref, br_ref, tri_ref, wg_ref, wu_ref, wo_ref, mod_ref, g_ref, o_ref,
                h_scr, sel_scr, rank_scr, w_scr, acc):
    e = pl.program_id(1)
    tb = x_ref.shape[0]

    @pl.when(e == 0)
    def _():
        h = _rms(x_ref[...], g_ref[2:3, :]) * (1.0 + mod_ref[4:5, :]) + mod_ref[3:4, :]
        h_hi = h.astype(BF16)
        h_lo = (h - h_hi.astype(F32)).astype(BF16)
        nt = (((1,), (1,)), ((), ()))
        logits = (lax.dot_general(wr_ref[0], h_hi, nt, preferred_element_type=F32)
                  + lax.dot_general(wr_ref[1], h_hi, nt, preferred_element_type=F32)
                  + lax.dot_general(wr_ref[0], h_lo, nt, preferred_element_type=F32))
        n_e = sel_scr.shape[0]
        comb, sel = _route(logits[:n_e, :] + br_ref[:n_e, 0:1])
        sel_scr[...] = sel
        w_scr[...] = comb
        sel_pad = jnp.concatenate([sel, jnp.zeros_like(sel)], axis=0).astype(BF16)
        rank_scr[...] = _dot(sel_pad, tri_ref[...])[:n_e, :]
        h_scr[...] = h_hi
        acc[...] = jnp.zeros_like(acc)

    sel_row = sel_scr[pl.ds(e, 1), :]
    rank_row = rank_scr[pl.ds(e, 1), :]
    w_row = w_scr[pl.ds(e, 1), :]
    n_rows = jnp.sum(sel_row).astype(jnp.int32)

    def step(r, carry):
        slot = (lax.broadcasted_iota(jnp.int32, (MOE_SUB, tb), 0) + r * MOE_SUB).astype(F32)
        hit = rank_row == slot
        onehot = jnp.where(hit, sel_row, 0.0).astype(BF16)
        xs = _dot(onehot, h_scr[...]).astype(BF16)
        g = _dot(xs, wg_ref[...])
        u = _dot(xs, wu_ref[...])
        act = (g * _sigmoid(g) * u).astype(BF16)
        y = _dot(act, wo_ref[...])
        w_col = jnp.sum(jnp.where(hit, w_row, 0.0), axis=1, keepdims=True)
        yw = (y * w_col).astype(BF16)
        acc[...] += lax.dot_general(onehot, yw, (((0,), (0,)), ((), ())), preferred_element_type=F32)
        return carry

    lax.fori_loop(0, (n_rows + MOE_SUB - 1) // MOE_SUB, step, 0)

    @pl.when(e == pl.num_programs(1) - 1)
    def _():
        o_ref[...] = x_ref[...] + mod_ref[5:6, :] * _rms(acc[...], g_ref[3:4, :])


def _moe_layer(x, w_router, b_router, w_in, w_out, mod_l, g_l):
    bsz, seq, d = x.shape
    n_e, f, _ = w_out.shape
    tb = min(MOE_TB, seq)
    blocks_per_batch = seq // tb
    wr = jnp.pad(w_router.T, ((0, LANES - n_e), (0, 0)))
    wr_hi = wr.astype(BF16)
    wr = jnp.stack([wr_hi, (wr - wr_hi.astype(F32)).astype(BF16)])
    br = jnp.broadcast_to(jnp.pad(b_router, (0, LANES - n_e))[:, None], (LANES, LANES))
    pos = jnp.arange(tb)
    tri = (pos[:, None] < pos[None, :]).astype(BF16)
    w_in = w_in.astype(BF16)
    out = pl.pallas_call(
        _moe_kernel,
        grid=(bsz * blocks_per_batch, n_e),
        in_specs=[
            pl.BlockSpec((tb, d), lambda i, e: (i, 0)),
            pl.BlockSpec((2, LANES, d), lambda i, e: (0, 0, 0)),
            pl.BlockSpec((LANES, LANES), lambda i, e: (0, 0)),
            pl.BlockSpec((tb, tb), lambda i, e: (0, 0)),
            pl.BlockSpec((None, d, f), lambda i, e: (e, 0, 0)),
            pl.BlockSpec((None, d, f), lambda i, e: (e, 0, 1)),
            pl.BlockSpec((None, f, d), lambda i, e: (e, 0, 0)),
            pl.BlockSpec((None, 6, d), lambda i, e: (i // blocks_per_batch, 0, 0)),
            pl.BlockSpec((4, d), lambda i, e: (0, 0)),
        ],
        out_specs=pl.BlockSpec((tb, d), lambda i, e: (i, 0)),
        out_shape=jax.ShapeDtypeStruct((bsz * seq, d), F32),
        scratch_shapes=[pltpu.VMEM((tb, d), BF16), pltpu.VMEM((n_e, tb), F32), pltpu.VMEM((n_e, tb), F32),
                        pltpu.VMEM((n_e, tb), F32), pltpu.VMEM((tb, d), F32)],
        compiler_params=_cparams(("parallel", "arbitrary")),
        name="moe",
    )(x.reshape(bsz * seq, d), wr, br, tri, w_in, w_in, w_out.astype(BF16), mod_l, g_l)
    return out.reshape(bsz, seq, d)


def kernel(x, c, positions, norm_g, w_ada, b_ada, ssm_a_re, ssm_a_im, ssm_log_dt, ssm_b_re, ssm_b_im, ssm_c_re, ssm_c_im, ssm_d, ssm_w_glu, ffn_w_in, ffn_w_out, mla_w_in, mla_q_norm, mla_kv_norm, mla_w_uq, mla_w_ukv, mla_w_o, moe_w_router, moe_b_router, moe_w_in, moe_w_out):
    depth = norm_g.shape[0]
    seq = x.shape[1]
    tm = min(512, seq)
    mod = _ada(c, w_ada, b_ada)
    rope_t = None
    for i in range(depth):
        j = i // 2
        mod_l, g_l = mod[i], norm_g[i]
        if i % 2 == 0:
            ssm_w = _ssm_weights(ssm_a_re[j], ssm_a_im[j], ssm_log_dt[j], ssm_b_re[j], ssm_b_im[j],
                                 ssm_c_re[j], ssm_c_im[j], ssm_d[j])
            ucat = _ssm_pre(x, mod_l, g_l, tm)
            zcat = _ssm(ucat, *ssm_w)
            x = _ssm_post(zcat, x, ssm_w_glu[j].astype(BF16), mod_l, g_l, tm)
            f = ffn_w_out.shape[1]
            x = _ffn(x, ffn_w_in[j].astype(BF16), ffn_w_out[j].astype(BF16), mod_l, g_l, min(FFN_TM, seq), f)
        else:
            if rope_t is None:
                rope_t = _rope_inputs(positions)
            proj_w, wo_pad = _mla_weights(mla_w_in[j], mla_q_norm[j], mla_kv_norm[j],
                                          mla_w_uq[j], mla_w_ukv[j], mla_w_o[j])
            q, k, v = _mla_proj(x, rope_t, mod_l, g_l, proj_w, tm)
            a = _attention(q, k, v, min(ATTN_TQ, seq))
            x = _attn_out(a, x, wo_pad, mod_l, g_l, tm)
            x = _moe_layer(x, moe_w_router[j], moe_b_router[j], moe_w_in[j], moe_w_out[j], mod_l, g_l)
    return x
```

```python
import functools
import math

import jax
import jax.numpy as jnp
from jax import lax
from jax.experimental import pallas as pl
from jax.experimental.pallas import tpu as pltpu

F32 = jnp.float32
BF16 = jnp.bfloat16

RMS_EPS = 1e-6
LANES = 128
SSM_GROUP = 16
SSM_STATE = 64
SSM_L = 16
SSM_LG = LANES // SSM_GROUP
STRIDE_PAD = 4
MLA_HEADS = 16
MLA_NOPE = 64
MLA_ROPE = 32
MLA_V = 64
MLA_QK = MLA_NOPE + MLA_ROPE
ATTN_TQ = 1024
V_ROWS = 80
MLA_Q_RANK = 384
MLA_KV_RANK = 256
ROPE_THETA = 10000.0
N_EXPERTS = 8
IO_TM = 1024
FFN_TM = 256
MOE_TB = 1024
MOE_SUB = 144
VMEM_LIMIT = 56 * 1024 * 1024


def _cparams(sem):
    return pltpu.CompilerParams(dimension_semantics=sem, vmem_limit_bytes=VMEM_LIMIT)


def _rms(x, g):
    return x * lax.rsqrt(jnp.mean(x * x, axis=-1, keepdims=True) + RMS_EPS) * g


def _sigmoid(x):
    return 0.5 * jnp.tanh(0.5 * x) + 0.5


def _dot(a, b):
    return jnp.dot(a, b, preferred_element_type=F32)


def _ada_kernel(c_ref, w_ref, b_ref, o_ref):
    c = c_ref[...]
    c_act = c * _sigmoid(c)
    c_hi = c_act.astype(BF16)
    c_lo = (c_act - c_hi.astype(F32)).astype(BF16)
    w = w_ref[...]
    w_hi = w.astype(BF16)
    w_lo = (w - w_hi.astype(F32)).astype(BF16)
    o_ref[...] = _dot(c_hi, w_hi) + _dot(c_hi, w_lo) + _dot(c_lo, w_hi) + b_ref[...]


def _ada(c, w_ada, b_ada):
    depth, d, _ = w_ada.shape
    bsz = c.shape[0]
    rows = 16
    c_pad =jnp.pad(c, ((0, rows - bsz), (0, 0)))
    out = pl.pallas_call(
        _ada_kernel,
        grid=(depth, 6),
        in_specs=[
            pl.BlockSpec((rows, d), lambda i, j: (0, 0)),
            pl.BlockSpec((None, d, d), lambda i, j: (i, 0, j)),
            pl.BlockSpec((None, None, 1, d), lambda i, j: (i, j, 0, 0)),
        ],
        out_specs=pl.BlockSpec((None, None, rows, d), lambda i, j: (i, j, 0, 0)),
        out_shape=jax.ShapeDtypeStruct((depth, 6, rows, d), F32),
        compiler_params=_cparams(("arbitrary", "arbitrary")),
        name="ada_mod",
    )(c_pad, w_ada, b_ada.reshape(depth, 6, 1, d))
    return out[:, :, :bsz, :].transpose(0, 2, 1, 3)


def _ssm_pre_kernel(x_ref, mod_ref, g_ref, o_ref, scr):
    tm = x_ref.shape[0]
    n = tm // SSM_L
    u = _rms(x_ref[...], g_ref[0:1, :]) * (1.0 + mod_ref[1:2, :]) + mod_ref[0:1, :]
    n_lg = o_ref.shape[0]
    pitch = SSM_L + STRIDE_PAD
    for c in range(n):
        for lg in range(n_lg):
            scr[lg, c * pitch:c * pitch + SSM_L, :] = u[c * SSM_L:(c + 1) * SSM_L, lg * LANES:(lg + 1) * LANES]
    for s in range(SSM_L):
        for lg in range(n_lg):
            o_ref[lg, :, s * LANES:(s + 1) * LANES] = scr[lg, pl.ds(s, n, stride=pitch), :].astype(BF16)


def _ssm_pre(x, mod_l, g_l, tm):
    bsz, seq, d = x.shape
    n_lg = d // LANES
    return pl.pallas_call(
        _ssm_pre_kernel,
        grid=(bsz, seq // tm),
        in_specs=[
            pl.BlockSpec((None, tm, d), lambda b, i: (b, i, 0)),
            pl.BlockSpec((None, 6, d), lambda b, i: (b, 0, 0)),
            pl.BlockSpec((4, d), lambda b, i: (0, 0)),
        ],
        out_specs=pl.BlockSpec((None, n_lg, tm // SSM_L, SSM_L * LANES), lambda b, i: (b, 0, i, 0)),
        out_shape=jax.ShapeDtypeStruct((bsz, n_lg, seq // SSM_L, SSM_L * LANES), BF16),
        scratch_shapes=[pltpu.VMEM((n_lg, (tm // SSM_L) * (SSM_L + STRIDE_PAD), LANES), F32)],
        compiler_params=_cparams(("parallel", "parallel")),
        name="ssm_pre",
    )(x, mod_l, g_l)


def _gelu_tanh(y):
    return 0.5 * y * (1.0 + jnp.tanh(math.sqrt(2.0 / math.pi) * (y + 0.044715 * (y * y * y))))


def _same_group(shape, row_shift, col_shift):
    row_g = (lax.broadcasted_iota(jnp.int32, shape, 0) >> row_shift) & (SSM_LG - 1)
    col_g = (lax.broadcasted_iota(jnp.int32, shape, 1) >> col_shift) & (SSM_LG - 1)
    return row_g == col_g


def _ssm_expand_weights(xw_ref, yw_ref, cc_ref, t1_ref, t1t_ref, win_s, wout_s, m8_s, mx_s):
    hl = SSM_L // 2
    in_mask = _same_group((LANES, t1_ref.shape[1]), 4, 6)
    out_mask = _same_group((t1_ref.shape[1], LANES), 6, 4)
    k_mask = _same_group((LANES, LANES), 4, 4)
    m8_s[...] = jnp.zeros_like(m8_s)
    for s in range(SSM_L):
        rows = slice(s * LANES, (s + 1) * LANES)
        val = lax.dot_general(xw_ref[s], t1_ref[...], (((0,), (0,)), ((), ())), preferred_element_type=F32)
        win_s[rows, :] = jnp.where(in_mask, val, 0.0).astype(BF16)
        val = lax.dot_general(t1t_ref[...], yw_ref[s], (((1,), (1,)), ((), ())), preferred_element_type=F32)
        wout_s[:, rows] = jnp.where(out_mask, val, 0.0).astype(BF16)
    for tau in range(SSM_L):
        k_tau = lax.dot_general(xw_ref[SSM_L - 1 - tau], cc_ref[...], (((0,), (1,)), ((), ())),
                                preferred_element_type=F32)
        kbd = jnp.where(k_mask, k_tau, 0.0).astype(BF16)
        for a in range(hl):
            if a + tau < hl:
                m8_s[a * LANES:(a + 1) * LANES, (a + tau) * LANES:(a + tau + 1) * LANES] = kbd
            b = a + tau - hl
            if 0 <= b < hl:
                mx_s[a * LANES:(a + 1) * LANES, b * LANES:(b + 1) * LANES] = kbd


def _ssm_kernel(u_ref, xw_ref, yw_ref, cc_ref, t1_ref, t1t_ref, a_ref, d_ref, o_ref,
                win_s, wout_s, m8_s, mx_s, v_scr, h_scr):
    n_chunks = u_ref.shape[0]
    half = (SSM_L // 2) * LANES
    ns = a_ref.shape[1]

    @pl.when(pl.program_id(1) == 0)
    def _():
        _ssm_expand_weights(xw_ref, yw_ref, cc_ref, t1_ref, t1t_ref, win_s, wout_s, m8_s, mx_s)

    uc = u_ref[...]
    v_scr[...] = _dot(uc, win_s[...])
    ar = a_ref[0:1, :]
    ai = a_ref[1:2, :]

    def step(c, carry):
        hr, hi = carry
        h_scr[pl.ds(c, 1), pl.ds(0, ns)] = hr
        h_scr[pl.ds(c, 1), pl.ds(ns, ns)] = hi
        vr = v_scr[pl.ds(c, 1), pl.ds(0, ns)]
        vi = v_scr[pl.ds(c, 1), pl.ds(ns, ns)]
        return ar * hr - ai * hi + vr, ar * hi + ai * hr + vi

    zero = jnp.zeros((1, ns), F32)
    lax.fori_loop(0, n_chunks, step, (zero, zero))
    y = _dot(h_scr[...].astype(BF16), wout_s[...])
    u_lo = uc[:, :half]
    u_hi = uc[:, half:]
    m8 = m8_s[...]
    y_lo = y[:, :half] + _dot(u_lo, m8)
    y_hi = y[:, half:] + _dot(u_hi, m8) + _dot(u_lo, mx_s[...])
    dd = d_ref[...]
    o_ref[:, :half] = _gelu_tanh(y_lo + dd[:, :half] * u_lo.astype(F32)).astype(BF16)
    o_ref[:, half:] = _gelu_tanh(y_hi + dd[:, half:] * u_hi.astype(F32)).astype(BF16)


def _ssm(ucat, xw, yw, cc, t1, t1t, a16, dcat):
    bsz, n_lg, n_chunks, width = ucat.shape
    ns2 = t1.shape[-1]
    half = width // 2

    def per_lg(a):
        return pl.BlockSpec((None,) + a.shape[1:], lambda g, b: (g,) + (0,) * (a.ndim - 1))

    def per_lg_axis1(a):
        return pl.BlockSpec((a.shape[0], None) + a.shape[2:], lambda g, b: (0, g, 0, 0))

    def const(a):
        return pl.BlockSpec(a.shape, lambda g, b: (0, 0))

    return pl.pallas_call(
        _ssm_kernel,
        grid=(n_lg, bsz),
        in_specs=[
            pl.BlockSpec((None, None, n_chunks, width), lambda g, b: (b, g, 0, 0)),
            per_lg_axis1(xw), per_lg_axis1(yw), per_lg(cc), const(t1), const(t1t), per_lg(a16), per_lg(dcat),
        ],
        out_specs=pl.BlockSpec((None, None, n_chunks, width), lambda g, b: (b, g, 0, 0)),
        out_shape=jax.ShapeDtypeStruct(ucat.shape, BF16),
        scratch_shapes=[
            pltpu.VMEM((width, ns2), BF16), pltpu.VMEM((ns2, width), BF16),
            pltpu.VMEM((half, half), BF16), pltpu.VMEM((half, half), BF16),
            pltpu.VMEM((n_chunks, ns2), F32), pltpu.VMEM((n_chunks, ns2), F32),
        ],
        compiler_params=_cparams(("arbitrary", "arbitrary")),
        name="ssm_scan",
    )(ucat, xw, yw, cc, t1, t1t, a16, dcat)


def _ssm_post_kernel(z_ref, x_ref, w_ref, mod_ref, g_ref, o_ref, zp, rp):
    n_lg, n, _ = z_ref.shape
    d = x_ref.shape[1]
    for s in range(SSM_L):
        for lg in range(n_lg):
            zp[s * n:(s + 1) * n, lg * LANES:(lg + 1) * LANES] = z_ref[lg, :, s * LANES:(s + 1) * LANES]
    gl = _dot(zp[...], w_ref[...])
    h = gl[:, :d] * _sigmoid(gl[:, d:])
    r = mod_ref[2:3, :] * _rms(h, g_ref[1:2, :])
    pitch = n + STRIDE_PAD
    for s in range(SSM_L):
        for lg in range(n_lg):
            rp[lg, s * pitch:s * pitch + n, :] = r[s * n:(s + 1) * n, lg * LANES:(lg + 1) * LANES]
    sub = 8
    for c in range(n):
        for k in range(SSM_L // sub):
            rows = pl.ds(c * SSM_L + k * sub, sub)
            for lg in range(n_lg):
                cols = slice(lg * LANES, (lg + 1) * LANES)
                o_ref[rows, cols] = x_ref[rows, cols] + rp[lg, pl.ds(k * sub * pitch + c, sub, stride=pitch), :]


def _ssm_post(zcat, x, w_glu, mod_l, g_l, tm):
    bsz, seq, d = x.shape
    n_lg = d // LANES
    return pl.pallas_call(
        _ssm_post_kernel,
        grid=(bsz, seq // tm),
        in_specs=[
            pl.BlockSpec((None, n_lg, tm // SSM_L, SSM_L * LANES), lambda b, i: (b, 0, i, 0)),
            pl.BlockSpec((None, tm, d), lambda b, i: (b, i, 0)),
            pl.BlockSpec((d, 2 * d), lambda b, i: (0, 0)),
            pl.BlockSpec((None, 6, d), lambda b, i: (b, 0, 0)),
            pl.BlockSpec((4, d), lambda b, i: (0, 0)),
        ],
        out_specs=pl.BlockSpec((None, tm, d), lambda b, i: (b, i, 0)),
        out_shape=jax.ShapeDtypeStruct(x.shape, F32),
        scratch_shapes=[pltpu.VMEM((tm, d), BF16),
                        pltpu.VMEM((n_lg, SSM_L * (tm // SSM_L + STRIDE_PAD), LANES), F32)],
        compiler_params=_cparams(("parallel", "parallel")),
        name="ssm_post",
    )(zcat, x, w_glu, mod_l, g_l)


def _ssm_weights(a_re, a_im, log_dt, b_re, b_im, c_re, c_im, d_skip):
    n_groups = a_re.shape[0]
    n_lg = n_groups // SSM_LG
    p, hh, ll = SSM_STATE, SSM_GROUP, SSM_L
    dt = jnp.exp(log_dt)[:, None]
    k = jnp.arange(ll + 1, dtype=F32)[:, None, None]
    mag = jnp.exp(k * (a_re * dt))
    pw_re = mag * jnp.cos(k * (a_im * dt))
    pw_im = mag * jnp.sin(k * (a_im * dt))
    den = a_re * a_re + a_im * a_im
    num_re, num_im = pw_re[1] - 1.0, pw_im[1]
    f_re = (num_re * a_re + num_im * a_im) / den
    f_im = (num_im * a_re - num_re * a_im) / den
    bb_re = f_re[..., None] * b_re - f_im[..., None] * b_im
    bb_im = f_re[..., None] * b_im + f_im[..., None] * b_re
    b2_re, b2_im = (x.reshape(n_lg, SSM_LG, p, hh).transpose(0, 2, 1, 3).reshape(n_lg, p, LANES)
                    for x in (bb_re, bb_im))

    def lanes_g(x):
        x = x.reshape(x.shape[0], n_lg, SSM_LG, p).transpose(0, 1, 3, 2)
        return jnp.repeat(x, hh, axis=-1)

    l_re, l_im = lanes_g(pw_re[:ll][::-1]), lanes_g(pw_im[:ll][::-1])
    xw = jnp.concatenate([l_re * b2_re - l_im * b2_im, l_re * b2_im + l_im * b2_re], axis=2)
    c1 = jnp.concatenate([c_re, -c_im], axis=-1)
    c2 = jnp.concatenate([-c_im, -c_re], axis=-1)
    p1 = jnp.concatenate([pw_re[1:], pw_re[1:]], axis=-1)[:, :, None, :]
    p2 = jnp.concatenate([pw_im[1:], pw_im[1:]], axis=-1)[:, :, None, :]
    yw = (c1[None] * p1 + c2[None] * p2).reshape(ll, n_lg, LANES, 2 * p)
    cc = c1.reshape(n_lg, LANES, 2 * p)
    j1 = jnp.arange(2 * SSM_LG * p)
    t1 = (jnp.arange(2 * p)[:, None] == ((j1 // (SSM_LG * p)) * p + j1 % p)[None, :]).astype(BF16)
    a16 = jnp.stack([pw_re[ll], pw_im[ll]], axis=0).reshape(2, n_lg, SSM_LG * p).transpose(1, 0, 2)
    dcat = jnp.tile(d_skip.reshape(n_lg, 1, LANES), (1, 1, ll))
    return xw.astype(BF16), yw.astype(BF16), cc.astype(BF16), t1, t1.T, a16, dcat


def _ffn_kernel(x_ref, wg_ref, wu_ref, wo_ref, mod_ref, g_ref, o_ref, h_scr, acc):
    k = pl.program_id(2)

    @pl.when(k == 0)
    def _():
        h = _rms(x_ref[...], g_ref[2:3, :]) * (1.0 + mod_ref[4:5, :]) + mod_ref[3:4, :]
        h_scr[...] = h.astype(BF16)
        acc[...] = jnp.zeros_like(acc)

    h = h_scr[...]
    g = _dot(h, wg_ref[...])
    u = _dot(h, wu_ref[...])
    act = (g * _sigmoid(g) * u).astype(BF16)
    acc[...] += _dot(act, wo_ref[...])

    @pl.when(k == pl.num_programs(2) - 1)
    def _():
        o_ref[...] = x_ref[...] + mod_ref[5:6, :] * _rms(acc[...], g_ref[3:4, :])


def _ffn(x, w_in, w_out, mod_l, g_l, tm, tf):
    bsz, seq, d = x.shape
    f = w_out.shape[0]
    nk = f // tf
    return pl.pallas_call(
        _ffn_kernel,
        grid=(bsz, seq // tm, nk),
        in_specs=[
            pl.BlockSpec((None, tm, d), lambda b, i, k: (b, i, 0)),
            pl.BlockSpec((d, tf), lambda b, i, k: (0, k)),
            pl.BlockSpec((d, tf), lambda b, i, k: (0, k + nk)),
            pl.BlockSpec((tf, d), lambda b, i, k: (k, 0)),
            pl.BlockSpec((None, 6, d), lambda b, i, k: (b, 0, 0)),
            pl.BlockSpec((4, d), lambda b, i, k: (0, 0)),
        ],
        out_specs=pl.BlockSpec((None, tm, d), lambda b, i, k: (b, i, 0)),
        out_shape=jax.ShapeDtypeStruct(x.shape, F32),
        scratch_shapes=[pltpu.VMEM((tm, d), BF16), pltpu.VMEM((tm, d), F32)],
        compiler_params=_cparams(("parallel", "parallel", "arbitrary")),
        name="ffn",
    )(x, w_in, w_in, w_out, mod_l, g_l)


def _rope_inputs(positions):
    bsz, seq = positions.shape
    half = MLA_ROPE // 2
    per_row = LANES // half
    inv_freq = ROPE_THETA ** (-jnp.arange(half, dtype=F32) / half)
    pos_rep = jnp.broadcast_to(positions[..., None], (bsz, seq, half)).reshape(bsz, seq // per_row, LANES)
    return pos_rep, jnp.tile(inv_freq, per_row).reshape(1, LANES)


def _rope_tables(pos_ref, f_ref, cos_scr, sin_lo_scr, sin_hi_scr):
    half = MLA_ROPE // 2
    per_row = LANES // half
    ang = pos_ref[...].astype(F32) * f_ref[...]
    cos_d = jnp.cos(ang)
    sin_d = jnp.sin(ang)
    n = ang.shape[0]
    lane = lax.broadcasted_iota(jnp.int32, ang.shape, 1)

    def onto(t, first_lane, q):
        shift = (first_lane - half * q) % LANES
        return t if shift == 0 else pltpu.roll(t, shift, axis=1)

    for q in range(per_row):
        rows = pl.ds(q, n, stride=per_row)
        x1, x2 = MLA_NOPE, MLA_NOPE + half
        cos_scr[rows, :] = jnp.where(lane < x1, 1.0, jnp.where(lane < x2, onto(cos_d, x1, q),
                                     jnp.where(lane < MLA_QK, onto(cos_d, x2, q), 0.0)))
        sin_lo_scr[rows, :] = jnp.where(lane < x1, 0.0, jnp.where(lane < x2, -onto(sin_d, x1, q), 0.0))
        sin_hi_scr[rows, :] = jnp.where(lane < x2, 0.0, jnp.where(lane < MLA_QK, onto(sin_d, x2, q), 0.0))


def _mla_proj_kernel(x_ref, pos_ref, f_ref, mod_ref, g_ref, wcq_ref, wckv_ref, wkr_ref,
                     qn_ref, kvn_ref, wq_ref, wk_ref, wv_ref, q_ref, k_ref, v_ref,
                     cos_scr, sin_lo_scr, sin_hi_scr):
    h = (_rms(x_ref[...], g_ref[0:1, :]) * (1.0 + mod_ref[1:2, :]) + mod_ref[0:1, :]).astype(BF16)
    cq = _rms(_dot(h, wcq_ref[...]), qn_ref[...]).astype(BF16)
    ckv = _rms(_dot(h, wckv_ref[...]), kvn_ref[...]).astype(BF16)
    kr = _dot(h, wkr_ref[...])
    _rope_tables(pos_ref, f_ref, cos_scr, sin_lo_scr, sin_hi_scr)
    cos = cos_scr[...]
    sin_lo = sin_lo_scr[...]
    sin_hi = sin_hi_scr[...]
    half = MLA_ROPE // 2

    def rope(t):
        width = t.shape[1]
        up = pltpu.roll(t, half, axis=1)
        down = pltpu.roll(t, width - half, axis=1)
        return [t[:, s:s + LANES] * cos + down[:, s:s + LANES] * sin_lo + up[:, s:s + LANES] * sin_hi
                for s in range(0, width, LANES)]

    q = _dot(cq, wq_ref[...])
    for hd, q_hd in enumerate(rope(q)):
        q_ref[:, hd * LANES:(hd + 1) * LANES] = q_hd.astype(BF16)
    kr_rot = rope(kr)[0]
    k = _dot(ckv, wk_ref[...])
    for hd in range(MLA_HEADS):
        sl = slice(hd * LANES, (hd + 1) * LANES)
        k_ref[:, sl] = (k[:, sl] + kr_rot).astype(BF16)
    vt = lax.dot_general(wv_ref[...], ckv, (((1,), (1,)), ((), ())), preferred_element_type=F32)
    row = lax.broadcasted_iota(jnp.int32, vt.shape, 0)
    v_ref[...] = jnp.where(row % V_ROWS == MLA_V, 1.0, vt).astype(BF16)


def _mla_proj(x, rope_t, mod_l, g_l, wts, tm):
    bsz, seq, d = x.shape
    hw = MLA_HEADS * LANES

    def full(a):
        return pl.BlockSpec(a.shape, lambda b, i: (0,) * a.ndim)

    out_sds = jax.ShapeDtypeStruct((bsz, seq, hw), BF16)
    vt_rows = MLA_HEADS * V_ROWS
    vt_sds = jax.ShapeDtypeStruct((bsz, seq // tm, vt_rows, tm), BF16)
    return pl.pallas_call(
        _mla_proj_kernel,
        grid=(bsz, seq // tm),
        in_specs=[
            pl.BlockSpec((None, tm, d), lambda b, i: (b, i, 0)),
            pl.BlockSpec((None, tm // (LANES // (MLA_ROPE // 2)), LANES), lambda b, i: (b, i, 0)),
            pl.BlockSpec((1, LANES), lambda b, i: (0, 0)),
            pl.BlockSpec((None, 6, d), lambda b, i: (b, 0, 0)),
            pl.BlockSpec((4, d), lambda b, i: (0, 0)),
        ] + [full(w) for w in wts],
        out_specs=[pl.BlockSpec((None, tm, hw), lambda b, i: (b, i, 0))] * 2
        + [pl.BlockSpec((None, None, vt_rows, tm), lambda b, i: (b, i, 0, 0))],
        out_shape=[out_sds, out_sds, vt_sds],
        scratch_shapes=[pltpu.VMEM((tm, LANES), F32)] * 3,
        compiler_params=_cparams(("parallel", "parallel")),
        name="mla_proj",
    )(x, *rope_t, mod_l, g_l, *wts)


def _mla_weights(w_in, q_norm, kv_norm, w_uq, w_ukv, w_o):
    pad = LANES - MLA_QK
    w_cq = w_in[:, :MLA_Q_RANK]
    w_ckv = w_in[:, MLA_Q_RANK:MLA_Q_RANK + MLA_KV_RANK]
    w_kr = jnp.pad(w_in[:, MLA_Q_RANK + MLA_KV_RANK:], ((0, 0), (MLA_NOPE, pad)))
    scale = MLA_QK ** -0.5 * math.log2(math.e)
    wq = w_uq.reshape(MLA_Q_RANK, MLA_HEADS, MLA_QK) * scale
    wq_pad = jnp.pad(wq, ((0, 0), (0, 0), (0, pad))).reshape(MLA_Q_RANK, MLA_HEADS * LANES)
    wkv =w_ukv.reshape(MLA_KV_RANK, MLA_HEADS, MLA_NOPE + MLA_V)
    zk = jnp.zeros((MLA_KV_RANK, MLA_HEADS, LANES - MLA_NOPE), F32)
    wk_pad = jnp.concatenate([wkv[..., :MLA_NOPE], zk], axis=-1).reshape(MLA_KV_RANK, MLA_HEADS * LANES)
    wv_t = jnp.pad(wkv[..., MLA_NOPE:].transpose(1, 2, 0), ((0, 0), (0, V_ROWS - MLA_V), (0, 0)))
    wv_t = wv_t.reshape(MLA_HEADS * V_ROWS, MLA_KV_RANK)
    proj_w = (w_cq.astype(BF16), w_ckv.astype(BF16), w_kr.astype(BF16),
              q_norm.reshape(1, -1), kv_norm.reshape(1, -1),
              wq_pad.astype(BF16), wk_pad.astype(BF16), wv_t.astype(BF16))
    return proj_w, w_o.astype(BF16)


def _attn_kernel(q_ref, k_ref, vt_ref, o_ref, m_scr, mt_scr, acc, s0_scr, s1_scr, qt_scr):
    tk = vt_ref.shape[2]
    n_sub = q_ref.shape[0] // tk
    qi = pl.program_id(2)
    s_scr = (s0_scr, s1_scr)
    qt_scr[...] = q_ref[...].astype(F32).T.astype(BF16)
    m_scr[...] = jnp.full_like(m_scr, -jnp.inf)
    acc[...] = jnp.zeros_like(acc)

    def scores(hd, j, q0=0):
        lanes = slice(hd * LANES, (hd + 1) * LANES)
        k = k_ref[pl.ds(pl.multiple_of(j * tk, tk), tk), lanes]
        st = _dot(k, qt_scr[lanes, q0:])
        s_scr[hd][:, q0:] = st
        mt_scr[hd, :, q0:] = jnp.max(st, axis=0, keepdims=True)

    def accumulate(hd, j, diag=None):
        q0 = 0 if diag is None else diag * tk
        st = s_scr[hd][:, q0:]
        if diag is not None:
            key = lax.broadcasted_iota(jnp.int32, st.shape, 0)
            qry = lax.broadcasted_iota(jnp.int32, st.shape, 1)
            st = jnp.where(key <= qry, st, -jnp.inf)
            mt = jnp.max(st, axis=0, keepdims=True)
        else:
            mt = mt_scr[hd]
        m_prev = m_scr[hd, :, q0:]
        m_new = jnp.maximum(m_prev, mt)
        alpha = jnp.exp2(m_prev - m_new)
        pt = jnp.exp2(st - m_new).astype(BF16)
        vt = vt_ref[j, hd * V_ROWS:(hd + 1) * V_ROWS, :]
        acc[hd, :, q0:] = alpha * acc[hd, :, q0:] + _dot(vt, pt)
        m_scr[hd, :, q0:] = m_new

    scores(0, 0)

    def body(j):
        scores(1, j)
        accumulate(0, j)
        scores(0, j + 1)
        accumulate(1, j)

    def body_pair(jj, carry):
        body(2 * jj)
        body(2 * jj + 1)
        return carry

    lax.fori_loop(0, (n_sub // 2) * qi, body_pair, 0)
    d0 = n_sub * qi
    for d in range(n_sub):
        scores(1, d0 + d, d * tk)
        accumulate(0, d0 + d, diag=d)
        if d + 1 < n_sub:
            scores(0, d0 + d + 1, (d + 1) * tk)
        accumulate(1, d0 + d, diag=d)
    outs = []
    for hd in range(2):
        a = acc[hd]
        outs.append(a[:MLA_V, :] / a[MLA_V:MLA_V + 1, :])
    o_ref[...] = jnp.concatenate(outs, axis=0).T.astype(BF16)


def _attention(q, k, vt, tq):
    bsz, seq, hw = q.shape
    n_pairs = hw // (2 * LANES)
    n_kt, _, tk = vt.shape[1:]
    assert tq % (2 * tk) == 0, "the main loop consumes kv tiles in pairs"
    return pl.pallas_call(
        _attn_kernel,
        grid=(bsz, n_pairs, seq // tq),
        in_specs=[
            pl.BlockSpec((None, tq, 2 * LANES), lambda b, h, i: (b, i, h)),
            pl.BlockSpec((None, seq, 2 * LANES), lambda b, h, i: (b, 0, h)),
            pl.BlockSpec((None, n_kt, 2 * V_ROWS, tk), lambda b, h, i: (b, 0, h, 0)),
        ],
        out_specs=pl.BlockSpec((None, tq, 2 * MLA_V), lambda b, h, i: (b, i, h)),
        out_shape=jax.ShapeDtypeStruct((bsz, seq, n_pairs * 2 * MLA_V), BF16),
        scratch_shapes=[pltpu.VMEM((2, 1, tq), F32), pltpu.VMEM((2, 1, tq), F32), pltpu.VMEM((2, V_ROWS, tq), F32),
                        pltpu.VMEM((tk, tq), F32), pltpu.VMEM((tk, tq), F32), pltpu.VMEM((2 * LANES, tq), BF16)],
        compiler_params=_cparams(("parallel", "parallel", "arbitrary")),
        name="mla_attention",
    )(q, k, vt)


def _attn_out_kernel(a_ref, x_ref, w_ref, mod_ref, g_ref, o_ref):
    h = _dot(a_ref[...], w_ref[...])
    o_ref[...] = x_ref[...] + mod_ref[2:3, :] * _rms(h, g_ref[1:2, :])


def _attn_out(a, x, wo_pad, mod_l, g_l, tm):
    bsz, seq, d = x.shape
    hw = a.shape[-1]
    return pl.pallas_call(
        _attn_out_kernel,
        grid=(bsz, seq // tm),
        in_specs=[
            pl.BlockSpec((None, tm, hw), lambda b, i: (b, i, 0)),
            pl.BlockSpec((None, tm, d), lambda b, i: (b, i, 0)),
            pl.BlockSpec((hw, d), lambda b, i: (0, 0)),
            pl.BlockSpec((None, 6, d), lambda b, i: (b, 0, 0)),
            pl.BlockSpec((4, d), lambda b, i: (0, 0)),
        ],
        out_specs=pl.BlockSpec((None, tm, d), lambda b, i: (b, i, 0)),
        out_shape=jax.ShapeDtypeStruct(x.shape, F32),
        compiler_params=_cparams(("parallel", "parallel")),
        name="mla_out",
    )(a, x, wo_pad, mod_l, g_l)


def _route(logits):
    row = lax.broadcasted_iota(jnp.int32, logits.shape, 0)
    n = logits.shape[0]
    neg = -jnp.inf
    m1 = jnp.max(logits, axis=0, keepdims=True)
    i1 = jnp.min(jnp.where(logits == m1, row, n), axis=0, keepdims=True)
    rest = jnp.where(row == i1, neg, logits)
    m2 = jnp.max(rest, axis=0, keepdims=True)
    i2 = jnp.min(jnp.where(rest == m2, row, n), axis=0, keepdims=True)
    e = jnp.exp(m2 - m1)
    w1 = 1.0 / (1.0 + e)
    w2 = e / (1.0 + e)
    first = row == i1
    second = row == i2
    comb = jnp.where(first, w1, 0.0) + jnp.where(second, w2, 0.0)
    sel = jnp.where(first, 1.0, 0.0) + jnp.where(second, 1.0, 0.0)
    return comb, sel


def _moe_kernel(x_ref, wr_ref, br_ref, tri_ref, wg_ref, wu_ref, wo_ref, mod_ref, g_ref, o_ref,
                h_scr, sel_scr, rank_scr, w_scr, acc):
    e = pl.program_id(1)
    tb = x_ref.shape[0]

    @pl.when(e == 0)
    def _():
        h = _rms(x_ref[...], g_ref[2:3, :]) * (1.0 + mod_ref[4:5, :]) + mod_ref[3:4, :]
        h_hi = h.astype(BF16)
        h_lo = (h - h_hi.astype(F32)).astype(BF16)
        nt = (((1,), (1,)), ((), ()))
        logits = (lax.dot_general(wr_ref[0], h_hi, nt, preferred_element_type=F32)
                  + lax.dot_general(wr_ref[1], h_hi, nt, preferred_element_type=F32)
                  + lax.dot_general(wr_ref[0], h_lo, nt, preferred_element_type=F32))
        n_e = sel_scr.shape[0]
        comb, sel = _route(logits[:n_e, :] + br_ref[:n_e, 0:1])
        sel_scr[...] = sel
        w_scr[...] = comb
        sel_pad = jnp.concatenate([sel, jnp.zeros_like(sel)], axis=0).astype(BF16)
        rank_scr[...] = _dot(sel_pad, tri_ref[...])[:n_e, :]
        h_scr[...] = h_hi
        acc[...] = jnp.zeros_like(acc)

    sel_row = sel_scr[pl.ds(e, 1), :]
    rank_row = rank_scr[pl.ds(e, 1), :]
    w_row = w_scr[pl.ds(e, 1), :]
    n_rows = jnp.sum(sel_row).astype(jnp.int32)

    def step(r, carry):
        slot = (lax.broadcasted_iota(jnp.int32, (MOE_SUB, tb), 0) + r * MOE_SUB).astype(F32)
        hit = rank_row == slot
        onehot = jnp.where(hit, sel_row, 0.0).astype(BF16)
        xs = _dot(onehot, h_scr[...]).astype(BF16)
        g = _dot(xs, wg_ref[...])
        u = _dot(xs, wu_ref[...])
        act = (g * _sigmoid(g) * u).astype(BF16)
        y = _dot(act, wo_ref[...])
        w_col = jnp.sum(jnp.where(hit, w_row, 0.0), axis=1, keepdims=True)
        yw = (y * w_col).astype(BF16)
        acc[...] += lax.dot_general(onehot, yw, (((0,), (0,)), ((), ())), preferred_element_type=F32)
        return carry

    lax.fori_loop(0, (n_rows + MOE_SUB - 1) // MOE_SUB, step, 0)

    @pl.when(e == pl.num_programs(1) - 1)
    def _():
        o_ref[...] = x_ref[...] + mod_ref[5:6, :] * _rms(acc[...], g_ref[3:4, :])


def _moe_layer(x, w_router, b_router, w_in, w_out, mod_l, g_l):
    bsz, seq, d = x.shape
    n_e, f, _ = w_out.shape
    tb = min(MOE_TB, seq)
    blocks_per_batch = seq // tb
    wr = jnp.pad(w_router.T, ((0, LANES - n_e), (0, 0)))
    wr_hi = wr.astype(BF16)
    wr = jnp.stack([wr_hi, (wr - wr_hi.astype(F32)).astype(BF16)])
    br = jnp.broadcast_to(jnp.pad(b_router, (0, LANES - n_e))[:, None], (LANES, LANES))
    pos = jnp.arange(tb)
    tri = (pos[:, None] < pos[None, :]).astype(BF16)
    w_in = w_in.astype(BF16)
    out = pl.pallas_call(
        _moe_kernel,
        grid=(bsz * blocks_per_batch, n_e),
        in_specs=[
            pl.BlockSpec((tb, d), lambda i, e: (i, 0)),
            pl.BlockSpec((2, LANES, d), lambda i, e: (0, 0, 0)),
            pl.BlockSpec((LANES, LANES), lambda i, e: (0, 0)),
            pl.BlockSpec((tb, tb), lambda i, e: (0, 0)),
            pl.BlockSpec((None, d, f), lambda i, e: (e, 0, 0)),
            pl.BlockSpec((None, d, f), lambda i, e: (e, 0, 1)),
            pl.BlockSpec((None, f, d), lambda i, e: (e, 0, 0)),
            pl.BlockSpec((None, 6, d), lambda i, e: (i // blocks_per_batch, 0, 0)),
            pl.BlockSpec((4, d), lambda i, e: (0, 0)),
        ],
        out_specs=pl.BlockSpec((tb, d), lambda i, e: (i, 0)),
        out_shape=jax.ShapeDtypeStruct((bsz * seq, d), F32),
        scratch_shapes=[pltpu.VMEM((tb, d), BF16), pltpu.VMEM((n_e, tb), F32), pltpu.VMEM((n_e, tb), F32),
                        pltpu.VMEM((n_e, tb), F32), pltpu.VMEM((tb, d), F32)],
        compiler_params=_cparams(("parallel", "arbitrary")),
        name="moe",
    )(x.reshape(bsz * seq, d), wr, br, tri, w_in, w_in, w_out.astype(BF16), mod_l, g_l)
    return out.reshape(bsz, seq, d)


def kernel(x, c, positions, norm_g, w_ada, b_ada, ssm_a_re, ssm_a_im, ssm_log_dt, ssm_b_re, ssm_b_im, ssm_c_re, ssm_c_im, ssm_d, ssm_w_glu, ffn_w_in, ffn_w_out, mla_w_in, mla_q_norm, mla_kv_norm, mla_w_uq, mla_w_ukv, mla_w_o, moe_w_router, moe_b_router, moe_w_in, moe_w_out):
    depth = norm_g.shape[0]
    seq = x.shape[1]
    tm = min(512, seq)
    mod = _ada(c, w_ada, b_ada)
    rope_t = None
    for i in range(depth):
        j = i // 2
        mod_l, g_l = mod[i], norm_g[i]
        if i % 2 == 0:
            ssm_w = _ssm_weights(ssm_a_re[j], ssm_a_im[j], ssm_log_dt[j], ssm_b_re[j], ssm_b_im[j],
                                 ssm_c_re[j], ssm_c_im[j], ssm_d[j])
            ucat = _ssm_pre(x, mod_l, g_l, min(IO_TM, seq))
            zcat = _ssm(ucat, *ssm_w)
            x = _ssm_post(zcat, x, ssm_w_glu[j].astype(BF16), mod_l, g_l, tm)
            f = ffn_w_out.shape[1]
            x = _ffn(x, ffn_w_in[j].astype(BF16), ffn_w_out[j].astype(BF16), mod_l, g_l, min(FFN_TM, seq), f)
        else:
            if rope_t is None:
                rope_t = _rope_inputs(positions)
            proj_w, wo_pad = _mla_weights(mla_w_in[j], mla_q_norm[j], mla_kv_norm[j],
                                          mla_w_uq[j], mla_w_ukv[j], mla_w_o[j])
            q, k, v = _mla_proj(x, rope_t, mod_l, g_l, proj_w, tm)
            a = _attention(q, k, v, min(ATTN_TQ, seq))
            x = _attn_out(a, x, wo_pad, mod_l, g_l, min(IO_TM, seq))
            x = _moe_layer(x, moe_w_router[j], moe_b_router[j], moe_w_in[j], moe_w_out[j], mod_l, g_l)
    return x
```

```python
import functools
import math

import jax
import jax.numpy as jnp
from jax import lax
from jax.experimental import pallas as pl
from jax.experimental.pallas import tpu as pltpu

F32 = jnp.float32
BF16 = jnp.bfloat16

RMS_EPS = 1e-6
LANES = 128
SSM_GROUP = 16
SSM_STATE = 64
SSM_L = 16
SSM_LG = LANES // SSM_GROUP
STRIDE_PAD = 4
MLA_HEADS = 16
MLA_NOPE = 64
MLA_ROPE = 32
MLA_V = 64
MLA_QK = MLA_NOPE + MLA_ROPE
ATTN_TQ = 1024
V_ROWS = 80
MLA_Q_RANK = 384
MLA_KV_RANK = 256
ROPE_THETA = 10000.0
N_EXPERTS = 8
IO_TM = 1024
FFN_TM = 256
MOE_TB = 1024
MOE_SUB = 144
VMEM_LIMIT = 56 * 1024 * 1024


def _cparams(sem):
    return pltpu.CompilerParams(dimension_semantics=sem, vmem_limit_bytes=VMEM_LIMIT)


def _rms(x, g):
    return x * lax.rsqrt(jnp.mean(x * x, axis=-1, keepdims=True) + RMS_EPS) * g


def _sigmoid(x):
    return 0.5 * jnp.tanh(0.5 * x) + 0.5


def _dot(a, b):
    return jnp.dot(a, b, preferred_element_type=F32)


def _ada_kernel(c_ref, w_ref, b_ref, o_ref):
    c = c_ref[...]
    c_act = c * _sigmoid(c)
    c_hi = c_act.astype(BF16)
    c_lo = (c_act - c_hi.astype(F32)).astype(BF16)
    w = w_ref[...]
    w_hi = w.astype(BF16)
    w_lo = (w - w_hi.astype(F32)).astype(BF16)
    o_ref[...] = _dot(c_hi, w_hi) + _dot(c_hi, w_lo) + _dot(c_lo, w_hi) + b_ref[...]


def _ada(c, w_ada, b_ada):
    depth, d, _ = w_ada.shape
    bsz = c.shape[0]
    rows = 16
    c_pad =jnp.pad(c, ((0, rows - bsz), (0, 0)))
    out = pl.pallas_call(
        _ada_kernel,
        grid=(depth, 6),
        in_specs=[
            pl.BlockSpec((rows, d), lambda i, j: (0, 0)),
            pl.BlockSpec((None, d, d), lambda i, j: (i, 0, j)),
            pl.BlockSpec((None, None, 1, d), lambda i, j: (i, j, 0, 0)),
        ],
        out_specs=pl.BlockSpec((None, None, rows, d), lambda i, j: (i, j, 0, 0)),
        out_shape=jax.ShapeDtypeStruct((depth, 6, rows, d), F32),
        compiler_params=_cparams(("arbitrary", "arbitrary")),
        name="ada_mod",
    )(c_pad, w_ada, b_ada.reshape(depth, 6, 1, d))
    return out[:, :, :bsz, :].transpose(0, 2, 1, 3)


def _ssm_pre_kernel(x_ref, mod_ref, g_ref, o_ref, scr):
    tm = x_ref.shape[0]
    n = tm // SSM_L
    u = _rms(x_ref[...], g_ref[0:1, :]) * (1.0 + mod_ref[1:2, :]) + mod_ref[0:1, :]
    n_lg = o_ref.shape[0]
    pitch = SSM_L + STRIDE_PAD
    for c in range(n):
        for lg in range(n_lg):
            scr[lg, c * pitch:c * pitch + SSM_L, :] = u[c * SSM_L:(c + 1) * SSM_L, lg * LANES:(lg + 1) * LANES]
    for s in range(SSM_L):
        for lg in range(n_lg):
            o_ref[lg, :, s * LANES:(s + 1) * LANES] = scr[lg, pl.ds(s, n, stride=pitch), :].astype(BF16)


def _ssm_pre(x, mod_l, g_l, tm):
    bsz, seq, d = x.shape
    n_lg = d // LANES
    return pl.pallas_call(
        _ssm_pre_kernel,
        grid=(bsz, seq // tm),
        in_specs=[
            pl.BlockSpec((None, tm, d), lambda b, i: (b, i, 0)),
            pl.BlockSpec((None, 6, d), lambda b, i: (b, 0, 0)),
            pl.BlockSpec((4, d), lambda b, i: (0, 0)),
        ],
        out_specs=pl.BlockSpec((None, n_lg, tm // SSM_L, SSM_L * LANES), lambda b, i: (b, 0, i, 0)),
        out_shape=jax.ShapeDtypeStruct((bsz, n_lg, seq // SSM_L, SSM_L * LANES), BF16),
        scratch_shapes=[pltpu.VMEM((n_lg, (tm // SSM_L) * (SSM_L + STRIDE_PAD), LANES), F32)],
        compiler_params=_cparams(("parallel", "parallel")),
        name="ssm_pre",
    )(x, mod_l, g_l)


def _gelu_tanh(y):
    return 0.5 * y * (1.0 + jnp.tanh(math.sqrt(2.0 / math.pi) * (y + 0.044715 * (y * y * y))))


def _same_group(shape, row_shift, col_shift):
    row_g = (lax.broadcasted_iota(jnp.int32, shape, 0) >> row_shift) & (SSM_LG - 1)
    col_g = (lax.broadcasted_iota(jnp.int32, shape, 1) >> col_shift) & (SSM_LG - 1)
    return row_g == col_g


def _ssm_expand_weights(xw_ref, yw_ref, cc_ref, t1_ref, t1t_ref, win_s, wout_s, m8_s, mx_s):
    hl = SSM_L // 2
    in_mask = _same_group((LANES, t1_ref.shape[1]), 4, 6)
    out_mask = _same_group((t1_ref.shape[1], LANES), 6, 4)
    k_mask = _same_group((LANES, LANES), 4, 4)
    m8_s[...] = jnp.zeros_like(m8_s)
    for s in range(SSM_L):
        rows = slice(s * LANES, (s + 1) * LANES)
        val = lax.dot_general(xw_ref[s], t1_ref[...], (((0,), (0,)), ((), ())), preferred_element_type=F32)
        win_s[rows, :] = jnp.where(in_mask, val, 0.0).astype(BF16)
        val = lax.dot_general(t1t_ref[...], yw_ref[s], (((1,), (1,)), ((), ())), preferred_element_type=F32)
        wout_s[:, rows] = jnp.where(out_mask, val, 0.0).astype(BF16)
    for tau in range(SSM_L):
        k_tau = lax.dot_general(xw_ref[SSM_L - 1 - tau], cc_ref[...], (((0,), (1,)), ((), ())),
                                preferred_element_type=F32)
        kbd = jnp.where(k_mask, k_tau, 0.0).astype(BF16)
        for a in range(hl):
            if a + tau < hl:
                m8_s[a * LANES:(a + 1) * LANES, (a + tau) * LANES:(a + tau + 1) * LANES] = kbd
            b = a + tau - hl
            if 0 <= b < hl:
                mx_s[a * LANES:(a + 1) * LANES, b * LANES:(b + 1) * LANES] = kbd


def _ssm_kernel(u_ref, xw_ref, yw_ref, cc_ref, t1_ref, t1t_ref, a_ref, d_ref, o_ref,
                win_s, wout_s, m8_s, mx_s, v_scr, h_scr):
    n_chunks = u_ref.shape[0]
    half = (SSM_L // 2) * LANES
    ns = a_ref.shape[1]

    @pl.when(pl.program_id(1) == 0)
    def _():
        _ssm_expand_weights(xw_ref, yw_ref, cc_ref, t1_ref, t1t_ref, win_s, wout_s, m8_s, mx_s)

    uc = u_ref[...]
    v_scr[...] = _dot(uc, win_s[...])
    ar = a_ref[0:1, :]
    ai = a_ref[1:2, :]

    def step(c, carry):
        hr, hi = carry
        h_scr[pl.ds(c, 1), pl.ds(0, ns)] = hr
        h_scr[pl.ds(c, 1), pl.ds(ns, ns)] = hi
        vr = v_scr[pl.ds(c, 1), pl.ds(0, ns)]
        vi = v_scr[pl.ds(c, 1), pl.ds(ns, ns)]
        return ar * hr - ai * hi + vr, ar * hi + ai * hr + vi

    zero = jnp.zeros((1, ns), F32)
    lax.fori_loop(0, n_chunks, step, (zero, zero))
    y = _dot(h_scr[...].astype(BF16), wout_s[...])
    u_lo = uc[:, :half]
    u_hi = uc[:, half:]
    m8 = m8_s[...]
    y_lo = y[:, :half] + _dot(u_lo, m8)
    y_hi = y[:, half:] + _dot(u_hi, m8) + _dot(u_lo, mx_s[...])
    dd = d_ref[...]
    o_ref[:, :half] = _gelu_tanh(y_lo + dd[:, :half] * u_lo.astype(F32)).astype(BF16)
    o_ref[:, half:] = _gelu_tanh(y_hi + dd[:, half:] * u_hi.astype(F32)).astype(BF16)


def _ssm(ucat, xw, yw, cc, t1, t1t, a16, dcat):
    bsz, n_lg, n_chunks, width = ucat.shape
    ns2 = t1.shape[-1]
    half = width // 2

    def per_lg(a):
        return pl.BlockSpec((None,) + a.shape[1:], lambda g, b: (g,) + (0,) * (a.ndim - 1))

    def per_lg_axis1(a):
        return pl.BlockSpec((a.shape[0], None) + a.shape[2:], lambda g, b: (0, g, 0, 0))

    def const(a):
        return pl.BlockSpec(a.shape, lambda g, b: (0, 0))

    return pl.pallas_call(
        _ssm_kernel,
        grid=(n_lg, bsz),
        in_specs=[
            pl.BlockSpec((None, None, n_chunks, width), lambda g, b: (b, g, 0, 0)),
            per_lg_axis1(xw), per_lg_axis1(yw), per_lg(cc), const(t1), const(t1t), per_lg(a16), per_lg(dcat),
        ],
        out_specs=pl.BlockSpec((None, None, n_chunks, width), lambda g, b: (b, g, 0, 0)),
        out_shape=jax.ShapeDtypeStruct(ucat.shape, BF16),
        scratch_shapes=[
            pltpu.VMEM((width, ns2), BF16), pltpu.VMEM((ns2, width), BF16),
            pltpu.VMEM((half, half), BF16), pltpu.VMEM((half, half), BF16),
            pltpu.VMEM((n_chunks, ns2), F32), pltpu.VMEM((n_chunks, ns2), F32),
        ],
        compiler_params=_cparams(("arbitrary", "arbitrary")),
        name="ssm_scan",
    )(ucat, xw, yw, cc, t1, t1t, a16, dcat)


def _ssm_post_kernel(z_ref, x_ref, w_ref, mod_ref, g_ref, o_ref, zp, rp):
    n_lg, n, _ = z_ref.shape
    d = x_ref.shape[1]
    for s in range(SSM_L):
        for lg in range(n_lg):
            zp[s * n:(s + 1) * n, lg * LANES:(lg + 1) * LANES] = z_ref[lg, :, s * LANES:(s + 1) * LANES]
    gl = _dot(zp[...], w_ref[...])
    h = gl[:, :d] * _sigmoid(gl[:, d:])
    r = mod_ref[2:3, :] * _rms(h, g_ref[1:2, :])
    pitch = n + STRIDE_PAD
    for s in range(SSM_L):
        for lg in range(n_lg):
            rp[lg, s * pitch:s * pitch + n, :] = r[s * n:(s + 1) * n, lg * LANES:(lg + 1) * LANES]
    sub = 8
    for c in range(n):
        for k in range(SSM_L // sub):
            rows = pl.ds(c * SSM_L + k * sub, sub)
            for lg in range(n_lg):
                cols = slice(lg * LANES, (lg + 1) * LANES)
                o_ref[rows, cols] = x_ref[rows, cols] + rp[lg, pl.ds(k * sub * pitch + c, sub, stride=pitch), :]


def _ssm_post(zcat, x, w_glu, mod_l, g_l, tm):
    bsz, seq, d = x.shape
    n_lg = d // LANES
    return pl.pallas_call(
        _ssm_post_kernel,
        grid=(bsz, seq // tm),
        in_specs=[
            pl.BlockSpec((None, n_lg, tm // SSM_L, SSM_L * LANES), lambda b, i: (b, 0, i, 0)),
            pl.BlockSpec((None, tm, d), lambda b, i: (b, i, 0)),
            pl.BlockSpec((d, 2 * d), lambda b, i: (0, 0)),
            pl.BlockSpec((None, 6, d), lambda b, i: (b, 0, 0)),
            pl.BlockSpec((4, d), lambda b, i: (0, 0)),
        ],
        out_specs=pl.BlockSpec((None, tm, d), lambda b, i: (b, i, 0)),
        out_shape=jax.ShapeDtypeStruct(x.shape, F32),
        scratch_shapes=[pltpu.VMEM((tm, d), BF16),
                        pltpu.VMEM((n_lg, SSM_L * (tm // SSM_L + STRIDE_PAD), LANES), F32)],
        compiler_params=_cparams(("parallel", "parallel")),
        name="ssm_post",
    )(zcat, x, w_glu, mod_l, g_l)


def _ssm_weights(a_re, a_im, log_dt, b_re, b_im, c_re, c_im, d_skip):
    n_groups = a_re.shape[0]
    n_lg = n_groups // SSM_LG
    p, hh, ll = SSM_STATE, SSM_GROUP, SSM_L
    dt = jnp.exp(log_dt)[:, None]
    k = jnp.arange(ll + 1, dtype=F32)[:, None, None]
    mag = jnp.exp(k * (a_re * dt))
    pw_re = mag * jnp.cos(k * (a_im * dt))
    pw_im = mag * jnp.sin(k * (a_im * dt))
    den = a_re * a_re + a_im * a_im
    num_re, num_im = pw_re[1] - 1.0, pw_im[1]
    f_re = (num_re * a_re + num_im * a_im) / den
    f_im = (num_im * a_re - num_re * a_im) / den
    bb_re = f_re[..., None] * b_re - f_im[..., None] * b_im
    bb_im = f_re[..., None] * b_im + f_im[..., None] * b_re
    b2_re, b2_im = (x.reshape(n_lg, SSM_LG, p, hh).transpose(0, 2, 1, 3).reshape(n_lg, p, LANES)
                    for x in (bb_re, bb_im))

    def lanes_g(x):
        x = x.reshape(x.shape[0], n_lg, SSM_LG, p).transpose(0, 1, 3, 2)
        return jnp.repeat(x, hh, axis=-1)

    l_re, l_im = lanes_g(pw_re[:ll][::-1]), lanes_g(pw_im[:ll][::-1])
    xw = jnp.concatenate([l_re * b2_re - l_im * b2_im, l_re * b2_im + l_im * b2_re], axis=2)
    c1 = jnp.concatenate([c_re, -c_im], axis=-1)
    c2 = jnp.concatenate([-c_im, -c_re], axis=-1)
    p1 = jnp.concatenate([pw_re[1:], pw_re[1:]], axis=-1)[:, :, None, :]
    p2 = jnp.concatenate([pw_im[1:], pw_im[1:]], axis=-1)[:, :, None, :]
    yw = (c1[None] * p1 + c2[None] * p2).reshape(ll, n_lg, LANES, 2 * p)
    cc = c1.reshape(n_lg, LANES, 2 * p)
    j1 = jnp.arange(2 * SSM_LG * p)
    t1 = (jnp.arange(2 * p)[:, None] == ((j1 // (SSM_LG * p)) * p + j1 % p)[None, :]).astype(BF16)
    a16 = jnp.stack([pw_re[ll], pw_im[ll]], axis=0).reshape(2, n_lg, SSM_LG * p).transpose(1, 0, 2)
    dcat = jnp.tile(d_skip.reshape(n_lg, 1, LANES), (1, 1, ll))
    return xw.astype(BF16), yw.astype(BF16), cc.astype(BF16), t1, t1.T, a16, dcat


def _ffn_kernel(x_ref, wg_ref, wu_ref, wo_ref, mod_ref, g_ref, o_ref, h_scr, acc):
    k = pl.program_id(2)

    @pl.when(k == 0)
    def _():
        h = _rms(x_ref[...], g_ref[2:3, :]) * (1.0 + mod_ref[4:5, :]) + mod_ref[3:4, :]
        h_scr[...] = h.astype(BF16)
        acc[...] = jnp.zeros_like(acc)

    h = h_scr[...]
    g = _dot(h, wg_ref[...])
    u = _dot(h, wu_ref[...])
    act = (g * _sigmoid(g) * u).astype(BF16)
    acc[...] += _dot(act, wo_ref[...])

    @pl.when(k == pl.num_programs(2) - 1)
    def _():
        o_ref[...] = x_ref[...] + mod_ref[5:6, :] * _rms(acc[...], g_ref[3:4, :])


def _ffn(x, w_in, w_out, mod_l, g_l, tm, tf):
    bsz, seq, d = x.shape
    f = w_out.shape[0]
    nk = f // tf
    return pl.pallas_call(
        _ffn_kernel,
        grid=(bsz, seq // tm, nk),
        in_specs=[
            pl.BlockSpec((None, tm, d), lambda b, i, k: (b, i, 0)),
            pl.BlockSpec((d, tf), lambda b, i, k: (0, k)),
            pl.BlockSpec((d, tf), lambda b, i, k: (0, k + nk)),
            pl.BlockSpec((tf, d), lambda b, i, k: (k, 0)),
            pl.BlockSpec((None, 6, d), lambda b, i, k: (b, 0, 0)),
            pl.BlockSpec((4, d), lambda b, i, k: (0, 0)),
        ],
        out_specs=pl.BlockSpec((None, tm, d), lambda b, i, k: (b, i, 0)),
        out_shape=jax.ShapeDtypeStruct(x.shape, F32),
        scratch_shapes=[pltpu.VMEM((tm, d), BF16), pltpu.VMEM((tm, d), F32)],
        compiler_params=_cparams(("parallel", "parallel", "arbitrary")),
        name="ffn",
    )(x, w_in, w_in, w_out, mod_l, g_l)


def _rope_inputs(positions):
    bsz, seq = positions.shape
    half = MLA_ROPE // 2
    per_row = LANES // half
    inv_freq = ROPE_THETA ** (-jnp.arange(half, dtype=F32) / half)
    pos_rep = jnp.broadcast_to(positions[..., None], (bsz, seq, half)).reshape(bsz, seq // per_row, LANES)
    return pos_rep, jnp.tile(inv_freq, per_row).reshape(1, LANES)


def _rope_tables(pos_ref, f_ref, cos_scr, sin_lo_scr, sin_hi_scr):
    half = MLA_ROPE // 2
    per_row = LANES // half
    ang = pos_ref[...].astype(F32) * f_ref[...]
    cos_d = jnp.cos(ang)
    sin_d = jnp.sin(ang)
    n = ang.shape[0]
    lane = lax.broadcasted_iota(jnp.int32, ang.shape, 1)

    def onto(t, first_lane, q):
        shift = (first_lane - half * q) % LANES
        return t if shift == 0 else pltpu.roll(t, shift, axis=1)

    for q in range(per_row):
        rows = pl.ds(q, n, stride=per_row)
        x1, x2 = MLA_NOPE, MLA_NOPE + half
        cos_scr[rows, :] = jnp.where(lane < x1, 1.0, jnp.where(lane < x2, onto(cos_d, x1, q),
                                     jnp.where(lane < MLA_QK, onto(cos_d, x2, q), 0.0)))
        sin_lo_scr[rows, :] = jnp.where(lane < x1, 0.0, jnp.where(lane < x2, -onto(sin_d, x1, q), 0.0))
        sin_hi_scr[rows, :] = jnp.where(lane < x2, 0.0, jnp.where(lane < MLA_QK, onto(sin_d, x2, q), 0.0))


def _mla_proj_kernel(x_ref, pos_ref, f_ref, mod_ref, g_ref, wcq_ref, wckv_ref, wkr_ref,
                     qn_ref, kvn_ref, wq_ref, wk_ref, wv_ref, q_ref, k_ref, v_ref,
                     cos_scr, sin_lo_scr, sin_hi_scr):
    h = (_rms(x_ref[...], g_ref[0:1, :]) * (1.0 + mod_ref[1:2, :]) + mod_ref[0:1, :]).astype(BF16)
    cq = _rms(_dot(h, wcq_ref[...]), qn_ref[...]).astype(BF16)
    ckv = _rms(_dot(h, wckv_ref[...]), kvn_ref[...]).astype(BF16)
    kr = _dot(h, wkr_ref[...])
    _rope_tables(pos_ref, f_ref, cos_scr, sin_lo_scr, sin_hi_scr)
    cos = cos_scr[...]
    sin_lo = sin_lo_scr[...]
    sin_hi = sin_hi_scr[...]
    half = MLA_ROPE // 2

    def rope(t):
        width = t.shape[1]
        up = pltpu.roll(t, half, axis=1)
        down = pltpu.roll(t, width - half, axis=1)
        return [t[:, s:s + LANES] * cos + down[:, s:s + LANES] * sin_lo + up[:, s:s + LANES] * sin_hi
                for s in range(0, width, LANES)]

    q = _dot(cq, wq_ref[...])
    for hd, q_hd in enumerate(rope(q)):
        q_ref[:, hd * LANES:(hd + 1) * LANES] = q_hd.astype(BF16)
    kr_rot = rope(kr)[0]
    k = _dot(ckv, wk_ref[...])
    for hd in range(MLA_HEADS):
        sl = slice(hd * LANES, (hd + 1) * LANES)
        k_ref[:, sl] = (k[:, sl] + kr_rot).astype(BF16)
    vt = lax.dot_general(wv_ref[...], ckv, (((1,), (1,)), ((), ())), preferred_element_type=F32)
    row = lax.broadcasted_iota(jnp.int32, vt.shape, 0)
    v_ref[...] = jnp.where(row % V_ROWS == MLA_V, 1.0, vt).astype(BF16)


def _mla_proj(x, rope_t, mod_l, g_l, wts, tm):
    bsz, seq, d = x.shape
    hw = MLA_HEADS * LANES

    def full(a):
        return pl.BlockSpec(a.shape, lambda b, i: (0,) * a.ndim)

    out_sds = jax.ShapeDtypeStruct((bsz, seq, hw), BF16)
    vt_rows = MLA_HEADS * V_ROWS
    vt_sds = jax.ShapeDtypeStruct((bsz, seq // tm, vt_rows, tm), BF16)
    return pl.pallas_call(
        _mla_proj_kernel,
        grid=(bsz, seq // tm),
        in_specs=[
            pl.BlockSpec((None, tm, d), lambda b, i: (b, i, 0)),
            pl.BlockSpec((None, tm // (LANES // (MLA_ROPE // 2)), LANES), lambda b, i: (b, i, 0)),
            pl.BlockSpec((1, LANES), lambda b, i: (0, 0)),
            pl.BlockSpec((None, 6, d), lambda b, i: (b, 0, 0)),
            pl.BlockSpec((4, d), lambda b, i: (0, 0)),
        ] + [full(w) for w in wts],
        out_specs=[pl.BlockSpec((None, tm, hw), lambda b, i: (b, i, 0))] * 2
        + [pl.BlockSpec((None, None, vt_rows, tm), lambda b, i: (b, i, 0, 0))],
        out_shape=[out_sds, out_sds, vt_sds],
        scratch_shapes=[pltpu.VMEM((tm, LANES), F32)] * 3,
        compiler_params=_cparams(("parallel", "parallel")),
        name="mla_proj",
    )(x, *rope_t, mod_l, g_l, *wts)


def _mla_weights(w_in, q_norm, kv_norm, w_uq, w_ukv, w_o):
    pad = LANES - MLA_QK
    w_cq = w_in[:, :MLA_Q_RANK]
    w_ckv = w_in[:, MLA_Q_RANK:MLA_Q_RANK + MLA_KV_RANK]
    w_kr = jnp.pad(w_in[:, MLA_Q_RANK + MLA_KV_RANK:], ((0, 0), (MLA_NOPE, pad)))
    scale = MLA_QK ** -0.5 * math.log2(math.e)
    wq = w_uq.reshape(MLA_Q_RANK, MLA_HEADS, MLA_QK) * scale
    wq_pad = jnp.pad(wq, ((0, 0), (0, 0), (0, pad))).reshape(MLA_Q_RANK, MLA_HEADS * LANES)
    wkv =w_ukv.reshape(MLA_KV_RANK, MLA_HEADS, MLA_NOPE + MLA_V)
    zk = jnp.zeros((MLA_KV_RANK, MLA_HEADS, LANES - MLA_NOPE), F32)
    wk_pad = jnp.concatenate([wkv[..., :MLA_NOPE], zk], axis=-1).reshape(MLA_KV_RANK, MLA_HEADS * LANES)
    wv_t = jnp.pad(wkv[..., MLA_NOPE:].transpose(1, 2, 0), ((0, 0), (0, V_ROWS - MLA_V), (0, 0)))
    wv_t = wv_t.reshape(MLA_HEADS * V_ROWS, MLA_KV_RANK)
    proj_w = (w_cq.astype(BF16), w_ckv.astype(BF16), w_kr.astype(BF16),
              q_norm.reshape(1, -1), kv_norm.reshape(1, -1),
              wq_pad.astype(BF16), wk_pad.astype(BF16), wv_t.astype(BF16))
    return proj_w, w_o.astype(BF16)


def _attn_kernel(q_ref, k_ref, vt_ref, o_ref, m_scr, mt_scr, acc, s0_scr, s1_scr, qt_scr):
    tk = vt_ref.shape[2]
    n_sub = q_ref.shape[0] // tk
    qi = pl.program_id(2)
    s_scr = (s0_scr, s1_scr)
    qt_scr[...] = q_ref[...].astype(F32).T.astype(BF16)
    m_scr[...] = jnp.full_like(m_scr, -jnp.inf)
    acc[...] = jnp.zeros_like(acc)

    def scores(hd, j, q0=0):
        lanes = slice(hd * LANES, (hd + 1) * LANES)
        k = k_ref[pl.ds(pl.multiple_of(j * tk, tk), tk), lanes]
        st = _dot(k, qt_scr[lanes, q0:])
        s_scr[hd][:, q0:] = st
        mt_scr[hd, :, q0:] = jnp.max(st, axis=0, keepdims=True)

    def accumulate(hd, j, diag=None):
        q0 = 0 if diag is None else diag * tk
        st = s_scr[hd][:, q0:]
        if diag is not None:
            key = lax.broadcasted_iota(jnp.int32, st.shape, 0)
            qry = lax.broadcasted_iota(jnp.int32, st.shape, 1)
            st = jnp.where(key <= qry, st, -jnp.inf)
            mt = jnp.max(st, axis=0, keepdims=True)
        else:
            mt = mt_scr[hd]
        m_prev = m_scr[hd, :, q0:]
        m_new = jnp.maximum(m_prev, mt)
        alpha = jnp.exp2(m_prev - m_new)
        pt = jnp.exp2(st - m_new).astype(BF16)
        vt = vt_ref[j, hd * V_ROWS:(hd + 1) * V_ROWS, :]
        acc[hd, :, q0:] = alpha * acc[hd, :, q0:] + _dot(vt, pt)
        m_scr[hd, :, q0:] = m_new

    scores(0, 0)

    def body(j):
        scores(1, j)
        accumulate(0, j)
        scores(0, j + 1)
        accumulate(1, j)

    def body_pair(jj, carry):
        body(2 * jj)
        body(2 * jj + 1)
        return carry

    lax.fori_loop(0, (n_sub // 2) * qi, body_pair, 0)
    d0 = n_sub * qi
    for d in range(n_sub):
        scores(1, d0 + d, d * tk)
        accumulate(0, d0 + d, diag=d)
        if d + 1 < n_sub:
            scores(0, d0 + d + 1, (d + 1) * tk)
        accumulate(1, d0 + d, diag=d)
    outs = []
    for hd in range(2):
        a = acc[hd]
        outs.append(a[:MLA_V, :] / a[MLA_V:MLA_V + 1, :])
    o_ref[...] = jnp.concatenate(outs, axis=0).T.astype(BF16)


def _attention(q, k, vt, tq):
    bsz, seq, hw = q.shape
    n_pairs = hw // (2 * LANES)
    n_kt, _, tk = vt.shape[1:]
    assert tq % (2 * tk) == 0, "the main loop consumes kv tiles in pairs"
    return pl.pallas_call(
        _attn_kernel,
        grid=(bsz, n_pairs, seq // tq),
        in_specs=[
            pl.BlockSpec((None, tq, 2 * LANES), lambda b, h, i: (b, i, h)),
            pl.BlockSpec((None, seq, 2 * LANES), lambda b, h, i: (b, 0, h)),
            pl.BlockSpec((None, n_kt, 2 * V_ROWS, tk), lambda b, h, i: (b, 0, h, 0)),
        ],
        out_specs=pl.BlockSpec((None, tq, 2 * MLA_V), lambda b, h, i: (b, i, h)),
        out_shape=jax.ShapeDtypeStruct((bsz, seq, n_pairs * 2 * MLA_V), BF16),
        scratch_shapes=[pltpu.VMEM((2, 1, tq), F32), pltpu.VMEM((2, 1, tq), F32), pltpu.VMEM((2, V_ROWS, tq), F32),
                        pltpu.VMEM((tk, tq), F32), pltpu.VMEM((tk, tq), F32), pltpu.VMEM((2 * LANES, tq), BF16)],
        compiler_params=_cparams(("parallel", "parallel", "arbitrary")),
        name="mla_attention",
    )(q, k, vt)


def _attn_out_kernel(a_ref, x_ref, w_ref, mod_ref, g_ref, o_ref):
    h = _dot(a_ref[...], w_ref[...])
    o_ref[...] = x_ref[...] + mod_ref[2:3, :] * _rms(h, g_ref[1:2, :])


def _attn_out(a, x, wo_pad, mod_l, g_l, tm):
    bsz, seq, d = x.shape
    hw = a.shape[-1]
    return pl.pallas_call(
        _attn_out_kernel,
        grid=(bsz, seq // tm),
        in_specs=[
            pl.BlockSpec((None, tm, hw), lambda b, i: (b, i, 0)),
            pl.BlockSpec((None, tm, d), lambda b, i: (b, i, 0)),
            pl.BlockSpec((hw, d), lambda b, i: (0, 0)),
            pl.BlockSpec((None, 6, d), lambda b, i: (b, 0, 0)),
            pl.BlockSpec((4, d), lambda b, i: (0, 0)),
        ],
        out_specs=pl.BlockSpec((None, tm, d), lambda b, i: (b, i, 0)),
        out_shape=jax.ShapeDtypeStruct(x.shape, F32),
        compiler_params=_cparams(("parallel", "parallel")),
        name="mla_out",
    )(a, x, wo_pad, mod_l, g_l)


def _route(logits):
    row = lax.broadcasted_iota(jnp.int32, logits.shape, 0)
    n = logits.shape[0]
    neg = -jnp.inf
    m1 = jnp.max(logits, axis=0, keepdims=True)
    i1 = jnp.min(jnp.where(logits == m1, row, n), axis=0, keepdims=True)
    rest = jnp.where(row == i1, neg, logits)
    m2 = jnp.max(rest, axis=0, keepdims=True)
    i2 = jnp.min(jnp.where(rest == m2, row, n), axis=0, keepdims=True)
    e = jnp.exp(m2 - m1)
    w1 = 1.0 / (1.0 + e)
    w2 = e / (1.0 + e)
    first = row == i1
    second = row == i2
    comb = jnp.where(first, w1, 0.0) + jnp.where(second, w2, 0.0)
    sel = jnp.where(first, 1.0, 0.0) + jnp.where(second, 1.0, 0.0)
    return comb, sel


def _moe_kernel(x_ref, wr_ref, br_ref, tri_ref, wg_ref, wu_ref, wo_ref, mod_ref, g_ref, o_ref,
                h_scr, sel_scr, rank_scr, w_scr, acc):
    e = pl.program_id(1)
    tb = x_ref.shape[0]

    @pl.when(e == 0)
    def _():
        h = _rms(x_ref[...], g_ref[2:3, :]) * (1.0 + mod_ref[4:5, :]) + mod_ref[3:4, :]
        h_hi = h.astype(BF16)
        h_lo = (h - h_hi.astype(F32)).astype(BF16)
        nt = (((1,), (1,)), ((), ()))
        logits = (lax.dot_general(wr_ref[0], h_hi, nt, preferred_element_type=F32)
                  + lax.dot_general(wr_ref[1], h_hi, nt, preferred_element_type=F32)
                  + lax.dot_general(wr_ref[0], h_lo, nt, preferred_element_type=F32))
        n_e = sel_scr.shape[0]
        comb, sel = _route(logits[:n_e, :] + br_ref[:n_e, 0:1])
        sel_scr[...] = sel
        w_scr[...] = comb
        sel_pad = jnp.concatenate([sel, jnp.zeros_like(sel)], axis=0).astype(BF16)
        rank_scr[...] = _dot(sel_pad, tri_ref[...])[:n_e, :]
        h_scr[...] = h_hi
        acc[...] = jnp.zeros_like(acc)

    sel_row = sel_scr[pl.ds(e, 1), :]
    rank_row = rank_scr[pl.ds(e, 1), :]
    w_row = w_scr[pl.ds(e, 1), :]
    n_rows = jnp.sum(sel_row).astype(jnp.int32)

    def step(r, carry):
        slot = (lax.broadcasted_iota(jnp.int32, (MOE_SUB, tb), 0) + r * MOE_SUB).astype(F32)
        hit = rank_row == slot
        onehot = jnp.where(hit, sel_row, 0.0).astype(BF16)
        xs = _dot(onehot, h_scr[...]).astype(BF16)
        g = _dot(xs, wg_ref[...])
        u = _dot(xs, wu_ref[...])
        act = (g * _sigmoid(g) * u).astype(BF16)
        y = _dot(act, wo_ref[...])
        w_col = jnp.sum(jnp.where(hit, w_row, 0.0), axis=1, keepdims=True)
        yw = (y * w_col).astype(BF16)
        acc[...] += lax.dot_general(onehot, yw, (((0,), (0,)), ((), ())), preferred_element_type=F32)
        return carry

    lax.fori_loop(0, (n_rows + MOE_SUB - 1) // MOE_SUB, step, 0)

    @pl.when(e == pl.num_programs(1) - 1)
    def _():
        o_ref[...] = x_ref[...] + mod_ref[5:6, :] * _rms(acc[...], g_ref[3:4, :])


def _moe_layer(x, w_router, b_router, w_in, w_out, mod_l, g_l):
    bsz, seq, d = x.shape
    n_e, f, _ = w_out.shape
    tb = min(MOE_TB, seq)
    blocks_per_batch = seq // tb
    wr = jnp.pad(w_router.T, ((0, LANES - n_e), (0, 0)))
    wr_hi = wr.astype(BF16)
    wr = jnp.stack([wr_hi, (wr - wr_hi.astype(F32)).astype(BF16)])
    br = jnp.broadcast_to(jnp.pad(b_router, (0, LANES - n_e))[:, None], (LANES, LANES))
    pos = jnp.arange(tb)
    tri = (pos[:, None] < pos[None, :]).astype(BF16)
    w_in = w_in.astype(BF16)
    out = pl.pallas_call(
        _moe_kernel,
        grid=(bsz * blocks_per_batch, n_e),
        in_specs=[
            pl.BlockSpec((tb, d), lambda i, e: (i, 0)),
            pl.BlockSpec((2, LANES, d), lambda i, e: (0, 0, 0)),
            pl.BlockSpec((LANES, LANES), lambda i, e: (0, 0)),
            pl.BlockSpec((tb, tb), lambda i, e: (0, 0)),
            pl.BlockSpec((None, d, f), lambda i, e: (e, 0, 0)),
            pl.BlockSpec((None, d, f), lambda i, e: (e, 0, 1)),
            pl.BlockSpec((None, f, d), lambda i, e: (e, 0, 0)),
            pl.BlockSpec((None, 6, d), lambda i, e: (i // blocks_per_batch, 0, 0)),
            pl.BlockSpec((4, d), lambda i, e: (0, 0)),
        ],
        out_specs=pl.BlockSpec((tb, d), lambda i, e: (i, 0)),
        out_shape=jax.ShapeDtypeStruct((bsz * seq, d), F32),
        scratch_shapes=[pltpu.VMEM((tb, d), BF16), pltpu.VMEM((n_e, tb), F32), pltpu.VMEM((n_e, tb), F32),
                        pltpu.VMEM((n_e, tb), F32), pltpu.VMEM((tb, d), F32)],
        compiler_params=_cparams(("parallel", "arbitrary")),
        name="moe",
    )(x.reshape(bsz * seq, d), wr, br, tri, w_in, w_in, w_out.astype(BF16), mod_l, g_l)
    return out.reshape(bsz, seq, d)


def kernel(x, c, positions, norm_g, w_ada, b_ada, ssm_a_re, ssm_a_im, ssm_log_dt, ssm_b_re, ssm_b_im, ssm_c_re, ssm_c_im, ssm_d, ssm_w_glu, ffn_w_in, ffn_w_out, mla_w_in, mla_q_norm, mla_kv_norm, mla_w_uq, mla_w_ukv, mla_w_o, moe_w_router, moe_b_router, moe_w_in, moe_w_out):
    depth = norm_g.shape[0]
    seq = x.shape[1]
    tm = min(512, seq)
    mod = _ada(c, w_ada, b_ada)
    rope_t = None
    for i in range(depth):
        j = i // 2
        mod_l, g_l = mod[i], norm_g[i]
        if i % 2 == 0:
            ssm_w = _ssm_weights(ssm_a_re[j], ssm_a_im[j], ssm_log_dt[j], ssm_b_re[j], ssm_b_im[j],
                                 ssm_c_re[j], ssm_c_im[j], ssm_d[j])
            ucat = _ssm_pre(x, mod_l, g_l, min(IO_TM, seq))
            zcat = _ssm(ucat, *ssm_w)
            x = _ssm_post(zcat, x, ssm_w_glu[j].astype(BF16), mod_l, g_l, min(IO_TM, seq))
            f = ffn_w_out.shape[1]
            x = _ffn(x, ffn_w_in[j].astype(BF16), ffn_w_out[j].astype(BF16), mod_l, g_l, min(FFN_TM, seq), f)
        else:
            if rope_t is None:
                rope_t = _rope_inputs(positions)
            proj_w, wo_pad = _mla_weights(mla_w_in[j], mla_q_norm[j], mla_kv_norm[j],
                                          mla_w_uq[j], mla_w_ukv[j], mla_w_o[j])
            q, k, v = _mla_proj(x, rope_t, mod_l, g_l, proj_w, tm)
            a = _attention(q, k, v, min(ATTN_TQ, seq))
            x = _attn_out(a, x, wo_pad, mod_l, g_l, min(IO_TM, seq))
            x = _moe_layer(x, moe_w_router[j], moe_b_router[j], moe_w_in[j], moe_w_out[j], mod_l, g_l)
    return x
```
